```python
import math
import jax, jax.numpy as jnp
from jax import lax
import numpy as np

D_MODEL = 4096
BATCH = 32
SEQ = 256
DEPTH = 1
DEC_BATCH = 4
DEC_SEQ = 4096
PAST_LEN = 512

GRID_W = 64
ATT_WIDTH = D_MODEL // 2
V_HD = 256
QK_HD = V_HD // 2
N_ATT_HEADS = ATT_WIDTH // V_HD
AXIS_DIM = QK_HD // 2
ROPE_BASE = 10000.0
Q_BLOCK = 128
D_INNER = D_MODEL - ATT_WIDTH
SSM_HD = 64
N_SSM_HEADS = D_INNER // SSM_HD
N_GROUPS = 4
HEADS_PER_GROUP = N_SSM_HEADS // N_GROUPS
D_STATE = 128
CONV_K = 5
CHUNK = 128
D_FF = 4 * D_MODEL
RMS_EPS = 1e-6

Q_COLS = N_ATT_HEADS * 2 * QK_HD
K_COLS = N_ATT_HEADS * 2 * QK_HD
V_COLS = N_ATT_HEADS * V_HD
Z_COLS = D_INNER
XBC_COLS = D_INNER + 2 * N_GROUPS * D_STATE
DT_COLS = 2 * N_SSM_HEADS
IN_COLS = Q_COLS + K_COLS + V_COLS + Z_COLS + XBC_COLS + DT_COLS

kernel_name = "hymba_diffattn_bissd_sandwich_adaln"

F32 = jnp.float32


def rmsnorm(x, g, eps=RMS_EPS):
    xf = x.astype(F32)
    y = xf * lax.rsqrt(jnp.mean(xf * xf, axis=-1, keepdims=True) + eps)
    return y.astype(x.dtype) * g


def axial_rope(x):
    L = x.shape[1]
    rows = L // GRID_W
    row = jnp.repeat(jnp.arange(rows, dtype=F32), GRID_W)
    col = jnp.tile(jnp.arange(GRID_W, dtype=F32), rows)
    inv = 1.0 / (ROPE_BASE ** (jnp.arange(0, AXIS_DIM, 2, dtype=F32) / AXIS_DIM))
    half = AXIS_DIM // 2

    def rot(xa, pos):
        ang = pos[:, None] * inv[None, :]
        cos = jnp.cos(ang)[None, :, None, None, :].astype(x.dtype)
        sin = jnp.sin(ang)[None, :, None, None, :].astype(x.dtype)
        x1, x2 = xa[..., :half], xa[..., half:]
        return jnp.concatenate([x1 * cos - x2 * sin, x1 * sin + x2 * cos], axis=-1)

    return jnp.concatenate([rot(x[..., :AXIS_DIM], row), rot(x[..., AXIS_DIM:], col)], axis=-1)


def diff_attention(q, k, v, lam):
    b, Lq = q.shape[:2]
    nb = Lq // Q_BLOCK
    qb = jnp.swapaxes(q.reshape(b, nb, Q_BLOCK, N_ATT_HEADS, 2, QK_HD), 0, 1)
    scale = QK_HD ** -0.5

    def block(qi):
        s = jnp.einsum('bqhjd,bkhjd->bhjqk', qi, k).astype(F32) * scale
        pr = jax.nn.softmax(s, axis=-1)
        a = pr[:, :, 0] - lam * pr[:, :, 1]
        return jnp.einsum('bhqk,bkhv->bqhv', a.astype(v.dtype), v)

    o = lax.map(block, qb)
    return jnp.swapaxes(o, 0, 1).reshape(b, Lq, N_ATT_HEADS, V_HD)


def centred_conv(x, w, bias):
    L = x.shape[1]
    pad = CONV_K // 2
    xp = jnp.pad(x, ((0, 0), (pad, pad), (0, 0)))
    out = bias
    for j in range(CONV_K):
        out = out + w[j] * xp[:, j:j + L]
    return out


def ssd_scan(x, dt, A, Bm, Cm, init_state):
    b, L, G, HG, P = x.shape
    N = Bm.shape[-1]
    nc = L // CHUNK
    xf = x.astype(F32).reshape(b, nc, CHUNK, G, HG, P)
    dtc = dt.reshape(b, nc, CHUNK, G, HG)
    Bc = Bm.astype(F32).reshape(b, nc, CHUNK, G, N)
    Cc = Cm.astype(F32).reshape(b, nc, CHUNK, G, N)
    cs = jnp.cumsum(dtc * A, axis=2)
    seg = cs[:, :, :, None] - cs[:, :, None, :]
    causal = jnp.tril(jnp.ones((CHUNK, CHUNK), dtype=bool))[:, :, None, None]
    Lm = jnp.where(causal, jnp.exp(jnp.where(causal, seg, 0.0)), 0.0)
    cb = jnp.einsum('bclgn,bcsgn->bclsg', Cc, Bc)
    wts = cb[..., None] * Lm * dtc[:, :, None]
    y_diag = jnp.einsum('bclsgh,bcsghp->bclghp', wts, xf)
    decay_s = jnp.exp(cs[:, :, -1:] - cs)
    states = jnp.einsum('bcsgn,bcsghp->bcghpn', Bc, xf * (decay_s * dtc)[..., None])
    chunk_decay = jnp.exp(cs[:, :, -1])

    def step(carry, inp):
        st, dec = inp
        return carry * dec[..., None, None] + st, carry

    final, prev = lax.scan(step, init_state.astype(F32),
                           (jnp.moveaxis(states, 1, 0), jnp.moveaxis(chunk_decay, 1, 0)))
    prev = jnp.moveaxis(prev, 0, 1)
    y_off = jnp.einsum('bclgn,bcghpn->bclghp', Cc, prev) * jnp.exp(cs)[..., None]
    return (y_diag + y_off).reshape(b, L, G, HG, P), final


def token_mixers(h, p, layer_idx, ctx_k, ctx_v, init_f, init_b):
    latent = ctx_k is not None
    b, L, _ = h.shape
    proj = h @ p['w_in']
    o1 = Q_COLS
    o2 = o1 + K_COLS
    o3 = o2 + V_COLS
    o4 = o3 + Z_COLS
    o5 = o4 + XBC_COLS
    q = proj[..., :o1].reshape(b, L, N_ATT_HEADS, 2, QK_HD)
    k = proj[..., o1:o2].reshape(b, L, N_ATT_HEADS, 2, QK_HD)
    v = proj[..., o2:o3].reshape(b, L, N_ATT_HEADS, V_HD)
    z = proj[..., o3:o4]
    xbc = proj[..., o4:o5]
    dt_raw = proj[..., o5:]

    if latent:
        q_use = axial_rope(q)
        k_all = jnp.concatenate([ctx_k, axial_rope(k)], axis=1)
        v_all = jnp.concatenate([ctx_v, v], axis=1)
    else:
        q_use, k_all, v_all = q, k, v
    lam_init = 0.8 - 0.6 * math.exp(-0.3 * layer_idx)
    lam = (jnp.exp(jnp.sum(p['lq1'].astype(F32) * p['lk1'].astype(F32)))
           - jnp.exp(jnp.sum(p['lq2'].astype(F32) * p['lk2'].astype(F32))) + lam_init)
    o_att = diff_attention(q_use, k_all, v_all, lam)
    o_att = (rmsnorm(o_att, p['g_subln']) * (1.0 - lam_init)).reshape(b, L, ATT_WIDTH)

    xbc = jax.nn.silu(centred_conv(xbc, p['conv_w'], p['conv_b']))
    xs = xbc[..., :D_INNER].reshape(b, L, N_GROUPS, HEADS_PER_GROUP, SSM_HD)
    Bm = xbc[..., D_INNER:D_INNER + N_GROUPS * D_STATE].reshape(b, L, N_GROUPS, D_STATE)
    Cm = xbc[..., D_INNER + N_GROUPS * D_STATE:].reshape(b, L, N_GROUPS, D_STATE)
    dt = jax.nn.softplus(dt_raw.astype(F32).reshape(b, L, 2, N_GROUPS, HEADS_PER_GROUP)
                         + p['dt_bias'].astype(F32).reshape(2, N_GROUPS, HEADS_PER_GROUP))
    A = -jnp.exp(p['a_log'].astype(F32)).reshape(2, N_GROUPS, HEADS_PER_GROUP)
    y_f, fin_f = ssd_scan(xs, dt[:, :, 0], A[0], Bm, Cm, init_f)
    y_b, fin_b = ssd_scan(jnp.flip(xs, 1), jnp.flip(dt[:, :, 1], 1), A[1],
                          jnp.flip(Bm, 1), jnp.flip(Cm, 1), init_b)
    y = (y_f + jnp.flip(y_b, 1)).astype(h.dtype) \
        + p['d_skip'].reshape(N_GROUPS, HEADS_PER_GROUP)[..., None] * xs
    y = y.reshape(b, L, D_INNER) * jax.nn.silu(z)
    y = rmsnorm(y.reshape(b, L, N_GROUPS, D_INNER // N_GROUPS),
                p['g_ssm_norm'].reshape(N_GROUPS, D_INNER // N_GROUPS)).reshape(b, L, D_INNER)

    out = jnp.concatenate([o_att, y], axis=-1) @ p['w_out']
    return out, (k, v, fin_f, fin_b)


def trunk_layer(x, c_vec, p, layer_idx, ctx_k, ctx_v, init_f, init_b):
    mod = (jax.nn.silu(c_vec) @ p['w_ada'] + p['b_ada'])[:, None, :]
    sh_a, sc_a, g_a, sh_m, sc_m, g_m = jnp.split(mod, 6, axis=-1)
    h = rmsnorm(x, p['g_mix_pre']) * (1.0 + sc_a) + sh_a
    mix, ctx_t = token_mixers(h, p, layer_idx, ctx_k, ctx_v, init_f, init_b)
    x = x + g_a * rmsnorm(mix, p['g_mix_post'])
    h = rmsnorm(x, p['g_mlp_pre']) * (1.0 + sc_m) + sh_m
    m = jnp.square(jax.nn.relu(h @ p['w_up'])) @ p['w_down']
    x = x + g_m * rmsnorm(m, p['g_mlp_post'])
    return x, ctx_t


def setup_inputs(seed: int = 0) -> dict:
    key = jax.random.key(seed)
    ks = jax.random.split(key, 32)
    nrm = lambda k, shape, s=1.0: jax.random.normal(k, shape, F32) * s
    dt0 = jnp.exp(jax.random.uniform(ks[20], (DEPTH, 2, N_SSM_HEADS), F32)
                  * (math.log(0.1) - math.log(0.001)) + math.log(0.001))
    return {
        'x_prompt': nrm(ks[0], (BATCH, SEQ, D_MODEL)),
        'x_sample': nrm(ks[1], (DEC_BATCH, DEC_SEQ, D_MODEL)),
        'c': nrm(ks[2], (DEC_BATCH, D_MODEL)),
        'cache_k': nrm(ks[3], (DEC_BATCH, DEPTH, PAST_LEN, N_ATT_HEADS, 2, QK_HD)),
        'cache_v': nrm(ks[4], (DEC_BATCH, DEPTH, PAST_LEN, N_ATT_HEADS, V_HD)),
        'state_ssm_fwd': nrm(ks[5], (DEC_BATCH, DEPTH, N_SSM_HEADS, SSM_HD, D_STATE), 0.5),
        'state_ssm_bwd': nrm(ks[6], (DEC_BATCH, DEPTH, N_SSM_HEADS, SSM_HD, D_STATE), 0.5),
        'c_ctx': nrm(ks[7], (D_MODEL,)),
        'w_ada': nrm(ks[8], (DEPTH, D_MODEL, 6 * D_MODEL), 0.5 * D_MODEL ** -0.5),
        'b_ada': nrm(ks[9], (DEPTH, 6 * D_MODEL), 0.02),
        'g_mix_pre': 1.0 + nrm(ks[10], (DEPTH, D_MODEL), 0.02),
        'g_mix_post': 1.0 + nrm(ks[11], (DEPTH, D_MODEL), 0.02),
        'g_mlp_pre': 1.0 + nrm(ks[12], (DEPTH, D_MODEL), 0.02),
        'g_mlp_post': 1.0 + nrm(ks[13], (DEPTH, D_MODEL), 0.02),
        'w_in': nrm(ks[14], (DEPTH, D_MODEL, IN_COLS), D_MODEL ** -0.5),
        'lambda_q1': nrm(ks[15], (DEPTH, QK_HD), 0.1),
        'lambda_k1': nrm(ks[16], (DEPTH, QK_HD), 0.1),
        'lambda_q2': nrm(ks[17], (DEPTH, QK_HD), 0.1),
        'lambda_k2': nrm(ks[18], (DEPTH, QK_HD), 0.1),
        'g_subln': 1.0 + nrm(ks[19], (DEPTH, V_HD), 0.02),
        'conv_w': nrm(ks[21], (DEPTH, CONV_K, XBC_COLS), CONV_K ** -0.5),
        'conv_b': nrm(ks[22], (DEPTH, XBC_COLS), 0.02),
        'a_log': jnp.log(jax.random.uniform(ks[23], (DEPTH, 2, N_SSM_HEADS), F32, 1.0, 16.0)),
        'dt_bias': dt0 + jnp.log(-jnp.expm1(-dt0)),
        'd_skip': 1.0 + nrm(ks[24], (DEPTH, N_SSM_HEADS), 0.1),
        'g_ssm_norm': 1.0 + nrm(ks[25], (DEPTH, D_INNER), 0.02),
        'w_out': nrm(ks[26], (DEPTH, D_MODEL, D_MODEL), D_MODEL ** -0.5),
        'w_up': nrm(ks[27], (DEPTH, D_MODEL, D_FF), D_MODEL ** -0.5),
        'w_down': nrm(ks[28], (DEPTH, D_FF, D_MODEL), D_FF ** -0.5),
    }


def reference(x_prompt, x_sample, c, cache_k, cache_v, state_ssm_fwd, state_ssm_bwd, c_ctx,
              w_ada, b_ada, g_mix_pre, g_mix_post, g_mlp_pre, g_mlp_post, w_in,
              lambda_q1, lambda_k1, lambda_q2, lambda_k2, g_subln, conv_w, conv_b,
              a_log, dt_bias, d_skip, g_ssm_norm, w_out, w_up, w_down):
    xp = x_prompt
    xs = x_sample
    bp = xp.shape[0]
    bd = xs.shape[0]
    sshape = (N_GROUPS, HEADS_PER_GROUP, SSM_HD, D_STATE)
    new_k, new_v, new_sf, new_sb = [], [], [], []
    for l in range(DEPTH):
        p = dict(w_ada=w_ada[l], b_ada=b_ada[l], g_mix_pre=g_mix_pre[l], g_mix_post=g_mix_post[l],
                 g_mlp_pre=g_mlp_pre[l], g_mlp_post=g_mlp_post[l], w_in=w_in[l],
                 lq1=lambda_q1[l], lk1=lambda_k1[l], lq2=lambda_q2[l], lk2=lambda_k2[l],
                 g_subln=g_subln[l], conv_w=conv_w[l], conv_b=conv_b[l], a_log=a_log[l],
                 dt_bias=dt_bias[l], d_skip=d_skip[l], g_ssm_norm=g_ssm_norm[l],
                 w_out=w_out[l], w_up=w_up[l], w_down=w_down[l])
        zero = jnp.zeros((bp,) + sshape, F32)
        xp, (k_c, v_c, sf, sb) = trunk_layer(xp, c_ctx[None, :], p, l, None, None, zero, zero)
        new_k.append(k_c)
        new_v.append(v_c)
        new_sf.append(sf.reshape(bp, N_SSM_HEADS, SSM_HD, D_STATE).astype(xp.dtype))
        new_sb.append(sb.reshape(bp, N_SSM_HEADS, SSM_HD, D_STATE).astype(xp.dtype))
        xs, _ = trunk_layer(xs, c, p, l, cache_k[:, l], cache_v[:, l],
                            state_ssm_fwd[:, l].reshape((bd,) + sshape),
                            state_ssm_bwd[:, l].reshape((bd,) + sshape))
    return (xp, xs, jnp.stack(new_k, axis=1), jnp.stack(new_v, axis=1),
            jnp.stack(new_sf, axis=1), jnp.stack(new_sb, axis=1))
```

```python
import functools
import math

import jax
import jax.numpy as jnp
from jax import lax
from jax.experimental import pallas as pl
from jax.experimental.pallas import tpu as pltpu

F32 = jnp.float32
BF16 = jnp.bfloat16

GRID_W = 64
V_HD = 256
QK_HD = V_HD // 2
AXIS_DIM = QK_HD // 2
ROPE_BASE = 10000.0
SSM_HD = 64
N_GROUPS = 4
D_STATE = 128
CONV_K = 5
CHUNK = 128
RMS_EPS = 1e-6
LAM_INIT = 0.8 - 0.6 * math.exp(-0.3 * 0)

V7X_VMEM_BYTES = 64 * 1024 * 1024
VMEM_LIMIT = V7X_VMEM_BYTES - 8 * 1024 * 1024
LANES = 128
SUBLANES = 8
CONV_HALO = SUBLANES
NORM_ROWS = 16
MOD_ROWS = 8


def _pick(n, prefs):
    for p in prefs:
        if n % p == 0:
            return p
    return n


def _params(sem):
    return pltpu.CompilerParams(dimension_semantics=sem, vmem_limit_bytes=VMEM_LIMIT)


def _sigmoid(x):
    return 1.0 / (1.0 + jnp.exp(-x))


def _silu(x):
    return x * _sigmoid(x)


def _softplus(x):
    return jnp.maximum(x, 0.0) + jnp.log1p(jnp.exp(-jnp.abs(x)))


def _rms_rows(x, eps=RMS_EPS):
    return x * lax.rsqrt(jnp.mean(x * x, axis=-1, keepdims=True) + eps)


def _ada_kernel(c_ref, w_ref, b_ref, o_ref):
    c = c_ref[...]
    s = _silu(c).astype(BF16)
    o_ref[...] = jnp.dot(s, w_ref[...].astype(BF16), preferred_element_type=F32) + b_ref[...]


def _ada(cvec, w_ada, b_ada):
    rows, d = cvec.shape
    n = w_ada.shape[1]
    tn = _pick(n, (512, 256, 128))
    return pl.pallas_call(
        _ada_kernel,
        grid=(n // tn,),
        in_specs=[pl.BlockSpec((rows, d), lambda j: (0, 0)),
                  pl.BlockSpec((d, tn), lambda j: (0, j)),
                  pl.BlockSpec((1, tn), lambda j: (0, j))],
        out_specs=pl.BlockSpec((rows, tn), lambda j: (0, j)),
        out_shape=jax.ShapeDtypeStruct((rows, n), F32),
        compiler_params=_params(("arbitrary",)),
        name="ada_mod",
    )(cvec, w_ada, b_ada.reshape(1, n))


def _row_tile(m, mod_rows, prefs=(512, 256, 128)):
    return _pick(math.gcd(m, mod_rows[1]), prefs)


def _mod_spec(d, which, mod_rows, tm):
    row0, per = mod_rows
    return pl.BlockSpec((None, None, 1, d), lambda i, j: (row0 + (i * tm) // per, which, 0, 0))


def _in_proj_kernel(x_ref, g_ref, sc_ref, sh_ref, w_ref, wdt_ref, *refs, bounds):
    outs = refs[:len(bounds)]
    dt_ref = refs[len(bounds)]
    h_ref = refs[len(bounds) + 1]
    j = pl.program_id(1)

    @pl.when(j == 0)
    def _():
        g = g_ref[...]
        mul = 1.0 + sc_ref[...]
        sh = sh_ref[...]

        def body(r, carry):
            rows = pl.ds(pl.multiple_of(r * NORM_ROWS, NORM_ROWS), NORM_ROWS)
            y = _rms_rows(x_ref[rows, :]) * g
            h_ref[rows, :] = (y * mul + sh).astype(BF16)
            return carry

        lax.fori_loop(0, x_ref.shape[0] // NORM_ROWS, body, 0)
        dt_ref[...] = jnp.dot(h_ref[...], wdt_ref[...], preferred_element_type=F32)

    for (lo, hi), o_ref in zip(bounds, outs):
        @pl.when((j >= lo) & (j < hi))
        def _(o_ref=o_ref):
            o_ref[...] = jnp.dot(h_ref[...], w_ref[...], preferred_element_type=F32)


def _in_proj(x, mod, g, w_main, w_dt, widths, mod_rows):
    m, d = x.shape
    tm = _row_tile(m, mod_rows)
    tn = _pick(math.gcd(*widths), (512, 256, 128))
    bounds, lo = [], 0
    for wd in widths:
        bounds.append((lo, lo + wd // tn))
        lo += wd // tn
    nj = lo

    def out_spec(lo_, hi_):
        return pl.BlockSpec((tm, tn), lambda i, j: (i, jnp.clip(j - lo_, 0, hi_ - lo_ - 1)))

    out_specs = [out_spec(*b) for b in bounds] + [pl.BlockSpec((tm, LANES), lambda i, j: (i, 0))]
    out_shape = [jax.ShapeDtypeStruct((m, wd), F32) for wd in widths]
    out_shape.append(jax.ShapeDtypeStruct((m, LANES), F32))
    return pl.pallas_call(
        functools.partial(_in_proj_kernel, bounds=tuple(bounds)),
        grid=(m // tm, nj),
        in_specs=[pl.BlockSpec((tm, d), lambda i, j: (i, 0), pipeline_mode=pl.Buffered(1)),
                  pl.BlockSpec((1, d), lambda i, j: (0, 0)),
                  _mod_spec(d, 1, mod_rows, tm),
                  _mod_spec(d, 0, mod_rows, tm),
                  pl.BlockSpec((d, tn), lambda i, j: (0, j)),
                  pl.BlockSpec((d, LANES), lambda i, j: (0, 0))],
        out_specs=out_specs,
        out_shape=out_shape,
        scratch_shapes=[pltpu.VMEM((tm, d), BF16)],
        compiler_params=_params(("arbitrary", "arbitrary")),
        name="in_proj",
    )(x, g.reshape(1, d), mod, mod, w_main, w_dt)


def _rope_tile(x, cos, sin_signed, first_half):
    half = AXIS_DIM // 2
    swapped = jnp.where(first_half, pltpu.roll(x, LANES - half, axis=1), pltpu.roll(x, half, axis=1))
    return x * cos + swapped * sin_signed


def _prep_kernel(*refs, n_ctx_tiles, rope):
    idx = 0
    ctx_ref = None
    if n_ctx_tiles:
        ctx_ref = refs[idx]; idx += 1
    x_ref = refs[idx]; idx += 1
    if rope:
        cos_ref, sin_ref = refs[idx], refs[idx + 1]; idx += 2
    o_ref = refs[idx]
    i = pl.program_id(1)

    def new_rows():
        if not rope:
            o_ref[...] = x_ref[...].astype(BF16)
            return
        cos = cos_ref[...]
        sin = sin_ref[...]
        lane = lax.broadcasted_iota(jnp.int32, cos.shape, 1)
        first_half = (lane % AXIS_DIM) < (AXIS_DIM // 2)
        for c in range(x_ref.shape[1] // LANES):
            cols = slice(c * LANES, (c + 1) * LANES)
            o_ref[:, cols] = _rope_tile(x_ref[:, cols], cos, sin, first_half).astype(BF16)

    if n_ctx_tiles:
        @pl.when(i < n_ctx_tiles)
        def _():
            o_ref[...] = ctx_ref[...].astype(BF16)

        @pl.when(i >= n_ctx_tiles)
        def _():
            new_rows()
    else:
        new_rows()


def _prep(x, ctx, tables):
    b, l, w = x.shape
    p = 0 if ctx is None else ctx.shape[1]
    tm = _pick(math.gcd(l, p) if p else l, (512, 256, 128))
    nct = p // tm
    in_specs, args = [], []
    if nct:
        in_specs.append(pl.BlockSpec((None, tm, w), lambda bi, i: (bi, jnp.minimum(i, nct - 1), 0)))
        args.append(ctx)
    in_specs.append(pl.BlockSpec((None, tm, w), lambda bi, i: (bi, jnp.maximum(i - nct, 0), 0)))
    args.append(x)
    if tables is not None:
        for t in tables:
            in_specs.append(pl.BlockSpec((tm, LANES), lambda bi, i: (jnp.maximum(i - nct, 0), 0)))
            args.append(t)
    return pl.pallas_call(
        functools.partial(_prep_kernel, n_ctx_tiles=nct, rope=tables is not None),
        grid=(b, (p + l) // tm),
        in_specs=in_specs,
        out_specs=pl.BlockSpec((None, tm, w), lambda bi, i: (bi, i, 0)),
        out_shape=jax.ShapeDtypeStruct((b, p + l, w), BF16),
        compiler_params=_params(("arbitrary", "arbitrary")),
        name="qkv_prep",
    )(*args)


def _rope_tables(l):
    rows = l // GRID_W
    row = jnp.repeat(jnp.arange(rows, dtype=F32), GRID_W)
    col = jnp.tile(jnp.arange(GRID_W, dtype=F32), rows)
    inv = 1.0 / (ROPE_BASE ** (jnp.arange(0, AXIS_DIM, 2, dtype=F32) / AXIS_DIM))

    def one(pos):
        ang = pos[:, None] * inv[None, :]
        c, s = jnp.cos(ang), jnp.sin(ang)
        return jnp.concatenate([c, c], axis=-1), jnp.concatenate([-s, s], axis=-1)

    cr, sr = one(row)
    cc, sc = one(col)
    return jnp.concatenate([cr, cc], axis=-1), jnp.concatenate([sr, sc], axis=-1)


def _attn_kernel(lq1_ref, lk1_ref, lq2_ref, lk2_ref, g_ref, q_ref, k_ref, v_ref, o_ref, *, heads):
    lam = (jnp.exp(jnp.sum(lq1_ref[...] * lk1_ref[...], axis=-1, keepdims=True))
           - jnp.exp(jnp.sum(lq2_ref[...] * lk2_ref[...], axis=-1, keepdims=True)) + LAM_INIT)
    scale = QK_HD ** -0.5
    for h in range(heads):
        base = h * V_HD
        probs = []
        for half in range(2):
            cols = slice(base + half * QK_HD, base + (half + 1) * QK_HD)
            q = q_ref[:, cols].astype(BF16)
            k = k_ref[:, cols].astype(BF16)
            s = lax.dot_general(q, k, (((1,), (1,)), ((), ())), preferred_element_type=F32) * scale
            e = jnp.exp(s - jnp.max(s, axis=-1, keepdims=True))
            probs.append(e * (1.0 / jnp.sum(e, axis=-1, keepdims=True)))
        a = (probs[0] - lam * probs[1]).astype(BF16)
        v = v_ref[:, base:base + V_HD].astype(BF16)
        o = jnp.dot(a, v, preferred_element_type=F32)
        o_ref[:, base:base + V_HD] = ((_rms_rows(o) * g_ref[...]) * (1.0 - LAM_INIT)).astype(BF16)


def _attention(q, k, v, lams, g_subln, heads_per_step, tq):
    b, lq, w = q.shape
    lk = k.shape[1]
    hw = heads_per_step * V_HD
    small = pl.BlockSpec((1, QK_HD), lambda bi, hi, qi: (0, 0))
    return pl.pallas_call(
        functools.partial(_attn_kernel, heads=heads_per_step),
        grid=(b, w // hw, lq // tq),
        in_specs=[small, small, small, small,
                  pl.BlockSpec((1, V_HD), lambda bi, hi, qi: (0, 0)),
                  pl.BlockSpec((None, tq, hw), lambda bi, hi, qi: (bi, qi, hi)),
                  pl.BlockSpec((None, lk, hw), lambda bi, hi, qi: (bi, 0, hi)),
                  pl.BlockSpec((None, lk, hw), lambda bi, hi, qi: (bi, 0, hi))],
        out_specs=pl.BlockSpec((None, tq, hw), lambda bi, hi, qi: (bi, qi, hi)),
        out_shape=jax.ShapeDtypeStruct((b, lq, w), BF16),
        compiler_params=_params(("arbitrary", "arbitrary", "arbitrary")),
        name="diff_attn",
    )(*lams, g_subln.reshape(1, V_HD), q, k, v)


def _conv_silu(prev_ref, main_ref, next_ref, w_ref, b_ref, has_prev, has_next):
    n = main_ref.shape[0]
    prev = jnp.where(has_prev, prev_ref[...], 0.0)
    nxt = jnp.where(has_next, next_ref[...], 0.0)
    win = jnp.concatenate([prev, main_ref[...], nxt], axis=0)
    total = win.shape[0]
    pad = CONV_K // 2
    out = b_ref[...]
    for j in range(CONV_K):
        d = j - pad
        shifted = win if d == 0 else pltpu.roll(win, (total - d) % total, axis=0)
        out = out + w_ref[j:j + 1, :] * shifted[CONV_HALO:CONV_HALO + n, :]
    return _silu(out)


def _ssd_kernel(*refs, nc, heads, has_init):
    (xs_p, xs_m, xs_n, bc_p, bc_m, bc_n, dt_ref, z_ref, cwx_ref, cbx_ref, cwbc_ref, cbbc_ref,
     alog_ref, dtb_ref, dskip_ref, gn_ref) = refs[:16]
    idx = 16
    if has_init:
        initf_ref, initb_ref = refs[idx], refs[idx + 1]
        idx += 2
    y_ref, sf_ref, sb_ref, carf_ref, carb_ref, prev_ref = refs[idx:idx + 6]

    phase = pl.program_id(1)
    i = pl.program_id(2)
    c = jnp.where(phase == 0, i, nc - 1 - i)
    has_prev = c > 0
    has_next = c < nc - 1
    gw = heads // N_GROUPS * SSM_HD
    hpg = heads // N_GROUPS

    @pl.when((phase == 0) & (i == 0))
    def _():
        if has_init:
            carf_ref[...] = initf_ref[...]
            carb_ref[...] = initb_ref[...]
        else:
            carf_ref[...] = jnp.zeros_like(carf_ref)
            carb_ref[...] = jnp.zeros_like(carb_ref)

    dt = _softplus(dt_ref[...] + dtb_ref[...])
    a = dt * (-jnp.exp(alog_ref[...]))
    row = lax.broadcasted_iota(jnp.int32, (CHUNK, CHUNK), 0)
    colm = lax.broadcasted_iota(jnp.int32, (CHUNK, CHUNK), 1)
    cs = a
    step = 1
    while step < CHUNK:
        cs = cs + jnp.where(row >= step, pltpu.roll(cs, step, axis=0), 0.0)
        step *= 2
    total = cs[CHUNK - 1:CHUNK, :]
    lane = lax.broadcasted_iota(jnp.int32, (CHUNK, LANES), 1)
    fwd_lane = lane < heads
    cum = jnp.where(fwd_lane, cs, total - cs + a)
    tot = total
    cum_t = cum.T
    dt_t = dt.T
    tot_t = jnp.broadcast_to(tot, (CHUNK, LANES)).T
    w_t = jnp.exp(tot_t - cum_t) * dt_t
    e_t = jnp.exp(cum_t)
    dec_t = jnp.exp(tot_t)

    xs = _conv_silu(xs_p, xs_m, xs_n, cwx_ref, cbx_ref, has_prev, has_next)
    bc = _conv_silu(bc_p, bc_m, bc_n, cwbc_ref, cbbc_ref, has_prev, has_next)
    x_t = xs.T

    def state_update(car_ref, off):
        for g in range(N_GROUPS):
            b_g = bc[:, g * D_STATE:(g + 1) * D_STATE].astype(BF16)
            rows = []
            for hh in range(hpg):
                h = g * hpg + hh
                rows.append(x_t[h * SSM_HD:(h + 1) * SSM_HD, :] * w_t[off + h:off + h + 1, :])
            xw = jnp.concatenate(rows, axis=0).astype(BF16)
            st = jnp.dot(xw, b_g, preferred_element_type=F32)
            for hh in range(hpg):
                h = g * hpg + hh
                r = slice(h * SSM_HD, (h + 1) * SSM_HD)
                car_ref[r, :] = car_ref[r, :] * dec_t[off + h:off + h + 1, :] + st[hh * SSM_HD:(hh + 1) * SSM_HD, :]

    @pl.when(phase == 0)
    def _():
        prev_ref[c] = carf_ref[...].astype(BF16)
        state_update(carf_ref, 0)

        @pl.when(i == nc - 1)
        def _():
            sf_ref[...] = carf_ref[...]

    @pl.when(phase == 1)
    def _():
        causal = colm >= row
        anti = colm <= row
        y_parts = []
        for g in range(N_GROUPS):
            b_g = bc[:, g * D_STATE:(g + 1) * D_STATE].astype(BF16)
            c_g = bc[:, (N_GROUPS + g) * D_STATE:(N_GROUPS + g + 1) * D_STATE].astype(BF16)
            cb_t = lax.dot_general(b_g, c_g, (((1,), (1,)), ((), ())), preferred_element_type=F32)
            rg = slice(g * gw, (g + 1) * gw)
            yo_f = lax.dot_general(prev_ref[c, rg, :], c_g, (((1,), (1,)), ((), ())),
                                   preferred_element_type=F32)
            yo_b = lax.dot_general(carb_ref[rg, :].astype(BF16), c_g, (((1,), (1,)), ((), ())),
                                   preferred_element_type=F32)
            for hh in range(hpg):
                h = g * hpg + hh
                hb = heads + h
                x_h = x_t[h * SSM_HD:(h + 1) * SSM_HD, :]
                seg_f = cum_t[h:h + 1, :] - cum[:, h:h + 1]
                seg_b = cum_t[hb:hb + 1, :] - cum[:, hb:hb + 1]
                w_f = cb_t * jnp.exp(jnp.where(causal, seg_f, -1e4))
                w_b = cb_t * jnp.exp(jnp.where(anti, seg_b, -1e4))
                lhs = jnp.concatenate([x_h * dt_t[h:h + 1, :], x_h * dt_t[hb:hb + 1, :]], axis=1)
                rhs = jnp.concatenate([w_f, w_b], axis=0)
                yd = jnp.dot(lhs.astype(BF16), rhs.astype(BF16), preferred_element_type=F32)
                r = slice(hh * SSM_HD, (hh + 1) * SSM_HD)
                y_parts.append(yd + yo_f[r, :] * e_t[h:h + 1, :] + yo_b[r, :] * e_t[hb:hb + 1, :])
        y = jnp.concatenate(y_parts, axis=0).T
        y = y + dskip_ref[...] * xs
        y = y * _silu(z_ref[...])
        for g in range(N_GROUPS):
            cols = slice(g * gw, (g + 1) * gw)
            y_ref[:, cols] = (_rms_rows(y[:, cols]) * gn_ref[:, cols]).astype(BF16)
        state_update(carb_ref, heads)

        @pl.when(i == nc - 1)
        def _():
            sb_ref[...] = carb_ref[...]


def _ssd(xs, bc, dt, z, p, init_f, init_b):
    b, l, dx = xs.shape
    heads = dx // SSM_HD
    nbc = bc.shape[2]
    nc = l // CHUNK
    hpc = CHUNK // CONV_HALO
    nhalo = l // CONV_HALO
    has_init = init_f is not None

    def chunk_of(ph, i):
        return jnp.where(ph == 0, i, nc - 1 - i)

    def main(wd):
        return pl.BlockSpec((None, CHUNK, wd), lambda bi, ph, i: (bi, chunk_of(ph, i), 0))

    def halo_prev(wd):
        return pl.BlockSpec((None, CONV_HALO, wd),
                            lambda bi, ph, i: (bi, jnp.maximum(chunk_of(ph, i) * hpc - 1, 0), 0))

    def halo_next(wd):
        return pl.BlockSpec((None, CONV_HALO, wd),
                            lambda bi, ph, i: (bi, jnp.minimum((chunk_of(ph, i) + 1) * hpc, nhalo - 1), 0))

    def full(shape):
        return pl.BlockSpec(shape, lambda bi, ph, i: (0,) * len(shape))

    state_spec = pl.BlockSpec((None, dx, D_STATE), lambda bi, ph, i: (bi, 0, 0))
    in_specs = [halo_prev(dx), main(dx), halo_next(dx), halo_prev(nbc), main(nbc), halo_next(nbc),
                main(LANES), main(dx),
                full((CONV_K, dx)), full((1, dx)), full((CONV_K, nbc)), full((1, nbc)),
                full((1, LANES)), full((1, LANES)), full((1, dx)), full((1, dx))]
    args = [xs, xs, xs, bc, bc, bc, dt, z,
            p['conv_w_x'], p['conv_b_x'], p['conv_w_bc'], p['conv_b_bc'],
            p['a_log'], p['dt_bias'], p['d_skip'], p['g_ssm_norm']]
    if has_init:
        in_specs += [state_spec, state_spec]
        args += [init_f, init_b]
    y_spec = pl.BlockSpec((None, CHUNK, dx), lambda bi, ph, i: (bi, jnp.where(ph == 0, nc - 1, nc - 1 - i), 0))
    return pl.pallas_call(
        functools.partial(_ssd_kernel, nc=nc, heads=heads, has_init=has_init),
        grid=(b, 2, nc),
        in_specs=in_specs,
        out_specs=[y_spec, state_spec, state_spec],
        out_shape=[jax.ShapeDtypeStruct((b, l, dx), BF16),
                   jax.ShapeDtypeStruct((b, dx, D_STATE), F32),
                   jax.ShapeDtypeStruct((b, dx, D_STATE), F32)],
        scratch_shapes=[pltpu.VMEM((dx, D_STATE), F32), pltpu.VMEM((dx, D_STATE), F32),
                        pltpu.VMEM((nc, dx, D_STATE), BF16)],
        compiler_params=_params(("arbitrary", "arbitrary", "arbitrary")),
        name="conv_ssd",
    )(*args)


def _out_proj_kernel(a_ref, y_ref, w_ref, x_ref, gpost_ref, gate_ref, gpre_ref, sc_ref, sh_ref,
                     x1_ref, h2_ref, *, nk_a, nk):
    k = pl.program_id(1)

    @pl.when(k == 0)
    def _():
        x1_ref[...] = jnp.dot(a_ref[...], w_ref[...], preferred_element_type=F32)

    @pl.when((k > 0) & (k < nk_a))
    def _():
        x1_ref[...] += jnp.dot(a_ref[...], w_ref[...], preferred_element_type=F32)

    @pl.when(k >= nk_a)
    def _():
        x1_ref[...] += jnp.dot(y_ref[...], w_ref[...], preferred_element_type=F32)

    @pl.when(k == nk - 1)
    def _():
        gpost = gpost_ref[...]
        gate = gate_ref[...]
        gpre = gpre_ref[...]
        mul = 1.0 + sc_ref[...]
        sh = sh_ref[...]

        def body(r, carry):
            rows = pl.ds(pl.multiple_of(r * NORM_ROWS, NORM_ROWS), NORM_ROWS)
            x1 = x_ref[rows, :] + gate * (_rms_rows(x1_ref[rows, :]) * gpost)
            x1_ref[rows, :] = x1
            h2_ref[rows, :] = ((_rms_rows(x1) * gpre) * mul + sh).astype(BF16)
            return carry

        lax.fori_loop(0, x1_ref.shape[0] // NORM_ROWS, body, 0)


def _out_proj(att, y, w_out, x, mod, g_post, g_pre, mod_rows):
    m, wa = att.shape
    d = x.shape[1]
    tm = _row_tile(m, mod_rows)
    tk = _pick(math.gcd(wa, d - wa), (512, 256, 128))
    nk_a, nk = wa // tk, d // tk
    vec = pl.BlockSpec((1, d), lambda i, k: (0, 0))
    return pl.pallas_call(
        functools.partial(_out_proj_kernel, nk_a=nk_a, nk=nk),
        grid=(m // tm, nk),
        in_specs=[pl.BlockSpec((tm, tk), lambda i, k: (i, jnp.minimum(k, nk_a - 1))),
                  pl.BlockSpec((tm, tk), lambda i, k: (i, jnp.maximum(k - nk_a, 0))),
                  pl.BlockSpec((tk, d), lambda i, k: (k, 0)),
                  pl.BlockSpec((tm, d), lambda i, k: (i, 0), pipeline_mode=pl.Buffered(1)),
                  vec, _mod_spec(d, 2, mod_rows, tm), vec,
                  _mod_spec(d, 4, mod_rows, tm), _mod_spec(d, 3, mod_rows, tm)],
        out_specs=[pl.BlockSpec((tm, d), lambda i, k: (i, 0)),
                   pl.BlockSpec((tm, d), lambda i, k: (i, 0))],
        out_shape=[jax.ShapeDtypeStruct((m, d), F32), jax.ShapeDtypeStruct((m, d), BF16)],
        compiler_params=_params(("arbitrary", "arbitrary")),
        name="out_proj",
    )(att, y, w_out, x, g_post.reshape(1, d), mod, g_pre.reshape(1, d), mod, mod)


def _up_kernel(h_ref, w_ref, o_ref):
    u = jnp.maximum(jnp.dot(h_ref[...], w_ref[...], preferred_element_type=F32), 0.0)
    o_ref[...] = (u * u).astype(BF16)


def _up(h2, w_up):
    m, d = h2.shape
    f = w_up.shape[1]
    tm = _pick(m, (1024, 512, 256, 128))
    tn = _pick(f, (1024, 512, 256, 128))
    return pl.pallas_call(
        _up_kernel,
        grid=(m // tm, f // tn),
        in_specs=[pl.BlockSpec((tm, d), lambda i, j: (i, 0)),
                  pl.BlockSpec((d, tn), lambda i, j: (0, j))],
        out_specs=pl.BlockSpec((tm, tn), lambda i, j: (i, j)),
        out_shape=jax.ShapeDtypeStruct((m, f), BF16),
        compiler_params=_params(("arbitrary", "arbitrary")),
        name="mlp_up",
    )(h2, w_up)


def _down_kernel(u_ref, w_ref, x_ref, gpost_ref, gate_ref, o_ref, *, nk):
    k = pl.program_id(1)

    @pl.when(k == 0)
    def _():
        o_ref[...] = jnp.dot(u_ref[...], w_ref[...], preferred_element_type=F32)

    @pl.when(k > 0)
    def _():
        o_ref[...] += jnp.dot(u_ref[...], w_ref[...], preferred_element_type=F32)

    @pl.when(k == nk - 1)
    def _():
        gpost = gpost_ref[...]
        gate = gate_ref[...]

        def body(r, carry):
            rows = pl.ds(pl.multiple_of(r * NORM_ROWS, NORM_ROWS), NORM_ROWS)
            o_ref[rows, :] = x_ref[rows, :] + gate * (_rms_rows(o_ref[rows, :]) * gpost)
            return carry

        lax.fori_loop(0, o_ref.shape[0] // NORM_ROWS, body, 0)


def _down(u, w_down, x1, mod, g_post, mod_rows):
    m, f = u.shape
    d = x1.shape[1]
    tm = _row_tile(m, mod_rows)
    tk = _pick(f, (512, 256, 128))
    nk = f // tk
    return pl.pallas_call(
        functools.partial(_down_kernel, nk=nk),
        grid=(m // tm, nk),
        in_specs=[pl.BlockSpec((tm, tk), lambda i, k: (i, k)),
                  pl.BlockSpec((tk, d), lambda i, k: (k, 0)),
                  pl.BlockSpec((tm, d), lambda i, k: (i, 0), pipeline_mode=pl.Buffered(1)),
                  pl.BlockSpec((1, d), lambda i, k: (0, 0)),
                  _mod_spec(d, 5, mod_rows, tm)],
        out_specs=pl.BlockSpec((tm, d), lambda i, k: (i, 0)),
        out_shape=jax.ShapeDtypeStruct((m, d), F32),
        compiler_params=_params(("arbitrary", "arbitrary")),
        name="mlp_down",
    )(u, w_down, x1, g_post.reshape(1, d), mod)


def _trunk(x, mod, mod_row0, rows_per_mod, w, ctx_k, ctx_v, init_f, init_b):
    b, l, d = x.shape
    m = b * l
    att_w = d // 2
    d_inner = d - att_w
    n_bc = 2 * N_GROUPS * D_STATE
    latent = ctx_k is not None

    mod_rows = (mod_row0, rows_per_mod)
    x2d = x.reshape(m, d)
    widths = (att_w, att_w, att_w, d_inner, d_inner, n_bc)
    q, k, v, z, xs, bc, dt = _in_proj(x2d, mod, w['g_mix_pre'], w['w_in_main'], w['w_in_dt'], widths,
                                      mod_rows)
    q3, k3, v3 = (t.reshape(b, l, att_w) for t in (q, k, v))
    lams = (w['lq1'], w['lk1'], w['lq2'], w['lk2'])
    if latent:
        tables = _rope_tables(l)
        qa = _prep(q3, None, tables)
        ka = _prep(k3, ctx_k, tables)
        va = _prep(v3, ctx_v, None)
        att = _attention(qa, ka, va, lams, w['g_subln'], 1, _pick(l, (256, 128)))
    else:
        att = _attention(q3, k3, v3, lams, w['g_subln'], att_w // V_HD, l)
    y, fin_f, fin_b = _ssd(xs.reshape(b, l, d_inner), bc.reshape(b, l, n_bc), dt.reshape(b, l, LANES),
                           z.reshape(b, l, d_inner), w, init_f, init_b)
    x1, h2 = _out_proj(att.reshape(m, att_w), y.reshape(m, d_inner), w['w_out'], x2d, mod,
                       w['g_mix_post'], w['g_mlp_pre'], mod_rows)
    u = _up(h2, w['w_up'])
    x2 = _down(u, w['w_down'], x1, mod, w['g_mlp_post'], mod_rows)
    return x2.reshape(b, l, d), k, v, fin_f, fin_b


def kernel(x_prompt, x_sample, c, cache_k, cache_v, state_ssm_fwd, state_ssm_bwd, c_ctx, w_ada, b_ada,
           g_mix_pre, g_mix_post, g_mlp_pre, g_mlp_post, w_in, lambda_q1, lambda_k1, lambda_q2, lambda_k2,
           g_subln, conv_w, conv_b, a_log, dt_bias, d_skip, g_ssm_norm, w_out, w_up, w_down):
    bp, lp, d = x_prompt.shape
    bd, ld, _ = x_sample.shape
    depth = w_in.shape[0]
    assert depth == 1 and bd + 1 <= MOD_ROWS
    att_w = d // 2
    d_inner = d - att_w
    heads = d_inner // SSM_HD
    n_att = att_w // V_HD
    n_main = 3 * att_w + 2 * d_inner + 2 * N_GROUPS * D_STATE
    assert 2 * heads <= LANES and w_in.shape[2] == n_main + 2 * heads

    lyr = 0
    pad_l = LANES - 2 * heads
    w = dict(
        g_mix_pre=g_mix_pre[lyr], g_mix_post=g_mix_post[lyr], g_mlp_pre=g_mlp_pre[lyr],
        g_mlp_post=g_mlp_post[lyr],
        w_in_main=w_in[lyr, :, :n_main].astype(BF16),
        w_in_dt=jnp.pad(w_in[lyr, :, n_main:], ((0, 0), (0, pad_l))).astype(BF16),
        lq1=lambda_q1[lyr].reshape(1, QK_HD), lk1=lambda_k1[lyr].reshape(1, QK_HD),
        lq2=lambda_q2[lyr].reshape(1, QK_HD), lk2=lambda_k2[lyr].reshape(1, QK_HD),
        g_subln=g_subln[lyr],
        conv_w_x=conv_w[lyr, :, :d_inner], conv_w_bc=conv_w[lyr, :, d_inner:],
        conv_b_x=conv_b[lyr, :d_inner].reshape(1, -1), conv_b_bc=conv_b[lyr, d_inner:].reshape(1, -1),
        a_log=jnp.pad(a_log[lyr].reshape(1, -1), ((0, 0), (0, pad_l))),
        dt_bias=jnp.pad(dt_bias[lyr].reshape(1, -1), ((0, 0), (0, pad_l))),
        d_skip=jnp.repeat(d_skip[lyr], SSM_HD).reshape(1, -1),
        g_ssm_norm=g_ssm_norm[lyr].reshape(1, -1),
        w_out=w_out[lyr].astype(BF16), w_up=w_up[lyr].astype(BF16), w_down=w_down[lyr].astype(BF16),
    )

    cvec = jnp.concatenate([c_ctx[None, :], c, jnp.zeros((MOD_ROWS - 1 - bd, d), F32)], axis=0)
    mod = _ada(cvec, w_ada[lyr], b_ada[lyr]).reshape(MOD_ROWS, 6, 1, d)

    yp, k_c, v_c, sf, sb = _trunk(x_prompt, mod, 0, bp * lp, w, None, None, None, None)
    ys, _, _, _, _ = _trunk(
        x_sample, mod, 1, ld, w,
        cache_k[:, lyr].reshape(bd, -1, att_w), cache_v[:, lyr].reshape(bd, -1, att_w),
        state_ssm_fwd[:, lyr].reshape(bd, d_inner, D_STATE), state_ssm_bwd[:, lyr].reshape(bd, d_inner, D_STATE))

    return (yp, ys,
            k_c.reshape(bp, 1, lp, n_att, 2, QK_HD), v_c.reshape(bp, 1, lp, n_att, V_HD),
            sf.reshape(bp, 1, heads, SSM_HD, D_STATE), sb.reshape(bp, 1, heads, SSM_HD, D_STATE))
```

```python
import functools
import math

import jax
import jax.numpy as jnp
from jax import lax
from jax.experimental import pallas as pl
from jax.experimental.pallas import tpu as pltpu

F32 = jnp.float32
BF16 = jnp.bfloat16

GRID_W = 64
V_HD = 256
QK_HD = V_HD // 2
AXIS_DIM = QK_HD // 2
ROPE_BASE = 10000.0
SSM_HD = 64
N_GROUPS = 4
D_STATE = 128
CONV_K = 5
CHUNK = 128
RMS_EPS = 1e-6
LAM_INIT = 0.8 - 0.6 * math.exp(-0.3 * 0)

V7X_VMEM_BYTES = 64 * 1024 * 1024
VMEM_LIMIT = V7X_VMEM_BYTES - 8 * 1024 * 1024
LANES = 128
SUBLANES = 8
CONV_HALO = SUBLANES
NORM_ROWS = SUBLANES
BF16_ROWS = 2 * SUBLANES
STAT_UNROLL = 16
APPLY_COLS = 1024
MOD_ROWS = 8


def _pick(n, prefs):
    for p in prefs:
        if n % p == 0:
            return p
    return n


def _params(sem):
    return pltpu.CompilerParams(dimension_semantics=sem, vmem_limit_bytes=VMEM_LIMIT)


def _sigmoid(x):
    return 1.0 / (1.0 + jnp.exp(-x))


def _silu(x):
    return x * _sigmoid(x)


def _softplus(x):
    return jnp.maximum(x, 0.0) + jnp.log1p(jnp.exp(-jnp.abs(x)))


def _rms_rows(x, eps=RMS_EPS):
    return x * lax.rsqrt(jnp.mean(x * x, axis=-1, keepdims=True) + eps)


def _inv_rms(x, eps=RMS_EPS):
    return jnp.broadcast_to(lax.rsqrt(jnp.mean(x * x, axis=-1, keepdims=True) + eps), (x.shape[0], LANES))


def _rows(r, n):
    return pl.ds(pl.multiple_of(r * n, n), n)


def _store_inv_rms(src_ref, stat_ref):
    def body(r, carry):
        rows = _rows(r, NORM_ROWS)
        stat_ref[rows, :] = _inv_rms(src_ref[rows, :])
        return carry

    lax.fori_loop(0, src_ref.shape[0] // NORM_ROWS, body, 0, unroll=STAT_UNROLL)


def _col_chunks(d):
    w = _pick(d, (APPLY_COLS,))
    return [slice(c * w, (c + 1) * w) for c in range(d // w)]


def _ada_kernel(c_ref, w_ref, b_ref, o_ref):
    c = c_ref[...]
    s = _silu(c).astype(BF16)
    o_ref[...] = jnp.dot(s, w_ref[...].astype(BF16), preferred_element_type=F32) + b_ref[...]


def _ada(cvec, w_ada, b_ada):
    rows, d = cvec.shape
    n = w_ada.shape[1]
    tn = _pick(n, (512, 256, 128))
    return pl.pallas_call(
        _ada_kernel,
        grid=(n // tn,),
        in_specs=[pl.BlockSpec((rows, d), lambda j: (0, 0)),
                  pl.BlockSpec((d, tn), lambda j: (0, j)),
                  pl.BlockSpec((1, tn), lambda j: (0, j))],
        out_specs=pl.BlockSpec((rows, tn), lambda j: (0, j)),
        out_shape=jax.ShapeDtypeStruct((rows, n), F32),
        compiler_params=_params(("arbitrary",)),
        name="ada_mod",
    )(cvec, w_ada, b_ada.reshape(1, n))


def _row_tile(m, mod_rows, prefs=(512, 256, 128)):
    return _pick(math.gcd(m, mod_rows[1]), prefs)


def _mod_spec(d, which, mod_rows, tm):
    row0, per = mod_rows
    return pl.BlockSpec((None, None, 1, d), lambda i, j: (row0 + (i * tm) // per, which, 0, 0))


def _in_proj_kernel(x_ref, g_ref, sc_ref, sh_ref, w_ref, wdt_ref, o_ref, dt_ref, h_ref, stat_ref):
    j = pl.program_id(1)

    @pl.when(j == 0)
    def _():
        _store_inv_rms(x_ref, stat_ref)

        def body(r, carry):
            rows = _rows(r, BF16_ROWS)
            inv = stat_ref[rows, 0:1]
            for cols in _col_chunks(x_ref.shape[1]):
                y = (x_ref[rows, cols] * inv) * g_ref[:, cols]
                h_ref[rows, cols] = (y * (1.0 + sc_ref[:, cols]) + sh_ref[:, cols]).astype(BF16)
            return carry

        lax.fori_loop(0, x_ref.shape[0] // BF16_ROWS, body, 0, unroll=2)
        dt_ref[...] = jnp.dot(h_ref[...], wdt_ref[...], preferred_element_type=F32)

    o_ref[...] = jnp.dot(h_ref[...], w_ref[...], preferred_element_type=F32)


def _in_proj(x, mod, g, w_main, w_dt, mod_rows):
    m, d = x.shape
    n = w_main.shape[1]
    tm = _row_tile(m, mod_rows)
    tn = _pick(n, (1024, 512, 256, 128))
    return pl.pallas_call(
        _in_proj_kernel,
        grid=(m // tm, n // tn),
        in_specs=[pl.BlockSpec((tm, d), lambda i, j: (i, 0)),
                  pl.BlockSpec((1, d), lambda i, j: (0, 0)),
                  _mod_spec(d, 1, mod_rows, tm),
                  _mod_spec(d, 0, mod_rows, tm),
                  pl.BlockSpec((d, tn), lambda i, j: (0, j)),
                  pl.BlockSpec((d, LANES), lambda i, j: (0, 0))],
        out_specs=[pl.BlockSpec((tm, tn), lambda i, j: (i, j)),
                   pl.BlockSpec((tm, LANES), lambda i, j: (i, 0))],
        out_shape=[jax.ShapeDtypeStruct((m, n), F32), jax.ShapeDtypeStruct((m, LANES), F32)],
        scratch_shapes=[pltpu.VMEM((tm, d), BF16), pltpu.VMEM((tm, LANES), F32)],
        compiler_params=_params(("arbitrary", "arbitrary")),
        name="in_proj",
    )(x, g.reshape(1, d), mod, mod, w_main, w_dt)


def _rope_tile(x, cos, sin_signed, first_half):
    half = AXIS_DIM // 2
    swapped = jnp.where(first_half, pltpu.roll(x, LANES - half, axis=1), pltpu.roll(x, half, axis=1))
    return x * cos + swapped * sin_signed


def _prep_kernel(*refs, n_ctx_tiles, rope):
    idx = 0
    ctx_ref = None
    if n_ctx_tiles:
        ctx_ref = refs[idx]; idx += 1
    x_ref = refs[idx]; idx += 1
    if rope:
        cos_ref, sin_ref = refs[idx], refs[idx + 1]; idx += 2
    o_ref = refs[idx]
    i = pl.program_id(1)

    def new_rows():
        if not rope:
            o_ref[...] = x_ref[...].astype(BF16)
            return
        cos = cos_ref[...]
        sin = sin_ref[...]
        lane = lax.broadcasted_iota(jnp.int32, cos.shape, 1)
        first_half = (lane % AXIS_DIM) < (AXIS_DIM // 2)
        for c in range(x_ref.shape[1] // LANES):
            cols = slice(c * LANES, (c + 1) * LANES)
            o_ref[:, cols] = _rope_tile(x_ref[:, cols], cos, sin, first_half).astype(BF16)

    if n_ctx_tiles:
        @pl.when(i < n_ctx_tiles)
        def _():
            o_ref[...] = ctx_ref[...].astype(BF16)

        @pl.when(i >= n_ctx_tiles)
        def _():
            new_rows()
    else:
        new_rows()


def _prep(x, w, xcol, ctx, tables):
    b, l, _ = x.shape
    p = 0 if ctx is None else ctx.shape[1]
    tm = _pick(math.gcd(l, p) if p else l, (512, 256, 128))
    nct = p // tm
    in_specs, args = [], []
    if nct:
        in_specs.append(pl.BlockSpec((None, tm, w), lambda bi, i: (bi, jnp.minimum(i, nct - 1), 0)))
        args.append(ctx)
    in_specs.append(pl.BlockSpec((None, tm, w), lambda bi, i: (bi, jnp.maximum(i - nct, 0), xcol)))
    args.append(x)
    if tables is not None:
        for t in tables:
            in_specs.append(pl.BlockSpec((tm, LANES), lambda bi, i: (jnp.maximum(i - nct, 0), 0)))
            args.append(t)
    return pl.pallas_call(
        functools.partial(_prep_kernel, n_ctx_tiles=nct, rope=tables is not None),
        grid=(b, (p + l) // tm),
        in_specs=in_specs,
        out_specs=pl.BlockSpec((None, tm, w), lambda bi, i: (bi, i, 0)),
        out_shape=jax.ShapeDtypeStruct((b, p + l, w), BF16),
        compiler_params=_params(("arbitrary", "arbitrary")),
        name="qkv_prep",
    )(*args)


def _rope_tables(l):
    rows = l // GRID_W
    row = jnp.repeat(jnp.arange(rows, dtype=F32), GRID_W)
    col = jnp.tile(jnp.arange(GRID_W, dtype=F32), rows)
    inv = 1.0 / (ROPE_BASE ** (jnp.arange(0, AXIS_DIM, 2, dtype=F32) / AXIS_DIM))

    def one(pos):
        ang = pos[:, None] * inv[None, :]
        c, s = jnp.cos(ang), jnp.sin(ang)
        return jnp.concatenate([c, c], axis=-1), jnp.concatenate([-s, s], axis=-1)

    cr, sr = one(row)
    cc, sc = one(col)
    return jnp.concatenate([cr, cc], axis=-1), jnp.concatenate([sr, sc], axis=-1)


def _attn_kernel(lq1_ref, lk1_ref, lq2_ref, lk2_ref, g_ref, q_ref, k_ref, v_ref, o_ref, *, heads, ck):
    lam = (jnp.exp(jnp.sum(lq1_ref[...] * lk1_ref[...], axis=-1, keepdims=True))
           - jnp.exp(jnp.sum(lq2_ref[...] * lk2_ref[...], axis=-1, keepdims=True)) + LAM_INIT)
    c2 = (QK_HD ** -0.5) * math.log2(math.e)
    nck = k_ref.shape[0] // ck
    for h in range(heads):
        base = h * V_HD
        outs = []
        for half in range(2):
            cols = slice(base + half * QK_HD, base + (half + 1) * QK_HD)
            q = q_ref[:, cols].astype(BF16)
            m = l = acc = None
            for c in range(nck):
                rows = slice(c * ck, (c + 1) * ck)
                k = k_ref[rows, cols].astype(BF16)
                s = lax.dot_general(q, k, (((1,), (1,)), ((), ())), preferred_element_type=F32)
                mc = jnp.max(s, axis=-1, keepdims=True)
                m_new = mc if c == 0 else jnp.maximum(m, mc)
                e = jnp.exp2((s - m_new) * c2)
                ls = jnp.sum(e, axis=-1, keepdims=True)
                pv = jnp.dot(e.astype(BF16), v_ref[rows, base:base + V_HD].astype(BF16),
                             preferred_element_type=F32)
                if c == 0:
                    l, acc = ls, pv
                else:
                    alpha = jnp.exp2((m - m_new) * c2)
                    l = alpha * l + ls
                    acc = alpha * acc + pv
                m = m_new
            outs.append(acc * (1.0 / l))
        o = outs[0] - lam * outs[1]
        o_ref[:, base:base + V_HD] = ((_rms_rows(o) * g_ref[...]) * (1.0 - LAM_INIT)).astype(BF16)


def _attention(q, k, v, cols, w, lams, g_subln, heads_per_step, tq):
    b, lq, _ = q.shape
    lk = k.shape[1]
    hw = heads_per_step * V_HD
    nh = w // hw
    qc, kc, vc = (c * nh for c in cols)
    ck = _pick(lk, (1536, 1024, 512, 256))
    small = pl.BlockSpec((1, QK_HD), lambda bi, hi, qi: (0, 0))
    return pl.pallas_call(
        functools.partial(_attn_kernel, heads=heads_per_step, ck=ck),
        grid=(b, nh, lq // tq),
        in_specs=[small, small, small, small,
                  pl.BlockSpec((1, V_HD), lambda bi, hi, qi: (0, 0)),
                  pl.BlockSpec((None, tq, hw), lambda bi, hi, qi: (bi, qi, qc + hi)),
                  pl.BlockSpec((None, lk, hw), lambda bi, hi, qi: (bi, 0, kc + hi)),
                  pl.BlockSpec((None, lk, hw), lambda bi, hi, qi: (bi, 0, vc + hi))],
        out_specs=pl.BlockSpec((None, tq, hw), lambda bi, hi, qi: (bi, qi, hi)),
        out_shape=jax.ShapeDtypeStruct((b, lq, w), BF16),
        compiler_params=_params(("arbitrary", "arbitrary", "arbitrary")),
        name="diff_attn",
    )(*lams, g_subln.reshape(1, V_HD), q, k, v)


def _conv_silu(win_ref, prev_ref, main_ref, next_ref, w_ref, b_ref, has_prev, has_next):
    n = main_ref.shape[0]
    win_ref[0:CONV_HALO, :] = jnp.where(has_prev, prev_ref[...], 0.0)
    win_ref[CONV_HALO:CONV_HALO + n, :] = main_ref[...]
    win_ref[CONV_HALO + n:CONV_HALO + n + CONV_HALO, :] = jnp.where(has_next, next_ref[...], 0.0)
    first = CONV_HALO - CONV_K // 2
    out = b_ref[...]
    for j in range(CONV_K):
        out = out + w_ref[j:j + 1, :] * win_ref[first + j:first + j + n, :]
    return _silu(out)


def _ssd_kernel(*refs, nc, heads, has_init):
    (xs_p, xs_m, xs_n, bc_p, bc_m, bc_n, dt_ref, z_ref, cwx_ref, cbx_ref, cwbc_ref, cbbc_ref,
     alog_ref, dtb_ref, dskip_ref, gn_ref) = refs[:16]
    idx = 16
    if has_init:
        initf_ref, initb_ref = refs[idx], refs[idx + 1]
        idx += 2
    y_ref, sf_ref, sb_ref, carf_ref, carb_ref, prev_ref, winx_ref, winbc_ref = refs[idx:idx + 8]

    phase = pl.program_id(1)
    i = pl.program_id(2)
    c = jnp.where(phase == 0, i, nc - 1 - i)
    has_prev = c > 0
    has_next = c < nc - 1
    gw = heads // N_GROUPS * SSM_HD
    hpg = heads // N_GROUPS

    @pl.when((phase == 0) & (i == 0))
    def _():
        if has_init:
            carf_ref[...] = initf_ref[...]
            carb_ref[...] = initb_ref[...]
        else:
            carf_ref[...] = jnp.zeros_like(carf_ref)
            carb_ref[...] = jnp.zeros_like(carb_ref)

    dt = _softplus(dt_ref[...] + dtb_ref[...])
    a = dt * (-jnp.exp(alog_ref[...]))
    row = lax.broadcasted_iota(jnp.int32, (CHUNK, CHUNK), 0)
    colm = lax.broadcasted_iota(jnp.int32, (CHUNK, CHUNK), 1)
    cs = a
    step = 1
    while step < CHUNK:
        cs = cs + jnp.where(row >= step, pltpu.roll(cs, step, axis=0), 0.0)
        step *= 2
    total = cs[CHUNK - 1:CHUNK, :]
    lane = lax.broadcasted_iota(jnp.int32, (CHUNK, LANES), 1)
    fwd_lane = lane < heads
    cum = jnp.where(fwd_lane, cs, total - cs + a)
    tot = total
    cum_t = cum.T
    dt_t = dt.T
    tot_t = jnp.broadcast_to(tot, (CHUNK, LANES)).T
    w_t = jnp.exp(tot_t - cum_t) * dt_t
    e_t = jnp.exp(cum_t)
    dec_t = jnp.exp(tot_t)

    xs = _conv_silu(winx_ref, xs_p, xs_m, xs_n, cwx_ref, cbx_ref, has_prev, has_next)
    bc = _conv_silu(winbc_ref, bc_p, bc_m, bc_n, cwbc_ref, cbbc_ref, has_prev, has_next)
    x_t = xs.T

    def state_update(car_ref, off):
        for g in range(N_GROUPS):
            b_g = bc[:, g * D_STATE:(g + 1) * D_STATE].astype(BF16)
            rows = []
            for hh in range(hpg):
                h = g * hpg + hh
                rows.append(x_t[h * SSM_HD:(h + 1) * SSM_HD, :] * w_t[off + h:off + h + 1, :])
            xw = jnp.concatenate(rows, axis=0).astype(BF16)
            st = jnp.dot(xw, b_g, preferred_element_type=F32)
            for hh in range(hpg):
                h = g * hpg + hh
                r = slice(h * SSM_HD, (h + 1) * SSM_HD)
                car_ref[r, :] = car_ref[r, :] * dec_t[off + h:off + h + 1, :] + st[hh * SSM_HD:(hh + 1) * SSM_HD, :]

    @pl.when(phase == 0)
    def _():
        prev_ref[c] = carf_ref[...].astype(BF16)
        state_update(carf_ref, 0)

        @pl.when(i == nc - 1)
        def _():
            sf_ref[...] = carf_ref[...]

    @pl.when(phase == 1)
    def _():
        causal = colm >= row
        anti = colm <= row
        y_parts = []
        for g in range(N_GROUPS):
            b_g = bc[:, g * D_STATE:(g + 1) * D_STATE].astype(BF16)
            c_g = bc[:, (N_GROUPS + g) * D_STATE:(N_GROUPS + g + 1) * D_STATE].astype(BF16)
            cb_t = lax.dot_general(b_g, c_g, (((1,), (1,)), ((), ())), preferred_element_type=F32)
            rg = slice(g * gw, (g + 1) * gw)
            yo_f = lax.dot_general(prev_ref[c, rg, :], c_g, (((1,), (1,)), ((), ())),
                                   preferred_element_type=F32)
            yo_b = lax.dot_general(carb_ref[rg, :].astype(BF16), c_g, (((1,), (1,)), ((), ())),
                                   preferred_element_type=F32)
            for hh in range(hpg):
                h = g * hpg + hh
                hb = heads + h
                x_h = x_t[h * SSM_HD:(h + 1) * SSM_HD, :]
                seg_f = cum_t[h:h + 1, :] - cum[:, h:h + 1]
                seg_b = cum_t[hb:hb + 1, :] - cum[:, hb:hb + 1]
                w_f = cb_t * jnp.exp(jnp.where(causal, seg_f, -1e4))
                w_b = cb_t * jnp.exp(jnp.where(anti, seg_b, -1e4))
                lhs = jnp.concatenate([x_h * dt_t[h:h + 1, :], x_h * dt_t[hb:hb + 1, :]], axis=1)
                rhs = jnp.concatenate([w_f, w_b], axis=0)
                yd = jnp.dot(lhs.astype(BF16), rhs.astype(BF16), preferred_element_type=F32)
                r = slice(hh * SSM_HD, (hh + 1) * SSM_HD)
                y_parts.append(yd + yo_f[r, :] * e_t[h:h + 1, :] + yo_b[r, :] * e_t[hb:hb + 1, :])
        y = jnp.concatenate(y_parts, axis=0).T
        y = y + dskip_ref[...] * xs
        y = y * _silu(z_ref[...])
        for g in range(N_GROUPS):
            cols = slice(g * gw, (g + 1) * gw)
            y_ref[:, cols] = (_rms_rows(y[:, cols]) * gn_ref[:, cols]).astype(BF16)
        state_update(carb_ref, heads)

        @pl.when(i == nc - 1)
        def _():
            sb_ref[...] = carb_ref[...]


def _ssd(proj, dt, p, init_f, init_b, dx, nbc, cols):
    b, l, _ = proj.shape
    heads = dx // SSM_HD
    nc = l // CHUNK
    zc, xc, bcc = cols
    hpc = CHUNK // CONV_HALO
    nhalo = l // CONV_HALO
    has_init = init_f is not None

    def chunk_of(ph, i):
        return jnp.where(ph == 0, i, nc - 1 - i)

    def main(wd, col):
        return pl.BlockSpec((None, CHUNK, wd), lambda bi, ph, i: (bi, chunk_of(ph, i), col))

    def halo_prev(wd, col):
        return pl.BlockSpec((None, CONV_HALO, wd),
                            lambda bi, ph, i: (bi, jnp.maximum(chunk_of(ph, i) * hpc - 1, 0), col))

    def halo_next(wd, col):
        return pl.BlockSpec((None, CONV_HALO, wd),
                            lambda bi, ph, i: (bi, jnp.minimum((chunk_of(ph, i) + 1) * hpc, nhalo - 1), col))

    def full(shape):
        return pl.BlockSpec(shape, lambda bi, ph, i: (0,) * len(shape))

    state_spec = pl.BlockSpec((None, dx, D_STATE), lambda bi, ph, i: (bi, 0, 0))
    in_specs = [halo_prev(dx, xc), main(dx, xc), halo_next(dx, xc),
                halo_prev(nbc, bcc), main(nbc, bcc), halo_next(nbc, bcc),
                main(LANES, 0), main(dx, zc),
                full((CONV_K, dx)), full((1, dx)), full((CONV_K, nbc)), full((1, nbc)),
                full((1, LANES)), full((1, LANES)), full((1, dx)), full((1, dx))]
    args = [proj, proj, proj, proj, proj, proj, dt, proj,
            p['conv_w_x'], p['conv_b_x'], p['conv_w_bc'], p['conv_b_bc'],
            p['a_log'], p['dt_bias'], p['d_skip'], p['g_ssm_norm']]
    if has_init:
        in_specs += [state_spec, state_spec]
        args += [init_f, init_b]
    y_spec = pl.BlockSpec((None, CHUNK, dx), lambda bi, ph, i: (bi, jnp.where(ph == 0, nc - 1, nc - 1 - i), 0))
    return pl.pallas_call(
        functools.partial(_ssd_kernel, nc=nc, heads=heads, has_init=has_init),
        grid=(b, 2, nc),
        in_specs=in_specs,
        out_specs=[y_spec, state_spec, state_spec],
        out_shape=[jax.ShapeDtypeStruct((b, l, dx), BF16),
                   jax.ShapeDtypeStruct((b, dx, D_STATE), F32),
                   jax.ShapeDtypeStruct((b, dx, D_STATE), F32)],
        scratch_shapes=[pltpu.VMEM((dx, D_STATE), F32), pltpu.VMEM((dx, D_STATE), F32),
                        pltpu.VMEM((nc, dx, D_STATE), BF16),
                        pltpu.VMEM((CHUNK + 2 * CONV_HALO, dx), F32),
                        pltpu.VMEM((CHUNK + 2 * CONV_HALO, nbc), F32)],
        compiler_params=_params(("arbitrary", "arbitrary", "arbitrary")),
        name="conv_ssd",
    )(*args)


def _out_proj_kernel(a_ref, y_ref, w_ref, x_ref, gpost_ref, gate_ref, gpre_ref, sc_ref, sh_ref,
                     x1_ref, h2_ref, stat_ref, stat2_ref, *, nk_a, nk):
    k = pl.program_id(1)

    @pl.when(k == 0)
    def _():
        x1_ref[...] = jnp.dot(a_ref[...], w_ref[...], preferred_element_type=F32)

    @pl.when((k > 0) & (k < nk_a))
    def _():
        x1_ref[...] += jnp.dot(a_ref[...], w_ref[...], preferred_element_type=F32)

    @pl.when(k >= nk_a)
    def _():
        x1_ref[...] += jnp.dot(y_ref[...], w_ref[...], preferred_element_type=F32)

    @pl.when(k == nk - 1)
    def _():
        chunks = _col_chunks(x1_ref.shape[1])
        _store_inv_rms(x1_ref, stat_ref)

        def residual(r, carry):
            rows = _rows(r, BF16_ROWS)
            inv = stat_ref[rows, 0:1]
            for cols in chunks:
                n = (x1_ref[rows, cols] * inv) * gpost_ref[:, cols]
                x1_ref[rows, cols] = x_ref[rows, cols] + gate_ref[:, cols] * n
            return carry

        lax.fori_loop(0, x1_ref.shape[0] // BF16_ROWS, residual, 0, unroll=2)
        _store_inv_rms(x1_ref, stat2_ref)

        def prenorm(r, carry):
            rows = _rows(r, BF16_ROWS)
            inv = stat2_ref[rows, 0:1]
            for cols in chunks:
                y = (x1_ref[rows, cols] * inv) * gpre_ref[:, cols]
                h2_ref[rows, cols] = (y * (1.0 + sc_ref[:, cols]) + sh_ref[:, cols]).astype(BF16)
            return carry

        lax.fori_loop(0, x1_ref.shape[0] // BF16_ROWS, prenorm, 0, unroll=2)


def _out_proj(att, y, w_out, x, mod, g_post, g_pre, mod_rows):
    m, wa = att.shape
    d = x.shape[1]
    tm = _row_tile(m, mod_rows)
    tk = _pick(math.gcd(wa, d - wa), (512, 256, 128))
    nk_a, nk = wa // tk, d // tk
    vec = pl.BlockSpec((1, d), lambda i, k: (0, 0))
    return pl.pallas_call(
        functools.partial(_out_proj_kernel, nk_a=nk_a, nk=nk),
        grid=(m // tm, nk),
        in_specs=[pl.BlockSpec((tm, tk), lambda i, k: (i, jnp.minimum(k, nk_a - 1))),
                  pl.BlockSpec((tm, tk), lambda i, k: (i, jnp.maximum(k - nk_a, 0))),
                  pl.BlockSpec((tk, d), lambda i, k: (k, 0)),
                  pl.BlockSpec((tm, d), lambda i, k: (i, 0), pipeline_mode=pl.Buffered(1)),
                  vec, _mod_spec(d, 2, mod_rows, tm), vec,
                  _mod_spec(d, 4, mod_rows, tm), _mod_spec(d, 3, mod_rows, tm)],
        out_specs=[pl.BlockSpec((tm, d), lambda i, k: (i, 0)),
                   pl.BlockSpec((tm, d), lambda i, k: (i, 0))],
        out_shape=[jax.ShapeDtypeStruct((m, d), F32), jax.ShapeDtypeStruct((m, d), BF16)],
        scratch_shapes=[pltpu.VMEM((tm, LANES), F32), pltpu.VMEM((tm, LANES), F32)],
        compiler_params=_params(("arbitrary", "arbitrary")),
        name="out_proj",
    )(att, y, w_out, x, g_post.reshape(1, d), mod, g_pre.reshape(1, d), mod, mod)


def _up_kernel(h_ref, w_ref, o_ref):
    u = jnp.maximum(jnp.dot(h_ref[...], w_ref[...], preferred_element_type=F32), 0.0)
    o_ref[...] = (u * u).astype(BF16)


def _up(h2, w_up):
    m, d = h2.shape
    f = w_up.shape[1]
    tm = _pick(m, (1024, 512, 256, 128))
    tn = _pick(f, (1024, 512, 256, 128))
    return pl.pallas_call(
        _up_kernel,
        grid=(m // tm, f // tn),
        in_specs=[pl.BlockSpec((tm, d), lambda i, j: (i, 0)),
                  pl.BlockSpec((d, tn), lambda i, j: (0, j))],
        out_specs=pl.BlockSpec((tm, tn), lambda i, j: (i, j)),
        out_shape=jax.ShapeDtypeStruct((m, f), BF16),
        compiler_params=_params(("arbitrary", "arbitrary")),
        name="mlp_up",
    )(h2, w_up)


def _down_kernel(u_ref, w_ref, x_ref, gpost_ref, gate_ref, o_ref, stat_ref, *, nk):
    k = pl.program_id(1)

    @pl.when(k == 0)
    def _():
        o_ref[...] = jnp.dot(u_ref[...], w_ref[...], preferred_element_type=F32)

    @pl.when(k > 0)
    def _():
        o_ref[...] += jnp.dot(u_ref[...], w_ref[...], preferred_element_type=F32)

    @pl.when(k == nk - 1)
    def _():
        _store_inv_rms(o_ref, stat_ref)

        def body(r, carry):
            rows = _rows(r, BF16_ROWS)
            inv = stat_ref[rows, 0:1]
            for cols in _col_chunks(o_ref.shape[1]):
                n = (o_ref[rows, cols] * inv) * gpost_ref[:, cols]
                o_ref[rows, cols] = x_ref[rows, cols] + gate_ref[:, cols] * n
            return carry

        lax.fori_loop(0, o_ref.shape[0] // BF16_ROWS, body, 0, unroll=2)


def _down(u, w_down, x1, mod, g_post, mod_rows):
    m, f = u.shape
    d = x1.shape[1]
    tm = _row_tile(m, mod_rows)
    tk = _pick(f, (1024, 512, 256, 128))
    nk = f // tk
    return pl.pallas_call(
        functools.partial(_down_kernel, nk=nk),
        grid=(m // tm, nk),
        in_specs=[pl.BlockSpec((tm, tk), lambda i, k: (i, k)),
                  pl.BlockSpec((tk, d), lambda i, k: (k, 0)),
                  pl.BlockSpec((tm, d), lambda i, k: (i, 0), pipeline_mode=pl.Buffered(1)),
                  pl.BlockSpec((1, d), lambda i, k: (0, 0)),
                  _mod_spec(d, 5, mod_rows, tm)],
        out_specs=pl.BlockSpec((tm, d), lambda i, k: (i, 0)),
        out_shape=jax.ShapeDtypeStruct((m, d), F32),
        scratch_shapes=[pltpu.VMEM((tm, LANES), F32)],
        compiler_params=_params(("arbitrary", "arbitrary")),
        name="mlp_down",
    )(u, w_down, x1, g_post.reshape(1, d), mod)


def _trunk(x, mod, mod_row0, rows_per_mod, w, ctx_k, ctx_v, init_f, init_b):
    b, l, d = x.shape
    m = b * l
    att_w = d // 2
    d_inner = d - att_w
    n_bc = 2 * N_GROUPS * D_STATE
    latent = ctx_k is not None

    mod_rows = (mod_row0, rows_per_mod)
    x2d = x.reshape(m, d)
    assert att_w == d_inner and (3 * att_w + 2 * d_inner) % n_bc == 0
    proj, dt = _in_proj(x2d, mod, w['g_mix_pre'], w['w_in_main'], w['w_in_dt'], mod_rows)
    proj3 = proj.reshape(b, l, -1)
    lams = (w['lq1'], w['lk1'], w['lq2'], w['lk2'])
    if latent:
        tables = _rope_tables(l)
        qa = _prep(proj3, att_w, 0, None, tables)
        ka = _prep(proj3, att_w, 1, ctx_k, tables)
        va = _prep(proj3, att_w, 2, ctx_v, None)
        att = _attention(qa, ka, va, (0, 0, 0), att_w, lams, w['g_subln'], 1, _pick(l, (256, 128)))
    else:
        att = _attention(proj3, proj3, proj3, (0, 1, 2), att_w, lams, w['g_subln'], att_w // V_HD, l)
    y, fin_f, fin_b = _ssd(proj3, dt.reshape(b, l, LANES), w, init_f, init_b, d_inner, n_bc,
                           (3, 4, (3 * att_w + 2 * d_inner) // n_bc))
    x1, h2 = _out_proj(att.reshape(m, att_w), y.reshape(m, d_inner), w['w_out'], x2d, mod,
                       w['g_mix_post'], w['g_mlp_pre'], mod_rows)
    u = _up(h2, w['w_up'])
    x2 = _down(u, w['w_down'], x1, mod, w['g_mlp_post'], mod_rows)
    return x2.reshape(b, l, d), proj[:, att_w:2 * att_w], proj[:, 2 * att_w:3 * att_w], fin_f, fin_b


def kernel(x_prompt, x_sample, c, cache_k, cache_v, state_ssm_fwd, state_ssm_bwd, c_ctx, w_ada, b_ada,
           g_mix_pre, g_mix_post, g_mlp_pre, g_mlp_post, w_in, lambda_q1, lambda_k1, lambda_q2, lambda_k2,
           g_subln, conv_w, conv_b, a_log, dt_bias, d_skip, g_ssm_norm, w_out, w_up, w_down):
    bp, lp, d = x_prompt.shape
    bd, ld, _ = x_sample.shape
    depth = w_in.shape[0]
    assert depth == 1 and bd + 1 <= MOD_ROWS
    att_w = d // 2
    d_inner = d - att_w
    heads = d_inner // SSM_HD
    n_att = att_w // V_HD
    n_main = 3 * att_w + 2 * d_inner + 2 * N_GROUPS * D_STATE
    assert 2 * heads <= LANES and w_in.shape[2] == n_main + 2 * heads

    lyr = 0
    pad_l = LANES - 2 * heads
    w = dict(
        g_mix_pre=g_mix_pre[lyr], g_mix_post=g_mix_post[lyr], g_mlp_pre=g_mlp_pre[lyr],
        g_mlp_post=g_mlp_post[lyr],
        w_in_main=w_in[lyr, :, :n_main].astype(BF16),
        w_in_dt=jnp.pad(w_in[lyr, :, n_main:], ((0, 0), (0, pad_l))).astype(BF16),
        lq1=lambda_q1[lyr].reshape(1, QK_HD), lk1=lambda_k1[lyr].reshape(1, QK_HD),
        lq2=lambda_q2[lyr].reshape(1, QK_HD), lk2=lambda_k2[lyr].reshape(1, QK_HD),
        g_subln=g_subln[lyr],
        conv_w_x=conv_w[lyr, :, :d_inner], conv_w_bc=conv_w[lyr, :, d_inner:],
        conv_b_x=conv_b[lyr, :d_inner].reshape(1, -1), conv_b_bc=conv_b[lyr, d_inner:].reshape(1, -1),
        a_log=jnp.pad(a_log[lyr].reshape(1, -1), ((0, 0), (0, pad_l))),
        dt_bias=jnp.pad(dt_bias[lyr].reshape(1, -1), ((0, 0), (0, pad_l))),
        d_skip=jnp.repeat(d_skip[lyr], SSM_HD).reshape(1, -1),
        g_ssm_norm=g_ssm_norm[lyr].reshape(1, -1),
        w_out=w_out[lyr].astype(BF16), w_up=w_up[lyr].astype(BF16), w_down=w_down[lyr].astype(BF16),
    )

    cvec = jnp.concatenate([c_ctx[None, :], c, jnp.zeros((MOD_ROWS - 1 - bd, d), F32)], axis=0)
    mod = _ada(cvec, w_ada[lyr], b_ada[lyr]).reshape(MOD_ROWS, 6, 1, d)

    yp, k_c, v_c, sf, sb = _trunk(x_prompt, mod, 0, bp * lp, w, None, None, None, None)
    ys, _, _, _, _ = _trunk(
        x_sample, mod, 1, ld, w,
        cache_k[:, lyr].reshape(bd, -1, att_w), cache_v[:, lyr].reshape(bd, -1, att_w),
        state_ssm_fwd[:, lyr].reshape(bd, d_inner, D_STATE), state_ssm_bwd[:, lyr].reshape(bd, d_inner, D_STATE))

    return (yp, ys,
            k_c.reshape(bp, 1, lp, n_att, 2, QK_HD), v_c.reshape(bp, 1, lp, n_att, V_HD),
            sf.reshape(bp, 1, heads, SSM_HD, D_STATE), sb.reshape(bp, 1, heads, SSM_HD, D_STATE))
```

```python
import functools
import math

import jax
import jax.numpy as jnp
from jax import lax
from jax.experimental import pallas as pl
from jax.experimental.pallas import tpu as pltpu

F32 = jnp.float32
BF16 = jnp.bfloat16

GRID_W = 64
V_HD = 256
QK_HD = V_HD // 2
AXIS_DIM = QK_HD // 2
ROPE_BASE = 10000.0
SSM_HD = 64
N_GROUPS = 4
D_STATE = 128
CONV_K = 5
CHUNK = 128
RMS_EPS = 1e-6
LAM_INIT = 0.8 - 0.6 * math.exp(-0.3 * 0)

V7X_VMEM_BYTES = 64 * 1024 * 1024
VMEM_LIMIT = V7X_VMEM_BYTES - 8 * 1024 * 1024
LANES = 128
SUBLANES = 8
CONV_HALO = SUBLANES
NORM_ROWS = SUBLANES
BF16_ROWS = 2 * SUBLANES
STAT_UNROLL = 16
APPLY_COLS = 1024
ROPE_ROWS = 512
MOD_ROWS = 8


def _pick(n, prefs):
    for p in prefs:
        if n % p == 0:
            return p
    return n


def _params(sem):
    return pltpu.CompilerParams(dimension_semantics=sem, vmem_limit_bytes=VMEM_LIMIT)


def _sigmoid(x):
    return 1.0 / (1.0 + jnp.exp(-x))


def _silu(x):
    return x * _sigmoid(x)


def _softplus(x):
    return jnp.maximum(x, 0.0) + jnp.log1p(jnp.exp(-jnp.abs(x)))


def _rms_rows(x, eps=RMS_EPS):
    return x * lax.rsqrt(jnp.mean(x * x, axis=-1, keepdims=True) + eps)


def _inv_rms(x, eps=RMS_EPS):
    return jnp.broadcast_to(lax.rsqrt(jnp.mean(x * x, axis=-1, keepdims=True) + eps), (x.shape[0], LANES))


def _rows(r, n):
    return pl.ds(pl.multiple_of(r * n, n), n)


def _store_inv_rms(src_ref, stat_ref):
    def body(r, carry):
        rows = _rows(r, NORM_ROWS)
        stat_ref[rows, :] = _inv_rms(src_ref[rows, :])
        return carry

    lax.fori_loop(0, src_ref.shape[0] // NORM_ROWS, body, 0, unroll=STAT_UNROLL)


def _col_chunks(d):
    w = _pick(d, (APPLY_COLS,))
    return [slice(c * w, (c + 1) * w) for c in range(d // w)]


def _ada_kernel(c_ref, w_ref, b_ref, o_ref):
    c = c_ref[...]
    s = _silu(c).astype(BF16)
    o_ref[...] = jnp.dot(s, w_ref[...].astype(BF16), preferred_element_type=F32) + b_ref[...]


def _ada(cvec, w_ada, b_ada):
    rows, d = cvec.shape
    n = w_ada.shape[1]
    tn = _pick(n, (512, 256, 128))
    return pl.pallas_call(
        _ada_kernel,
        grid=(n // tn,),
        in_specs=[pl.BlockSpec((rows, d), lambda j: (0, 0)),
                  pl.BlockSpec((d, tn), lambda j: (0, j)),
                  pl.BlockSpec((1, tn), lambda j: (0, j))],
        out_specs=pl.BlockSpec((rows, tn), lambda j: (0, j)),
        out_shape=jax.ShapeDtypeStruct((rows, n), F32),
        compiler_params=_params(("arbitrary",)),
        name="ada_mod",
    )(cvec, w_ada, b_ada.reshape(1, n))


def _row_tile(m, mod_rows, prefs=(512, 256, 128)):
    return _pick(math.gcd(m, mod_rows[1]), prefs)


def _mod_spec(d, which, mod_rows, tm):
    row0, per = mod_rows
    return pl.BlockSpec((None, None, 1, d), lambda i, j: (row0 + (i * tm) // per, which, 0, 0))


def _prenorm_kernel(x_ref, g_ref, sc_ref, sh_ref, h_ref, stat_ref):
    _store_inv_rms(x_ref, stat_ref)

    def body(r, carry):
        rows = _rows(r, BF16_ROWS)
        inv = stat_ref[rows, 0:1]
        for cols in _col_chunks(x_ref.shape[1]):
            y = (x_ref[rows, cols] * inv) * g_ref[:, cols]
            h_ref[rows, cols] = (y * (1.0 + sc_ref[:, cols]) + sh_ref[:, cols]).astype(BF16)
        return carry

    lax.fori_loop(0, x_ref.shape[0] // BF16_ROWS, body, 0, unroll=2)


def _prenorm(x, mod, g, mod_rows):
    m, d = x.shape
    tm = _row_tile(m, mod_rows, (256, 128))
    return pl.pallas_call(
        _prenorm_kernel,
        grid=(m // tm, 1),
        in_specs=[pl.BlockSpec((tm, d), lambda i, j: (i, 0)),
                  pl.BlockSpec((1, d), lambda i, j: (0, 0)),
                  _mod_spec(d, 1, mod_rows, tm),
                  _mod_spec(d, 0, mod_rows, tm)],
        out_specs=pl.BlockSpec((tm, d), lambda i, j: (i, 0)),
        out_shape=jax.ShapeDtypeStruct((m, d), BF16),
        scratch_shapes=[pltpu.VMEM((tm, LANES), F32)],
        compiler_params=_params(("arbitrary", "arbitrary")),
        name="mix_prenorm",
    )(x, g.reshape(1, d), mod, mod)


def _in_proj_kernel(h_ref, w_ref, wdt_ref, o_ref, dt_ref):
    @pl.when(pl.program_id(1) == 0)
    def _():
        dt_ref[...] = jnp.dot(h_ref[...], wdt_ref[...], preferred_element_type=F32)

    o_ref[...] = jnp.dot(h_ref[...], w_ref[...], preferred_element_type=F32)


def _in_proj(h, w_main, w_dt):
    m, d = h.shape
    n = w_main.shape[1]
    tm = _pick(m, (1024, 512, 256, 128))
    tn = _pick(n, (1024, 512, 256, 128))
    return pl.pallas_call(
        _in_proj_kernel,
        grid=(m // tm, n // tn),
        in_specs=[pl.BlockSpec((tm, d), lambda i, j: (i, 0)),
                  pl.BlockSpec((d, tn), lambda i, j: (0, j)),
                  pl.BlockSpec((d, LANES), lambda i, j: (0, 0))],
        out_specs=[pl.BlockSpec((tm, tn), lambda i, j: (i, j)),
                   pl.BlockSpec((tm, LANES), lambda i, j: (i, 0))],
        out_shape=[jax.ShapeDtypeStruct((m, n), F32), jax.ShapeDtypeStruct((m, LANES), F32)],
        compiler_params=_params(("arbitrary", "arbitrary")),
        name="in_proj",
    )(h, w_main, w_dt)


def _rope_tile(x, cos, sin_signed, first_half):
    half = AXIS_DIM // 2
    swapped = jnp.where(first_half, pltpu.roll(x, LANES - half, axis=1), pltpu.roll(x, half, axis=1))
    return x * cos + swapped * sin_signed


def _rope_tables(l):
    rows = l // GRID_W
    row = jnp.repeat(jnp.arange(rows, dtype=F32), GRID_W)
    col = jnp.tile(jnp.arange(GRID_W, dtype=F32), rows)
    inv = 1.0 / (ROPE_BASE ** (jnp.arange(0, AXIS_DIM, 2, dtype=F32) / AXIS_DIM))

    def one(pos):
        ang = pos[:, None] * inv[None, :]
        c, s = jnp.cos(ang), jnp.sin(ang)
        return jnp.concatenate([c, c], axis=-1), jnp.concatenate([-s, s], axis=-1)

    cr, sr = one(row)
    cc, sc = one(col)
    return jnp.concatenate([cr, cc], axis=-1), jnp.concatenate([sr, sc], axis=-1)


def _attn_kernel(*refs, heads, ck, n_ctx):
    lq1_ref, lk1_ref, lq2_ref, lk2_ref, g_ref, q_ref, k_ref, v_ref = refs[:8]
    if n_ctx:
        ck_ref, cv_ref, cosq_ref, sinq_ref, cosk_ref, sink_ref, o_ref, kb_ref, vb_ref = refs[8:]
        lane = lax.broadcasted_iota(jnp.int32, (1, LANES), 1)
        first_half = (lane % AXIS_DIM) < (AXIS_DIM // 2)

        @pl.when(pl.program_id(2) == 0)
        def _():
            n_new = k_ref.shape[0]
            kb_ref[0:n_ctx, :] = ck_ref[...].astype(BF16)
            vb_ref[0:n_ctx, :] = cv_ref[...].astype(BF16)
            vb_ref[n_ctx:n_ctx + n_new, :] = v_ref[...].astype(BF16)
            step = _pick(n_new, (ROPE_ROWS,))
            for r0 in range(0, n_new, step):
                rows = slice(r0, r0 + step)
                for c0 in range(0, heads * V_HD, LANES):
                    cols = slice(c0, c0 + LANES)
                    roped = _rope_tile(k_ref[rows, cols], cosk_ref[rows, :], sink_ref[rows, :], first_half)
                    kb_ref[n_ctx + r0:n_ctx + r0 + step, cols] = roped.astype(BF16)

        k_src, v_src = kb_ref, vb_ref
    else:
        o_ref = refs[8]
        k_src, v_src = k_ref, v_ref

    lam = (jnp.exp(jnp.sum(lq1_ref[...] * lk1_ref[...], axis=-1, keepdims=True))
           - jnp.exp(jnp.sum(lq2_ref[...] * lk2_ref[...], axis=-1, keepdims=True)) + LAM_INIT)
    c2 = (QK_HD ** -0.5) * math.log2(math.e)
    nck = k_src.shape[0] // ck
    for h in range(heads):
        base = h * V_HD
        outs = []
        for half in range(2):
            cols = slice(base + half * QK_HD, base + (half + 1) * QK_HD)
            q = q_ref[:, cols]
            if n_ctx:
                q = _rope_tile(q, cosq_ref[...], sinq_ref[...], first_half)
            q = q.astype(BF16)
            m = l = acc = None
            for c in range(nck):
                rows = slice(c * ck, (c + 1) * ck)
                k = k_src[rows, cols].astype(BF16)
                s = lax.dot_general(q, k, (((1,), (1,)), ((), ())), preferred_element_type=F32)
                mc = jnp.max(s, axis=-1, keepdims=True)
                m_new = mc if c == 0 else jnp.maximum(m, mc)
                e = jnp.exp2((s - m_new) * c2)
                ls = jnp.sum(e, axis=-1, keepdims=True)
                pv = jnp.dot(e.astype(BF16), v_src[rows, base:base + V_HD].astype(BF16),
                             preferred_element_type=F32)
                if c == 0:
                    l, acc = ls, pv
                else:
                    alpha = jnp.exp2((m - m_new) * c2)
                    l = alpha * l + ls
                    acc = alpha * acc + pv
                m = m_new
            outs.append(acc * (1.0 / l))
        o = outs[0] - lam * outs[1]
        o_ref[:, base:base + V_HD] = ((_rms_rows(o) * g_ref[...]) * (1.0 - LAM_INIT)).astype(BF16)


def _attention(proj, w, lams, g_subln, heads_per_step, tq, ctx_k=None, ctx_v=None, tables=None):
    b, l, _ = proj.shape
    hw = heads_per_step * V_HD
    nh = w // hw
    n_ctx = 0 if ctx_k is None else ctx_k.shape[1]
    ck = _pick(n_ctx + l, (1536, 1024, 512, 256))
    small = pl.BlockSpec((1, QK_HD), lambda bi, hi, qi: (0, 0))
    in_specs = [small, small, small, small,
                pl.BlockSpec((1, V_HD), lambda bi, hi, qi: (0, 0)),
                pl.BlockSpec((None, tq, hw), lambda bi, hi, qi: (bi, qi, hi)),
                pl.BlockSpec((None, l, hw), lambda bi, hi, qi: (bi, 0, nh + hi)),
                pl.BlockSpec((None, l, hw), lambda bi, hi, qi: (bi, 0, 2 * nh + hi))]
    args = [*lams, g_subln.reshape(1, V_HD), proj, proj, proj]
    scratch = []
    if n_ctx:
        ctx_spec = pl.BlockSpec((None, n_ctx, hw), lambda bi, hi, qi: (bi, 0, hi))
        q_tab = pl.BlockSpec((tq, LANES), lambda bi, hi, qi: (qi, 0))
        k_tab = pl.BlockSpec((l, LANES), lambda bi, hi, qi: (0, 0))
        in_specs += [ctx_spec, ctx_spec, q_tab, q_tab, k_tab, k_tab]
        args += [ctx_k, ctx_v, tables[0], tables[1], tables[0], tables[1]]
        scratch = [pltpu.VMEM((n_ctx + l, hw), BF16), pltpu.VMEM((n_ctx + l, hw), BF16)]
    return pl.pallas_call(
        functools.partial(_attn_kernel, heads=heads_per_step, ck=ck, n_ctx=n_ctx),
        grid=(b, nh, l // tq),
        in_specs=in_specs,
        out_specs=pl.BlockSpec((None, tq, hw), lambda bi, hi, qi: (bi, qi, hi)),
        out_shape=jax.ShapeDtypeStruct((b, l, w), BF16),
        scratch_shapes=scratch,
        compiler_params=_params(("arbitrary", "arbitrary", "arbitrary")),
        name="diff_attn",
    )(*args)


def _conv_silu(win_ref, prev_ref, main_ref, next_ref, w_ref, b_ref, has_prev, has_next):
    n = main_ref.shape[0]
    win_ref[0:CONV_HALO, :] = jnp.where(has_prev, prev_ref[...], 0.0)
    win_ref[CONV_HALO:CONV_HALO + n, :] = main_ref[...]
    win_ref[CONV_HALO + n:CONV_HALO + n + CONV_HALO, :] = jnp.where(has_next, next_ref[...], 0.0)
    first = CONV_HALO - CONV_K // 2
    out = b_ref[...]
    for j in range(CONV_K):
        out = out + w_ref[j:j + 1, :] * win_ref[first + j:first + j + n, :]
    return _silu(out)


def _conv_kernel(xs_p, xs_m, xs_n, bc_p, bc_m, bc_n, cwx_ref, cbx_ref, cwbc_ref, cbbc_ref,
                 xo_ref, bco_ref, winx_ref, winbc_ref, *, nt):
    i = pl.program_id(1)
    has_prev = i > 0
    has_next = i < nt - 1
    xo_ref[...] = _conv_silu(winx_ref, xs_p, xs_m, xs_n, cwx_ref, cbx_ref, has_prev, has_next)
    bco_ref[...] = _conv_silu(winbc_ref, bc_p, bc_m, bc_n, cwbc_ref, cbbc_ref, has_prev, has_next).astype(BF16)


def _conv(proj, p, dx, nbc, xc, bcc):
    b, l, _ = proj.shape
    tr = _pick(l, (256, 128))
    nt = l // tr
    hpt = tr // CONV_HALO
    nhalo = l // CONV_HALO

    def main(wd, col):
        return pl.BlockSpec((None, tr, wd), lambda bi, i: (bi, i, col))

    def halo_prev(wd, col):
        return pl.BlockSpec((None, CONV_HALO, wd), lambda bi, i: (bi, jnp.maximum(i * hpt - 1, 0), col))

    def halo_next(wd, col):
        return pl.BlockSpec((None, CONV_HALO, wd),
                            lambda bi, i: (bi, jnp.minimum((i + 1) * hpt, nhalo - 1), col))

    def full(shape):
        return pl.BlockSpec(shape, lambda bi, i: (0,) * len(shape))

    return pl.pallas_call(
        functools.partial(_conv_kernel, nt=nt),
        grid=(b, nt),
        in_specs=[halo_prev(dx, xc), main(dx, xc), halo_next(dx, xc),
                  halo_prev(nbc, bcc), main(nbc, bcc), halo_next(nbc, bcc),
                  full((CONV_K, dx)), full((1, dx)), full((CONV_K, nbc)), full((1, nbc))],
        out_specs=[pl.BlockSpec((None, tr, dx), lambda bi, i: (bi, i, 0)),
                   pl.BlockSpec((None, tr, nbc), lambda bi, i: (bi, i, 0))],
        out_shape=[jax.ShapeDtypeStruct((b, l, dx), F32), jax.ShapeDtypeStruct((b, l, nbc), BF16)],
        scratch_shapes=[pltpu.VMEM((tr + 2 * CONV_HALO, dx), F32),
                        pltpu.VMEM((tr + 2 * CONV_HALO, nbc), F32)],
        compiler_params=_params(("arbitrary", "arbitrary")),
        name="conv_silu",
    )(proj, proj, proj, proj, proj, proj, p['conv_w_x'], p['conv_b_x'], p['conv_w_bc'], p['conv_b_bc'])


def _ssd_kernel(*refs, nc, heads, has_init):
    xs_ref, bc_ref, dt_ref, z_ref, alog_ref, dtb_ref, dskip_ref, gn_ref = refs[:8]
    idx = 8
    if has_init:
        initf_ref, initb_ref = refs[idx], refs[idx + 1]
        idx += 2
    y_ref, sf_ref, sb_ref, carf_ref, carb_ref, prev_ref = refs[idx:idx + 6]

    phase = pl.program_id(1)
    i = pl.program_id(2)
    c = jnp.where(phase == 0, i, nc - 1 - i)
    gw = heads // N_GROUPS * SSM_HD
    hpg = heads // N_GROUPS

    @pl.when((phase == 0) & (i == 0))
    def _():
        if has_init:
            carf_ref[...] = initf_ref[...]
            carb_ref[...] = initb_ref[...]
        else:
            carf_ref[...] = jnp.zeros_like(carf_ref)
            carb_ref[...] = jnp.zeros_like(carb_ref)

    dt = _softplus(dt_ref[...] + dtb_ref[...])
    a = dt * (-jnp.exp(alog_ref[...]))
    row = lax.broadcasted_iota(jnp.int32, (CHUNK, CHUNK), 0)
    colm = lax.broadcasted_iota(jnp.int32, (CHUNK, CHUNK), 1)
    cs = a
    step = 1
    while step < CHUNK:
        cs = cs + jnp.where(row >= step, pltpu.roll(cs, step, axis=0), 0.0)
        step *= 2
    total = cs[CHUNK - 1:CHUNK, :]
    lane = lax.broadcasted_iota(jnp.int32, (CHUNK, LANES), 1)
    fwd_lane = lane < heads
    cum = jnp.where(fwd_lane, cs, total - cs + a)
    tot = total
    cum_t = cum.T
    dt_t = dt.T
    tot_t = jnp.broadcast_to(tot, (CHUNK, LANES)).T
    w_t = jnp.exp(tot_t - cum_t) * dt_t
    e_t = jnp.exp(cum_t)
    dec_t = jnp.exp(tot_t)

    xs = xs_ref[...]
    bc = bc_ref[...]
    x_t = xs.T

    def state_update(car_ref, off):
        for g in range(N_GROUPS):
            b_g = bc[:, g * D_STATE:(g + 1) * D_STATE].astype(BF16)
            rows = []
            for hh in range(hpg):
                h = g * hpg + hh
                rows.append(x_t[h * SSM_HD:(h + 1) * SSM_HD, :] * w_t[off + h:off + h + 1, :])
            xw = jnp.concatenate(rows, axis=0).astype(BF16)
            st = jnp.dot(xw, b_g, preferred_element_type=F32)
            for hh in range(hpg):
                h = g * hpg + hh
                r = slice(h * SSM_HD, (h + 1) * SSM_HD)
                car_ref[r, :] = car_ref[r, :] * dec_t[off + h:off + h + 1, :] + st[hh * SSM_HD:(hh + 1) * SSM_HD, :]

    @pl.when(phase == 0)
    def _():
        prev_ref[c] = carf_ref[...].astype(BF16)
        state_update(carf_ref, 0)

        @pl.when(i == nc - 1)
        def _():
            sf_ref[...] = carf_ref[...]

    @pl.when(phase == 1)
    def _():
        causal = colm >= row
        anti = colm <= row
        y_parts = []
        for g in range(N_GROUPS):
            b_g = bc[:, g * D_STATE:(g + 1) * D_STATE].astype(BF16)
            c_g = bc[:, (N_GROUPS + g) * D_STATE:(N_GROUPS + g + 1) * D_STATE].astype(BF16)
            cb_t = lax.dot_general(b_g, c_g, (((1,), (1,)), ((), ())), preferred_element_type=F32)
            rg = slice(g * gw, (g + 1) * gw)
            yo_f = lax.dot_general(prev_ref[c, rg, :], c_g, (((1,), (1,)), ((), ())),
                                   preferred_element_type=F32)
            yo_b = lax.dot_general(carb_ref[rg, :].astype(BF16), c_g, (((1,), (1,)), ((), ())),
                                   preferred_element_type=F32)
            for hh in range(hpg):
                h = g * hpg + hh
                hb = heads + h
                x_h = x_t[h * SSM_HD:(h + 1) * SSM_HD, :]
                seg_f = cum_t[h:h + 1, :] - cum[:, h:h + 1]
                seg_b = cum_t[hb:hb + 1, :] - cum[:, hb:hb + 1]
                w_f = cb_t * jnp.exp(jnp.where(causal, seg_f, -1e4))
                w_b = cb_t * jnp.exp(jnp.where(anti, seg_b, -1e4))
                lhs = jnp.concatenate([x_h * dt_t[h:h + 1, :], x_h * dt_t[hb:hb + 1, :]], axis=1)
                rhs = jnp.concatenate([w_f, w_b], axis=0)
                yd = jnp.dot(lhs.astype(BF16), rhs.astype(BF16), preferred_element_type=F32)
                r = slice(hh * SSM_HD, (hh + 1) * SSM_HD)
                y_parts.append(yd + yo_f[r, :] * e_t[h:h + 1, :] + yo_b[r, :] * e_t[hb:hb + 1, :])
        y = jnp.concatenate(y_parts, axis=0).T
        y = y + dskip_ref[...] * xs
        y = y * _silu(z_ref[...])
        for g in range(N_GROUPS):
            cols = slice(g * gw, (g + 1) * gw)
            y_ref[:, cols] = (_rms_rows(y[:, cols]) * gn_ref[:, cols]).astype(BF16)
        state_update(carb_ref, heads)

        @pl.when(i == nc - 1)
        def _():
            sb_ref[...] = carb_ref[...]


def _ssd(xs, bc, dt, proj, zc, p, init_f, init_b):
    b, l, dx = xs.shape
    nbc = bc.shape[2]
    heads = dx // SSM_HD
    nc = l // CHUNK
    has_init = init_f is not None

    def chunk_of(ph, i):
        return jnp.where(ph == 0, i, nc - 1 - i)

    def main(wd, col):
        return pl.BlockSpec((None, CHUNK, wd), lambda bi, ph, i: (bi, chunk_of(ph, i), col))

    def full(shape):
        return pl.BlockSpec(shape, lambda bi, ph, i: (0,) * len(shape))

    state_spec = pl.BlockSpec((None, dx, D_STATE), lambda bi, ph, i: (bi, 0, 0))
    in_specs = [main(dx, 0), main(nbc, 0), main(LANES, 0), main(dx, zc),
                full((1, LANES)), full((1, LANES)), full((1, dx)), full((1, dx))]
    args = [xs, bc, dt, proj, p['a_log'], p['dt_bias'], p['d_skip'], p['g_ssm_norm']]
    if has_init:
        in_specs += [state_spec, state_spec]
        args += [init_f, init_b]
    y_spec = pl.BlockSpec((None, CHUNK, dx), lambda bi, ph, i: (bi, jnp.where(ph == 0, nc - 1, nc - 1 - i), 0))
    return pl.pallas_call(
        functools.partial(_ssd_kernel, nc=nc, heads=heads, has_init=has_init),
        grid=(b, 2, nc),
        in_specs=in_specs,
        out_specs=[y_spec, state_spec, state_spec],
        out_shape=[jax.ShapeDtypeStruct((b, l, dx), BF16),
                   jax.ShapeDtypeStruct((b, dx, D_STATE), F32),
                   jax.ShapeDtypeStruct((b, dx, D_STATE), F32)],
        scratch_shapes=[pltpu.VMEM((dx, D_STATE), F32), pltpu.VMEM((dx, D_STATE), F32),
                        pltpu.VMEM((nc, dx, D_STATE), BF16)],
        compiler_params=_params(("arbitrary", "arbitrary", "arbitrary")),
        name="ssd_scan",
    )(*args)


def _out_proj_kernel(a_ref, y_ref, w_ref, x_ref, gpost_ref, gate_ref, gpre_ref, sc_ref, sh_ref,
                     x1_ref, h2_ref, stat_ref, stat2_ref, *, nk_a, nk):
    k = pl.program_id(1)

    @pl.when(k == 0)
    def _():
        x1_ref[...] = jnp.dot(a_ref[...], w_ref[...], preferred_element_type=F32)

    @pl.when((k > 0) & (k < nk_a))
    def _():
        x1_ref[...] += jnp.dot(a_ref[...], w_ref[...], preferred_element_type=F32)

    @pl.when(k >= nk_a)
    def _():
        x1_ref[...] += jnp.dot(y_ref[...], w_ref[...], preferred_element_type=F32)

    @pl.when(k == nk - 1)
    def _():
        chunks = _col_chunks(x1_ref.shape[1])
        _store_inv_rms(x1_ref, stat_ref)

        def residual(r, carry):
            rows = _rows(r, BF16_ROWS)
            inv = stat_ref[rows, 0:1]
            for cols in chunks:
                n = (x1_ref[rows, cols] * inv) * gpost_ref[:, cols]
                x1_ref[rows, cols] = x_ref[rows, cols] + gate_ref[:, cols] * n
            return carry

        lax.fori_loop(0, x1_ref.shape[0] // BF16_ROWS, residual, 0, unroll=2)
        _store_inv_rms(x1_ref, stat2_ref)

        def prenorm(r, carry):
            rows = _rows(r, BF16_ROWS)
            inv = stat2_ref[rows, 0:1]
            for cols in chunks:
                y = (x1_ref[rows, cols] * inv) * gpre_ref[:, cols]
                h2_ref[rows, cols] = (y * (1.0 + sc_ref[:, cols]) + sh_ref[:, cols]).astype(BF16)
            return carry

        lax.fori_loop(0, x1_ref.shape[0] // BF16_ROWS, prenorm, 0, unroll=2)


def _out_proj(att, y, w_out, x, mod, g_post, g_pre, mod_rows):
    m, wa = att.shape
    d = x.shape[1]
    tm = _row_tile(m, mod_rows)
    tk = _pick(math.gcd(wa, d - wa), (512, 256, 128))
    nk_a, nk = wa // tk, d // tk
    vec = pl.BlockSpec((1, d), lambda i, k: (0, 0))
    return pl.pallas_call(
        functools.partial(_out_proj_kernel, nk_a=nk_a, nk=nk),
        grid=(m // tm, nk),
        in_specs=[pl.BlockSpec((tm, tk), lambda i, k: (i, jnp.minimum(k, nk_a - 1))),
                  pl.BlockSpec((tm, tk), lambda i, k: (i, jnp.maximum(k - nk_a, 0))),
                  pl.BlockSpec((tk, d), lambda i, k: (k, 0)),
                  pl.BlockSpec((tm, d), lambda i, k: (i, 0)),
                  vec, _mod_spec(d, 2, mod_rows, tm), vec,
                  _mod_spec(d, 4, mod_rows, tm), _mod_spec(d, 3, mod_rows, tm)],
        out_specs=[pl.BlockSpec((tm, d), lambda i, k: (i, 0)),
                   pl.BlockSpec((tm, d), lambda i, k: (i, 0))],
        out_shape=[jax.ShapeDtypeStruct((m, d), F32), jax.ShapeDtypeStruct((m, d), BF16)],
        scratch_shapes=[pltpu.VMEM((tm, LANES), F32), pltpu.VMEM((tm, LANES), F32)],
        compiler_params=_params(("arbitrary", "arbitrary")),
        name="out_proj",
    )(att, y, w_out, x, g_post.reshape(1, d), mod, g_pre.reshape(1, d), mod, mod)


def _up_kernel(h_ref, w_ref, o_ref):
    u = jnp.maximum(jnp.dot(h_ref[...], w_ref[...], preferred_element_type=F32), 0.0)
    o_ref[...] = (u * u).astype(BF16)


def _up(h2, w_up):
    m, d = h2.shape
    f = w_up.shape[1]
    tm = _pick(m, (1024, 512, 256, 128))
    tn = _pick(f, (1024, 512, 256, 128))
    return pl.pallas_call(
        _up_kernel,
        grid=(m // tm, f // tn),
        in_specs=[pl.BlockSpec((tm, d), lambda i, j: (i, 0)),
                  pl.BlockSpec((d, tn), lambda i, j: (0, j))],
        out_specs=pl.BlockSpec((tm, tn), lambda i, j: (i, j)),
        out_shape=jax.ShapeDtypeStruct((m, f), BF16),
        compiler_params=_params(("arbitrary", "arbitrary")),
        name="mlp_up",
    )(h2, w_up)


def _down_kernel(u_ref, w_ref, x_ref, gpost_ref, gate_ref, o_ref, stat_ref, *, nk):
    k = pl.program_id(1)

    @pl.when(k == 0)
    def _():
        o_ref[...] = jnp.dot(u_ref[...], w_ref[...], preferred_element_type=F32)

    @pl.when(k > 0)
    def _():
        o_ref[...] += jnp.dot(u_ref[...], w_ref[...], preferred_element_type=F32)

    @pl.when(k == nk - 1)
    def _():
        _store_inv_rms(o_ref, stat_ref)

        def body(r, carry):
            rows = _rows(r, BF16_ROWS)
            inv = stat_ref[rows, 0:1]
            for cols in _col_chunks(o_ref.shape[1]):
                n = (o_ref[rows, cols] * inv) * gpost_ref[:, cols]
                o_ref[rows, cols] = x_ref[rows, cols] + gate_ref[:, cols] * n
            return carry

        lax.fori_loop(0, o_ref.shape[0] // BF16_ROWS, body, 0, unroll=2)


def _down(u, w_down, x1, mod, g_post, mod_rows):
    m, f = u.shape
    d = x1.shape[1]
    tm = _row_tile(m, mod_rows)
    tk = _pick(f, (1024, 512, 256, 128))
    nk = f // tk
    return pl.pallas_call(
        functools.partial(_down_kernel, nk=nk),
        grid=(m // tm, nk),
        in_specs=[pl.BlockSpec((tm, tk), lambda i, k: (i, k)),
                  pl.BlockSpec((tk, d), lambda i, k: (k, 0)),
                  pl.BlockSpec((tm, d), lambda i, k: (i, 0)),
                  pl.BlockSpec((1, d), lambda i, k: (0, 0)),
                  _mod_spec(d, 5, mod_rows, tm)],
        out_specs=pl.BlockSpec((tm, d), lambda i, k: (i, 0)),
        out_shape=jax.ShapeDtypeStruct((m, d), F32),
        scratch_shapes=[pltpu.VMEM((tm, LANES), F32)],
        compiler_params=_params(("arbitrary", "arbitrary")),
        name="mlp_down",
    )(u, w_down, x1, g_post.reshape(1, d), mod)


def _trunk(x, mod, mod_row0, rows_per_mod, w, ctx_k, ctx_v, init_f, init_b):
    b, l, d = x.shape
    m = b * l
    att_w = d // 2
    d_inner = d - att_w
    n_bc = 2 * N_GROUPS * D_STATE
    latent = ctx_k is not None

    mod_rows = (mod_row0, rows_per_mod)
    x2d = x.reshape(m, d)
    assert att_w == d_inner and (3 * att_w + 2 * d_inner) % n_bc == 0
    h = _prenorm(x2d, mod, w['g_mix_pre'], mod_rows)
    proj, dt = _in_proj(h, w['w_in_main'], w['w_in_dt'])
    proj3 = proj.reshape(b, l, -1)
    lams = (w['lq1'], w['lk1'], w['lq2'], w['lk2'])
    if latent:
        att = _attention(proj3, att_w, lams, w['g_subln'], 1, _pick(l, (256, 128)),
                         ctx_k, ctx_v, _rope_tables(l))
    else:
        att = _attention(proj3, att_w, lams, w['g_subln'], att_w // V_HD, l)
    xs, bc = _conv(proj3, w, d_inner, n_bc, 4, (3 * att_w + 2 * d_inner) // n_bc)
    y, fin_f, fin_b = _ssd(xs, bc, dt.reshape(b, l, LANES), proj3, 3, w, init_f, init_b)
    x1, h2 = _out_proj(att.reshape(m, att_w), y.reshape(m, d_inner), w['w_out'], x2d, mod,
                       w['g_mix_post'], w['g_mlp_pre'], mod_rows)
    u = _up(h2, w['w_up'])
    x2 = _down(u, w['w_down'], x1, mod, w['g_mlp_post'], mod_rows)
    return x2.reshape(b, l, d), proj[:, att_w:2 * att_w], proj[:, 2 * att_w:3 * att_w], fin_f, fin_b


def kernel(x_prompt, x_sample, c, cache_k, cache_v, state_ssm_fwd, state_ssm_bwd, c_ctx, w_ada, b_ada,
           g_mix_pre, g_mix_post, g_mlp_pre, g_mlp_post, w_in, lambda_q1, lambda_k1, lambda_q2, lambda_k2,
           g_subln, conv_w, conv_b, a_log, dt_bias, d_skip, g_ssm_norm, w_out, w_up, w_down):
    bp, lp, d = x_prompt.shape
    bd, ld, _ = x_sample.shape
    depth = w_in.shape[0]
    assert depth == 1 and bd + 1 <= MOD_ROWS
    att_w = d // 2
    d_inner = d - att_w
    heads = d_inner // SSM_HD
    n_att = att_w // V_HD
    n_main = 3 * att_w + 2 * d_inner + 2 * N_GROUPS * D_STATE
    assert 2 * heads <= LANES and w_in.shape[2] == n_main + 2 * heads

    lyr = 0
    pad_l = LANES - 2 * heads
    w = dict(
        g_mix_pre=g_mix_pre[lyr], g_mix_post=g_mix_post[lyr], g_mlp_pre=g_mlp_pre[lyr],
        g_mlp_post=g_mlp_post[lyr],
        w_in_main=w_in[lyr, :, :n_main].astype(BF16),
        w_in_dt=jnp.pad(w_in[lyr, :, n_main:], ((0, 0), (0, pad_l))).astype(BF16),
        lq1=lambda_q1[lyr].reshape(1, QK_HD), lk1=lambda_k1[lyr].reshape(1, QK_HD),
        lq2=lambda_q2[lyr].reshape(1, QK_HD), lk2=lambda_k2[lyr].reshape(1, QK_HD),
        g_subln=g_subln[lyr],
        conv_w_x=conv_w[lyr, :, :d_inner], conv_w_bc=conv_w[lyr, :, d_inner:],
        conv_b_x=conv_b[lyr, :d_inner].reshape(1, -1), conv_b_bc=conv_b[lyr, d_inner:].reshape(1, -1),
        a_log=jnp.pad(a_log[lyr].reshape(1, -1), ((0, 0), (0, pad_l))),
        dt_bias=jnp.pad(dt_bias[lyr].reshape(1, -1), ((0, 0), (0, pad_l))),
        d_skip=jnp.repeat(d_skip[lyr], SSM_HD).reshape(1, -1),
        g_ssm_norm=g_ssm_norm[lyr].reshape(1, -1),
        w_out=w_out[lyr].astype(BF16), w_up=w_up[lyr].astype(BF16), w_down=w_down[lyr].astype(BF16),
    )

    cvec = jnp.concatenate([c_ctx[None, :], c, jnp.zeros((MOD_ROWS - 1 - bd, d), F32)], axis=0)
    mod = _ada(cvec, w_ada[lyr], b_ada[lyr]).reshape(MOD_ROWS, 6, 1, d)

    yp, k_c, v_c, sf, sb = _trunk(x_prompt, mod, 0, bp * lp, w, None, None, None, None)
    ys, _, _, _, _ = _trunk(
        x_sample, mod, 1, ld, w,
        cache_k[:, lyr].reshape(bd, -1, att_w), cache_v[:, lyr].reshape(bd, -1, att_w),
        state_ssm_fwd[:, lyr].reshape(bd, d_inner, D_STATE), state_ssm_bwd[:, lyr].reshape(bd, d_inner, D_STATE))

    return (yp, ys,
            k_c.reshape(bp, 1, lp, n_att, 2, QK_HD), v_c.reshape(bp, 1, lp, n_att, V_HD),
            sf.reshape(bp, 1, heads, SSM_HD, D_STATE), sb.reshape(bp, 1, heads, SSM_HD, D_STATE))
```

```python
import functools
import math

import jax
import jax.numpy as jnp
from jax import lax
from jax.experimental import pallas as pl
from jax.experimental.pallas import tpu as pltpu

F32 = jnp.float32
BF16 = jnp.bfloat16

GRID_W = 64
V_HD = 256
QK_HD = V_HD // 2
AXIS_DIM = QK_HD // 2
ROPE_BASE = 10000.0
SSM_HD = 64
N_GROUPS = 4
D_STATE = 128
CONV_K = 5
CHUNK = 128
RMS_EPS = 1e-6
LAM_INIT = 0.8 - 0.6 * math.exp(-0.3 * 0)

V7X_VMEM_BYTES = 64 * 1024 * 1024
VMEM_LIMIT = V7X_VMEM_BYTES - 8 * 1024 * 1024
LANES = 128
SUBLANES = 8
CONV_HALO = SUBLANES
NORM_ROWS = SUBLANES
BF16_ROWS = 2 * SUBLANES
STAT_UNROLL = 16
APPLY_COLS = 1024
ROPE_ROWS = 512
MOD_ROWS = 8


def _pick(n, prefs):
    for p in prefs:
        if n % p == 0:
            return p
    return n


def _params(sem):
    return pltpu.CompilerParams(dimension_semantics=sem, vmem_limit_bytes=VMEM_LIMIT)


def _sigmoid(x):
    return 1.0 / (1.0 + jnp.exp(-x))


def _silu(x):
    return x * _sigmoid(x)


def _softplus(x):
    return jnp.maximum(x, 0.0) + jnp.log1p(jnp.exp(-jnp.abs(x)))


def _rms_rows(x, eps=RMS_EPS):
    return x * lax.rsqrt(jnp.mean(x * x, axis=-1, keepdims=True) + eps)


def _inv_rms(x, eps=RMS_EPS):
    return jnp.broadcast_to(lax.rsqrt(jnp.mean(x * x, axis=-1, keepdims=True) + eps), (x.shape[0], LANES))


def _rows(r, n):
    return pl.ds(pl.multiple_of(r * n, n), n)


def _store_inv_rms(src_ref, stat_ref):
    def body(r, carry):
        rows = _rows(r, NORM_ROWS)
        stat_ref[rows, :] = _inv_rms(src_ref[rows, :])
        return carry

    lax.fori_loop(0, src_ref.shape[0] // NORM_ROWS, body, 0, unroll=STAT_UNROLL)


def _col_chunks(d):
    w = _pick(d, (APPLY_COLS,))
    return [slice(c * w, (c + 1) * w) for c in range(d // w)]


def _ada_kernel(c_ref, w_ref, b_ref, o_ref):
    c = c_ref[...]
    s = _silu(c).astype(BF16)
    o_ref[...] = jnp.dot(s, w_ref[...].astype(BF16), preferred_element_type=F32) + b_ref[...]


def _ada(cvec, w_ada, b_ada):
    rows, d = cvec.shape
    n = w_ada.shape[1]
    tn = _pick(n, (512, 256, 128))
    return pl.pallas_call(
        _ada_kernel,
        grid=(n // tn,),
        in_specs=[pl.BlockSpec((rows, d), lambda j: (0, 0)),
                  pl.BlockSpec((d, tn), lambda j: (0, j)),
                  pl.BlockSpec((1, tn), lambda j: (0, j))],
        out_specs=pl.BlockSpec((rows, tn), lambda j: (0, j)),
        out_shape=jax.ShapeDtypeStruct((rows, n), F32),
        compiler_params=_params(("arbitrary",)),
        name="ada_mod",
    )(cvec, w_ada, b_ada.reshape(1, n))


def _row_tile(m, mod_rows, prefs=(512, 256, 128)):
    return _pick(math.gcd(m, mod_rows[1]), prefs)


def _mod_spec(d, which, mod_rows, tm):
    row0, per = mod_rows
    return pl.BlockSpec((None, None, 1, d), lambda i, j: (row0 + (i * tm) // per, which, 0, 0))


def _prenorm_kernel(x_ref, g_ref, sc_ref, sh_ref, h_ref, stat_ref):
    _store_inv_rms(x_ref, stat_ref)

    def body(r, carry):
        rows = _rows(r, BF16_ROWS)
        inv = stat_ref[rows, 0:1]
        for cols in _col_chunks(x_ref.shape[1]):
            y = (x_ref[rows, cols] * inv) * g_ref[:, cols]
            h_ref[rows, cols] = (y * (1.0 + sc_ref[:, cols]) + sh_ref[:, cols]).astype(BF16)
        return carry

    lax.fori_loop(0, x_ref.shape[0] // BF16_ROWS, body, 0, unroll=2)


def _prenorm(x, mod, g, mod_rows):
    m, d = x.shape
    tm = _row_tile(m, mod_rows, (256, 128))
    return pl.pallas_call(
        _prenorm_kernel,
        grid=(m // tm, 1),
        in_specs=[pl.BlockSpec((tm, d), lambda i, j: (i, 0)),
                  pl.BlockSpec((1, d), lambda i, j: (0, 0)),
                  _mod_spec(d, 1, mod_rows, tm),
                  _mod_spec(d, 0, mod_rows, tm)],
        out_specs=pl.BlockSpec((tm, d), lambda i, j: (i, 0)),
        out_shape=jax.ShapeDtypeStruct((m, d), BF16),
        scratch_shapes=[pltpu.VMEM((tm, LANES), F32)],
        compiler_params=_params(("arbitrary", "arbitrary")),
        name="mix_prenorm",
    )(x, g.reshape(1, d), mod, mod)


def _in_proj_kernel(h_ref, w_ref, wdt_ref, o_ref, dt_ref):
    @pl.when(pl.program_id(1) == 0)
    def _():
        dt_ref[...] = jnp.dot(h_ref[...], wdt_ref[...], preferred_element_type=F32)

    o_ref[...] = jnp.dot(h_ref[...], w_ref[...], preferred_element_type=F32)


def _in_proj(h, w_main, w_dt):
    m, d = h.shape
    n = w_main.shape[1]
    tm = _pick(m, (1024, 512, 256, 128))
    tn = _pick(n, (1024, 512, 256, 128))
    return pl.pallas_call(
        _in_proj_kernel,
        grid=(m // tm, n // tn),
        in_specs=[pl.BlockSpec((tm, d), lambda i, j: (i, 0)),
                  pl.BlockSpec((d, tn), lambda i, j: (0, j)),
                  pl.BlockSpec((d, LANES), lambda i, j: (0, 0))],
        out_specs=[pl.BlockSpec((tm, tn), lambda i, j: (i, j)),
                   pl.BlockSpec((tm, LANES), lambda i, j: (i, 0))],
        out_shape=[jax.ShapeDtypeStruct((m, n), F32), jax.ShapeDtypeStruct((m, LANES), F32)],
        compiler_params=_params(("arbitrary", "arbitrary")),
        name="in_proj",
    )(h, w_main, w_dt)


def _rope_tile(x, cos, sin_signed, first_half):
    half = AXIS_DIM // 2
    swapped = jnp.where(first_half, pltpu.roll(x, LANES - half, axis=1), pltpu.roll(x, half, axis=1))
    return x * cos + swapped * sin_signed


def _rope_tables(l):
    rows = l // GRID_W
    row = jnp.repeat(jnp.arange(rows, dtype=F32), GRID_W)
    col = jnp.tile(jnp.arange(GRID_W, dtype=F32), rows)
    inv = 1.0 / (ROPE_BASE ** (jnp.arange(0, AXIS_DIM, 2, dtype=F32) / AXIS_DIM))

    def one(pos):
        ang = pos[:, None] * inv[None, :]
        c, s = jnp.cos(ang), jnp.sin(ang)
        return jnp.concatenate([c, c], axis=-1), jnp.concatenate([-s, s], axis=-1)

    cr, sr = one(row)
    cc, sc = one(col)
    return jnp.concatenate([cr, cc], axis=-1), jnp.concatenate([sr, sc], axis=-1)


def _attn_kernel(*refs, heads, ck, n_ctx):
    lq1_ref, lk1_ref, lq2_ref, lk2_ref, g_ref, q_ref, k_ref, v_ref = refs[:8]
    if n_ctx:
        ck_ref, cv_ref, cosq_ref, sinq_ref, cosk_ref, sink_ref, o_ref, kb_ref, vb_ref = refs[8:]
        lane = lax.broadcasted_iota(jnp.int32, (1, LANES), 1)
        first_half = (lane % AXIS_DIM) < (AXIS_DIM // 2)

        @pl.when(pl.program_id(2) == 0)
        def _():
            n_new = k_ref.shape[0]
            kb_ref[0:n_ctx, :] = ck_ref[...].astype(BF16)
            vb_ref[0:n_ctx, :] = cv_ref[...].astype(BF16)
            vb_ref[n_ctx:n_ctx + n_new, :] = v_ref[...].astype(BF16)
            step = _pick(n_new, (ROPE_ROWS,))
            for r0 in range(0, n_new, step):
                rows = slice(r0, r0 + step)
                for c0 in range(0, heads * V_HD, LANES):
                    cols = slice(c0, c0 + LANES)
                    roped = _rope_tile(k_ref[rows, cols], cosk_ref[rows, :], sink_ref[rows, :], first_half)
                    kb_ref[n_ctx + r0:n_ctx + r0 + step, cols] = roped.astype(BF16)

        k_src, v_src = kb_ref, vb_ref
    else:
        o_ref = refs[8]
        k_src, v_src = k_ref, v_ref

    lam = (jnp.exp(jnp.sum(lq1_ref[...] * lk1_ref[...], axis=-1, keepdims=True))
           - jnp.exp(jnp.sum(lq2_ref[...] * lk2_ref[...], axis=-1, keepdims=True)) + LAM_INIT)
    c2 = (QK_HD ** -0.5) * math.log2(math.e)
    nck = k_src.shape[0] // ck
    for h in range(heads):
        base = h * V_HD
        outs = []
        for half in range(2):
            cols = slice(base + half * QK_HD, base + (half + 1) * QK_HD)
            q = q_ref[:, cols]
            if n_ctx:
                q = _rope_tile(q, cosq_ref[...], sinq_ref[...], first_half)
            q = q.astype(BF16)
            m = l = acc = None
            for c in range(nck):
                rows = slice(c * ck, (c + 1) * ck)
                k = k_src[rows, cols].astype(BF16)
                s = lax.dot_general(q, k, (((1,), (1,)), ((), ())), preferred_element_type=F32)
                mc = jnp.max(s, axis=-1, keepdims=True)
                m_new = mc if c == 0 else jnp.maximum(m, mc)
                e = jnp.exp2((s - m_new) * c2)
                ls = jnp.sum(e, axis=-1, keepdims=True)
                pv = jnp.dot(e.astype(BF16), v_src[rows, base:base + V_HD].astype(BF16),
                             preferred_element_type=F32)
                if c == 0:
                    l, acc = ls, pv
                else:
                    alpha = jnp.exp2((m - m_new) * c2)
                    l = alpha * l + ls
                    acc = alpha * acc + pv
                m = m_new
            outs.append(acc * (1.0 / l))
        o = outs[0] - lam * outs[1]
        o_ref[:, base:base + V_HD] = ((_rms_rows(o) * g_ref[...]) * (1.0 - LAM_INIT)).astype(BF16)


def _attention(proj, w, lams, g_subln, heads_per_step, tq, ctx_k=None, ctx_v=None, tables=None):
    b, l, _ = proj.shape
    hw = heads_per_step * V_HD
    nh = w // hw
    n_ctx = 0 if ctx_k is None else ctx_k.shape[1]
    ck = _pick(n_ctx + l, (1536, 1024, 512, 256))
    small = pl.BlockSpec((1, QK_HD), lambda bi, hi, qi: (0, 0))
    in_specs = [small, small, small, small,
                pl.BlockSpec((1, V_HD), lambda bi, hi, qi: (0, 0)),
                pl.BlockSpec((None, tq, hw), lambda bi, hi, qi: (bi, qi, hi)),
                pl.BlockSpec((None, l, hw), lambda bi, hi, qi: (bi, 0, nh + hi)),
                pl.BlockSpec((None, l, hw), lambda bi, hi, qi: (bi, 0, 2 * nh + hi))]
    args = [*lams, g_subln.reshape(1, V_HD), proj, proj, proj]
    scratch = []
    if n_ctx:
        ctx_spec = pl.BlockSpec((None, n_ctx, hw), lambda bi, hi, qi: (bi, 0, hi))
        q_tab = pl.BlockSpec((tq, LANES), lambda bi, hi, qi: (qi, 0))
        k_tab = pl.BlockSpec((l, LANES), lambda bi, hi, qi: (0, 0))
        in_specs += [ctx_spec, ctx_spec, q_tab, q_tab, k_tab, k_tab]
        args += [ctx_k, ctx_v, tables[0], tables[1], tables[0], tables[1]]
        scratch = [pltpu.VMEM((n_ctx + l, hw), BF16), pltpu.VMEM((n_ctx + l, hw), BF16)]
    return pl.pallas_call(
        functools.partial(_attn_kernel, heads=heads_per_step, ck=ck, n_ctx=n_ctx),
        grid=(b, nh, l // tq),
        in_specs=in_specs,
        out_specs=pl.BlockSpec((None, tq, hw), lambda bi, hi, qi: (bi, qi, hi)),
        out_shape=jax.ShapeDtypeStruct((b, l, w), BF16),
        scratch_shapes=scratch,
        compiler_params=_params(("arbitrary", "arbitrary", "arbitrary")),
        name="diff_attn",
    )(*args)


def _conv_silu(win_ref, prev_ref, main_ref, next_ref, w_ref, b_ref, has_prev, has_next):
    n = main_ref.shape[0]
    win_ref[0:CONV_HALO, :] = jnp.where(has_prev, prev_ref[...], 0.0)
    win_ref[CONV_HALO:CONV_HALO + n, :] = main_ref[...]
    win_ref[CONV_HALO + n:CONV_HALO + n + CONV_HALO, :] = jnp.where(has_next, next_ref[...], 0.0)
    first = CONV_HALO - CONV_K // 2
    out = b_ref[...]
    for j in range(CONV_K):
        out = out + w_ref[j:j + 1, :] * win_ref[first + j:first + j + n, :]
    return _silu(out)


def _decay_terms(dt_raw, dt_bias, a_log, heads):
    dt = _softplus(dt_raw + dt_bias)
    a = dt * (-jnp.exp(a_log))
    row = lax.broadcasted_iota(jnp.int32, (CHUNK, LANES), 0)
    cs = a
    step = 1
    while step < CHUNK:
        cs = cs + jnp.where(row >= step, pltpu.roll(cs, step, axis=0), 0.0)
        step *= 2
    total = cs[CHUNK - 1:CHUNK, :]
    lane = lax.broadcasted_iota(jnp.int32, (CHUNK, LANES), 1)
    cum = jnp.where(lane < heads, cs, total - cs + a)
    cum_t = cum.T
    dt_t = dt.T
    tot_t = jnp.broadcast_to(total, (CHUNK, LANES)).T
    log2e = math.log2(math.e)
    return (cum * log2e,
            cum_t * log2e,
            dt_t,
            jnp.exp(tot_t - cum_t) * dt_t,
            jnp.exp(cum_t),
            jnp.exp(tot_t))


def _conv_kernel(xs_p, xs_m, xs_n, bc_p, bc_m, bc_n, cwx_ref, cbx_ref, cwbc_ref, cbbc_ref,
                 xo_ref, bco_ref, winx_ref, winbc_ref, *, nt):
    i = pl.program_id(1)
    has_prev = i > 0
    has_next = i < nt - 1
    xo_ref[...] = _conv_silu(winx_ref, xs_p, xs_m, xs_n, cwx_ref, cbx_ref, has_prev, has_next)
    bco_ref[...] = _conv_silu(winbc_ref, bc_p, bc_m, bc_n, cwbc_ref, cbbc_ref, has_prev, has_next).astype(BF16)


def _conv(proj, p, dx, nbc, xc, bcc):
    b, l, _ = proj.shape
    tr = _pick(l, (256, 128))
    nt = l // tr
    hpt = tr // CONV_HALO
    nhalo = l // CONV_HALO

    def main(wd, col):
        return pl.BlockSpec((None, tr, wd), lambda bi, i: (bi, i, col))

    def halo_prev(wd, col):
        return pl.BlockSpec((None, CONV_HALO, wd), lambda bi, i: (bi, jnp.maximum(i * hpt - 1, 0), col))

    def halo_next(wd, col):
        return pl.BlockSpec((None, CONV_HALO, wd),
                            lambda bi, i: (bi, jnp.minimum((i + 1) * hpt, nhalo - 1), col))

    def full(shape):
        return pl.BlockSpec(shape, lambda bi, i: (0,) * len(shape))

    return pl.pallas_call(
        functools.partial(_conv_kernel, nt=nt),
        grid=(b, nt),
        in_specs=[halo_prev(dx, xc), main(dx, xc), halo_next(dx, xc),
                  halo_prev(nbc, bcc), main(nbc, bcc), halo_next(nbc, bcc),
                  full((CONV_K, dx)), full((1, dx)), full((CONV_K, nbc)), full((1, nbc))],
        out_specs=[pl.BlockSpec((None, tr, dx), lambda bi, i: (bi, i, 0)),
                   pl.BlockSpec((None, tr, nbc), lambda bi, i: (bi, i, 0))],
        out_shape=[jax.ShapeDtypeStruct((b, l, dx), F32), jax.ShapeDtypeStruct((b, l, nbc), BF16)],
        scratch_shapes=[pltpu.VMEM((tr + 2 * CONV_HALO, dx), F32),
                        pltpu.VMEM((tr + 2 * CONV_HALO, nbc), F32)],
        compiler_params=_params(("arbitrary", "arbitrary")),
        name="conv_silu",
    )(proj, proj, proj, proj, proj, proj, p['conv_w_x'], p['conv_b_x'], p['conv_w_bc'], p['conv_b_bc'])


def _ssd_kernel(*refs, nc, heads, has_init):
    xs_ref, bc_ref, dt_ref, z_ref, alog_ref, dtb_ref, dskip_ref, gn_ref = refs[:8]
    idx = 8
    if has_init:
        initf_ref, initb_ref = refs[idx], refs[idx + 1]
        idx += 2
    y_ref, sf_ref, sb_ref, carf_ref, carb_ref, prev_ref = refs[idx:idx + 6]

    phase = pl.program_id(1)
    i = pl.program_id(2)
    c = jnp.where(phase == 0, i, nc - 1 - i)
    gw = heads // N_GROUPS * SSM_HD
    hpg = heads // N_GROUPS

    @pl.when((phase == 0) & (i == 0))
    def _():
        if has_init:
            carf_ref[...] = initf_ref[...]
            carb_ref[...] = initb_ref[...]
        else:
            carf_ref[...] = jnp.zeros_like(carf_ref)
            carb_ref[...] = jnp.zeros_like(carb_ref)

    row = lax.broadcasted_iota(jnp.int32, (CHUNK, CHUNK), 0)
    colm = lax.broadcasted_iota(jnp.int32, (CHUNK, CHUNK), 1)
    cum2, cum2_t, dt_t, w_t, e_t, dec_t = _decay_terms(dt_ref[...], dtb_ref[...], alog_ref[...], heads)

    xs = xs_ref[...]
    bc = bc_ref[...]
    x_t = xs.T

    def state_update(car_ref, off):
        for g in range(N_GROUPS):
            b_g = bc[:, g * D_STATE:(g + 1) * D_STATE].astype(BF16)
            rows = []
            for hh in range(hpg):
                h = g * hpg + hh
                rows.append(x_t[h * SSM_HD:(h + 1) * SSM_HD, :] * w_t[off + h:off + h + 1, :])
            xw = jnp.concatenate(rows, axis=0).astype(BF16)
            st = jnp.dot(xw, b_g, preferred_element_type=F32)
            for hh in range(hpg):
                h = g * hpg + hh
                r = slice(h * SSM_HD, (h + 1) * SSM_HD)
                car_ref[r, :] = car_ref[r, :] * dec_t[off + h:off + h + 1, :] + st[hh * SSM_HD:(hh + 1) * SSM_HD, :]

    @pl.when(phase == 0)
    def _():
        prev_ref[c] = carf_ref[...].astype(BF16)
        state_update(carf_ref, 0)

        @pl.when(i == nc - 1)
        def _():
            sf_ref[...] = carf_ref[...]

    @pl.when(phase == 1)
    def _():
        causal = colm >= row
        anti = colm <= row
        y_parts = []
        for g in range(N_GROUPS):
            b_g = bc[:, g * D_STATE:(g + 1) * D_STATE].astype(BF16)
            c_g = bc[:, (N_GROUPS + g) * D_STATE:(N_GROUPS + g + 1) * D_STATE].astype(BF16)
            cb_t = lax.dot_general(b_g, c_g, (((1,), (1,)), ((), ())), preferred_element_type=F32)
            rg = slice(g * gw, (g + 1) * gw)
            yo_f = lax.dot_general(prev_ref[c, rg, :], c_g, (((1,), (1,)), ((), ())),
                                   preferred_element_type=F32)
            yo_b = lax.dot_general(carb_ref[rg, :].astype(BF16), c_g, (((1,), (1,)), ((), ())),
                                   preferred_element_type=F32)
            for hh in range(hpg):
                h = g * hpg + hh
                hb = heads + h
                x_h = x_t[h * SSM_HD:(h + 1) * SSM_HD, :]
                seg_f = cum2_t[h:h + 1, :] - cum2[:, h:h + 1]
                seg_b = cum2_t[hb:hb + 1, :] - cum2[:, hb:hb + 1]
                w_f = cb_t * jnp.exp2(jnp.where(causal, seg_f, -1e4))
                w_b = cb_t * jnp.exp2(jnp.where(anti, seg_b, -1e4))
                lhs = jnp.concatenate([x_h * dt_t[h:h + 1, :], x_h * dt_t[hb:hb + 1, :]], axis=1)
                rhs = jnp.concatenate([w_f, w_b], axis=0)
                yd = jnp.dot(lhs.astype(BF16), rhs.astype(BF16), preferred_element_type=F32)
                r = slice(hh * SSM_HD, (hh + 1) * SSM_HD)
                y_parts.append(yd + yo_f[r, :] * e_t[h:h + 1, :] + yo_b[r, :] * e_t[hb:hb + 1, :])
        y = jnp.concatenate(y_parts, axis=0).T
        y = y + dskip_ref[...] * xs
        y = y * _silu(z_ref[...])
        for g in range(N_GROUPS):
            cols = slice(g * gw, (g + 1) * gw)
            y_ref[:, cols] = (_rms_rows(y[:, cols]) * gn_ref[:, cols]).astype(BF16)
        state_update(carb_ref, heads)

        @pl.when(i == nc - 1)
        def _():
            sb_ref[...] = carb_ref[...]


def _ssd(xs, bc, dt, proj, zc, p, init_f, init_b):
    b, l, dx = xs.shape
    nbc = bc.shape[2]
    heads = dx // SSM_HD
    nc = l // CHUNK
    has_init = init_f is not None

    def chunk_of(ph, i):
        return jnp.where(ph == 0, i, nc - 1 - i)

    def main(wd, col):
        return pl.BlockSpec((None, CHUNK, wd), lambda bi, ph, i: (bi, chunk_of(ph, i), col))

    def full(shape):
        return pl.BlockSpec(shape, lambda bi, ph, i: (0,) * len(shape))

    state_spec = pl.BlockSpec((None, dx, D_STATE), lambda bi, ph, i: (bi, 0, 0))
    in_specs = [main(dx, 0), main(nbc, 0), main(LANES, 0), main(dx, zc),
                full((1, LANES)), full((1, LANES)), full((1, dx)), full((1, dx))]
    args = [xs, bc, dt, proj, p['a_log'], p['dt_bias'], p['d_skip'], p['g_ssm_norm']]
    if has_init:
        in_specs += [state_spec, state_spec]
        args += [init_f, init_b]
    y_spec = pl.BlockSpec((None, CHUNK, dx), lambda bi, ph, i: (bi, jnp.where(ph == 0, nc - 1, nc - 1 - i), 0))
    return pl.pallas_call(
        functools.partial(_ssd_kernel, nc=nc, heads=heads, has_init=has_init),
        grid=(b, 2, nc),
        in_specs=in_specs,
        out_specs=[y_spec, state_spec, state_spec],
        out_shape=[jax.ShapeDtypeStruct((b, l, dx), BF16),
                   jax.ShapeDtypeStruct((b, dx, D_STATE), F32),
                   jax.ShapeDtypeStruct((b, dx, D_STATE), F32)],
        scratch_shapes=[pltpu.VMEM((dx, D_STATE), F32), pltpu.VMEM((dx, D_STATE), F32),
                        pltpu.VMEM((nc, dx, D_STATE), BF16)],
        compiler_params=_params(("arbitrary", "arbitrary", "arbitrary")),
        name="ssd_scan",
    )(*args)


def _out_proj_kernel(a_ref, y_ref, wa_ref, wy_ref, o_ref):
    o_ref[...] = (jnp.dot(a_ref[...], wa_ref[...], preferred_element_type=F32)
                  + jnp.dot(y_ref[...], wy_ref[...], preferred_element_type=F32))


def _out_proj(att, y, w_out):
    m, wa = att.shape
    d = w_out.shape[1]
    assert y.shape[1] == wa and w_out.shape[0] == 2 * wa
    tm = _pick(m, (1024, 512, 256, 128))
    tn = _pick(d, (1024, 512, 256, 128))
    return pl.pallas_call(
        _out_proj_kernel,
        grid=(m // tm, d // tn),
        in_specs=[pl.BlockSpec((tm, wa), lambda i, j: (i, 0)),
                  pl.BlockSpec((tm, wa), lambda i, j: (i, 0)),
                  pl.BlockSpec((wa, tn), lambda i, j: (0, j)),
                  pl.BlockSpec((wa, tn), lambda i, j: (1, j))],
        out_specs=pl.BlockSpec((tm, tn), lambda i, j: (i, j)),
        out_shape=jax.ShapeDtypeStruct((m, d), F32),
        compiler_params=_params(("arbitrary", "arbitrary")),
        name="out_proj",
    )(att, y, w_out, w_out)


def _sandwich_kernel(mix_ref, x_ref, gpost_ref, gate_ref, gpre_ref, sc_ref, sh_ref,
                     x1_ref, h2_ref, stat_ref, stat2_ref):
    chunks = _col_chunks(x1_ref.shape[1])
    _store_inv_rms(mix_ref, stat_ref)

    def residual(r, carry):
        rows = _rows(r, BF16_ROWS)
        inv = stat_ref[rows, 0:1]
        for cols in chunks:
            n = (mix_ref[rows, cols] * inv) * gpost_ref[:, cols]
            x1_ref[rows, cols] = x_ref[rows, cols] + gate_ref[:, cols] * n
        return carry

    lax.fori_loop(0, x1_ref.shape[0] // BF16_ROWS, residual, 0, unroll=2)
    _store_inv_rms(x1_ref, stat2_ref)

    def prenorm(r, carry):
        rows = _rows(r, BF16_ROWS)
        inv = stat2_ref[rows, 0:1]
        for cols in chunks:
            y = (x1_ref[rows, cols] * inv) * gpre_ref[:, cols]
            h2_ref[rows, cols] = (y * (1.0 + sc_ref[:, cols]) + sh_ref[:, cols]).astype(BF16)
        return carry

    lax.fori_loop(0, x1_ref.shape[0] // BF16_ROWS, prenorm, 0, unroll=2)


def _sandwich(mix, x, mod, g_post, g_pre, mod_rows):
    m, d = x.shape
    tm = _row_tile(m, mod_rows, (256, 128))
    vec = pl.BlockSpec((1, d), lambda i, j: (0, 0))
    tile = pl.BlockSpec((tm, d), lambda i, j: (i, 0))
    return pl.pallas_call(
        _sandwich_kernel,
        grid=(m // tm, 1),
        in_specs=[tile, tile, vec, _mod_spec(d, 2, mod_rows, tm), vec,
                  _mod_spec(d, 4, mod_rows, tm), _mod_spec(d, 3, mod_rows, tm)],
        out_specs=[tile, tile],
        out_shape=[jax.ShapeDtypeStruct((m, d), F32), jax.ShapeDtypeStruct((m, d), BF16)],
        scratch_shapes=[pltpu.VMEM((tm, LANES), F32), pltpu.VMEM((tm, LANES), F32)],
        compiler_params=_params(("arbitrary", "arbitrary")),
        name="mix_sandwich",
    )(mix, x, g_post.reshape(1, d), mod, g_pre.reshape(1, d), mod, mod)


def _up_kernel(h_ref, w_ref, o_ref):
    u = jnp.maximum(jnp.dot(h_ref[...], w_ref[...], preferred_element_type=F32), 0.0)
    o_ref[...] = (u * u).astype(BF16)


def _up(h2, w_up):
    m, d = h2.shape
    f = w_up.shape[1]
    tm = _pick(m, (1024, 512, 256, 128))
    tn = _pick(f, (1024, 512, 256, 128))
    return pl.pallas_call(
        _up_kernel,
        grid=(m // tm, f // tn),
        in_specs=[pl.BlockSpec((tm, d), lambda i, j: (i, 0)),
                  pl.BlockSpec((d, tn), lambda i, j: (0, j))],
        out_specs=pl.BlockSpec((tm, tn), lambda i, j: (i, j)),
        out_shape=jax.ShapeDtypeStruct((m, f), BF16),
        compiler_params=_params(("arbitrary", "arbitrary")),
        name="mlp_up",
    )(h2, w_up)


def _down_kernel(u_ref, w_ref, x_ref, gpost_ref, gate_ref, o_ref, stat_ref, *, nk):
    k = pl.program_id(1)

    @pl.when(k == 0)
    def _():
        o_ref[...] = jnp.dot(u_ref[...], w_ref[...], preferred_element_type=F32)

    @pl.when(k > 0)
    def _():
        o_ref[...] += jnp.dot(u_ref[...], w_ref[...], preferred_element_type=F32)

    @pl.when(k == nk - 1)
    def _():
        _store_inv_rms(o_ref, stat_ref)

        def body(r, carry):
            rows = _rows(r, BF16_ROWS)
            inv = stat_ref[rows, 0:1]
            for cols in _col_chunks(o_ref.shape[1]):
                n = (o_ref[rows, cols] * inv) * gpost_ref[:, cols]
                o_ref[rows, cols] = x_ref[rows, cols] + gate_ref[:, cols] * n
            return carry

        lax.fori_loop(0, o_ref.shape[0] // BF16_ROWS, body, 0, unroll=2)


def _down(u, w_down, x1, mod, g_post, mod_rows):
    m, f = u.shape
    d = x1.shape[1]
    tm = _row_tile(m, mod_rows)
    tk = _pick(f, (1024, 512, 256, 128))
    nk = f // tk
    return pl.pallas_call(
        functools.partial(_down_kernel, nk=nk),
        grid=(m // tm, nk),
        in_specs=[pl.BlockSpec((tm, tk), lambda i, k: (i, k)),
                  pl.BlockSpec((tk, d), lambda i, k: (k, 0)),
                  pl.BlockSpec((tm, d), lambda i, k: (i, 0)),
                  pl.BlockSpec((1, d), lambda i, k: (0, 0)),
                  _mod_spec(d, 5, mod_rows, tm)],
        out_specs=pl.BlockSpec((tm, d), lambda i, k: (i, 0)),
        out_shape=jax.ShapeDtypeStruct((m, d), F32),
        scratch_shapes=[pltpu.VMEM((tm, LANES), F32)],
        compiler_params=_params(("arbitrary", "arbitrary")),
        name="mlp_down",
    )(u, w_down, x1, g_post.reshape(1, d), mod)


def _trunk(x, mod, mod_row0, rows_per_mod, w, ctx_k, ctx_v, init_f, init_b):
    b, l, d = x.shape
    m = b * l
    att_w = d // 2
    d_inner = d - att_w
    n_bc = 2 * N_GROUPS * D_STATE
    latent = ctx_k is not None

    mod_rows = (mod_row0, rows_per_mod)
    x2d = x.reshape(m, d)
    assert att_w == d_inner and (3 * att_w + 2 * d_inner) % n_bc == 0
    h = _prenorm(x2d, mod, w['g_mix_pre'], mod_rows)
    proj, dt = _in_proj(h, w['w_in_main'], w['w_in_dt'])
    proj3 = proj.reshape(b, l, -1)
    lams = (w['lq1'], w['lk1'], w['lq2'], w['lk2'])
    if latent:
        att = _attention(proj3, att_w, lams, w['g_subln'], 1, _pick(l, (512, 256, 128)),
                         ctx_k, ctx_v, _rope_tables(l))
    else:
        att = _attention(proj3, att_w, lams, w['g_subln'], att_w // V_HD, l)
    xs, bc = _conv(proj3, w, d_inner, n_bc, 4, (3 * att_w + 2 * d_inner) // n_bc)
    y, fin_f, fin_b = _ssd(xs, bc, dt.reshape(b, l, LANES), proj3, 3, w, init_f, init_b)
    mix = _out_proj(att.reshape(m, att_w), y.reshape(m, d_inner), w['w_out'])
    x1, h2 = _sandwich(mix, x2d, mod, w['g_mix_post'], w['g_mlp_pre'], mod_rows)
    u = _up(h2, w['w_up'])
    x2 = _down(u, w['w_down'], x1, mod, w['g_mlp_post'], mod_rows)
    return x2.reshape(b, l, d), proj[:, att_w:2 * att_w], proj[:, 2 * att_w:3 * att_w], fin_f, fin_b


def kernel(x_prompt, x_sample, c, cache_k, cache_v, state_ssm_fwd, state_ssm_bwd, c_ctx, w_ada, b_ada,
           g_mix_pre, g_mix_post, g_mlp_pre, g_mlp_post, w_in, lambda_q1, lambda_k1, lambda_q2, lambda_k2,
           g_subln, conv_w, conv_b, a_log, dt_bias, d_skip, g_ssm_norm, w_out, w_up, w_down):
    bp, lp, d = x_prompt.shape
    bd, ld, _ = x_sample.shape
    depth = w_in.shape[0]
    assert depth == 1 and bd + 1 <= MOD_ROWS
    att_w = d // 2
    d_inner = d - att_w
    heads = d_inner // SSM_HD
    n_att = att_w // V_HD
    n_main = 3 * att_w + 2 * d_inner + 2 * N_GROUPS * D_STATE
    assert 2 * heads <= LANES and w_in.shape[2] == n_main + 2 * heads

    lyr = 0
    pad_l = LANES - 2 * heads
    w = dict(
        g_mix_pre=g_mix_pre[lyr], g_mix_post=g_mix_post[lyr], g_mlp_pre=g_mlp_pre[lyr],
        g_mlp_post=g_mlp_post[lyr],
        w_in_main=w_in[lyr, :, :n_main].astype(BF16),
        w_in_dt=jnp.pad(w_in[lyr, :, n_main:], ((0, 0), (0, pad_l))).astype(BF16),
        lq1=lambda_q1[lyr].reshape(1, QK_HD), lk1=lambda_k1[lyr].reshape(1, QK_HD),
        lq2=lambda_q2[lyr].reshape(1, QK_HD), lk2=lambda_k2[lyr].reshape(1, QK_HD),
        g_subln=g_subln[lyr],
        conv_w_x=conv_w[lyr, :, :d_inner], conv_w_bc=conv_w[lyr, :, d_inner:],
        conv_b_x=conv_b[lyr, :d_inner].reshape(1, -1), conv_b_bc=conv_b[lyr, d_inner:].reshape(1, -1),
        a_log=jnp.pad(a_log[lyr].reshape(1, -1), ((0, 0), (0, pad_l))),
        dt_bias=jnp.pad(dt_bias[lyr].reshape(1, -1), ((0, 0), (0, pad_l))),
        d_skip=jnp.repeat(d_skip[lyr], SSM_HD).reshape(1, -1),
        g_ssm_norm=g_ssm_norm[lyr].reshape(1, -1),
        w_out=w_out[lyr].astype(BF16), w_up=w_up[lyr].astype(BF16), w_down=w_down[lyr].astype(BF16),
    )

    cvec = jnp.concatenate([c_ctx[None, :], c, jnp.zeros((MOD_ROWS - 1 - bd, d), F32)], axis=0)
    mod = _ada(cvec, w_ada[lyr], b_ada[lyr]).reshape(MOD_ROWS, 6, 1, d)

    yp, k_c, v_c, sf, sb = _trunk(x_prompt, mod, 0, bp * lp, w, None, None, None, None)
    ys, _, _, _, _ = _trunk(
        x_sample, mod, 1, ld, w,
        cache_k[:, lyr].reshape(bd, -1, att_w), cache_v[:, lyr].reshape(bd, -1, att_w),
        state_ssm_fwd[:, lyr].reshape(bd, d_inner, D_STATE), state_ssm_bwd[:, lyr].reshape(bd, d_inner, D_STATE))

    return (yp, ys,
            k_c.reshape(bp, 1, lp, n_att, 2, QK_HD), v_c.reshape(bp, 1, lp, n_att, V_HD),
            sf.reshape(bp, 1, heads, SSM_HD, D_STATE), sb.reshape(bp, 1, heads, SSM_HD, D_STATE))
```

```python
import functools
import math

import jax
import jax.numpy as jnp
from jax import lax
from jax.experimental import pallas as pl
from jax.experimental.pallas import tpu as pltpu

F32 = jnp.float32
BF16 = jnp.bfloat16

GRID_W = 64
V_HD = 256
QK_HD = V_HD // 2
AXIS_DIM = QK_HD // 2
ROPE_BASE = 10000.0
SSM_HD = 64
N_GROUPS = 4
D_STATE = 128
CONV_K = 5
CHUNK = 128
RMS_EPS = 1e-6
LAM_INIT = 0.8 - 0.6 * math.exp(-0.3 * 0)

V7X_VMEM_BYTES = 64 * 1024 * 1024
VMEM_LIMIT = V7X_VMEM_BYTES - 8 * 1024 * 1024
LANES = 128
SUBLANES = 8
CONV_HALO = SUBLANES
NORM_ROWS = SUBLANES
BF16_ROWS = 2 * SUBLANES
STAT_UNROLL = 16
APPLY_COLS = 1024
ROPE_ROWS = 512
MOD_ROWS = 8


def _pick(n, prefs):
    for p in prefs:
        if n % p == 0:
            return p
    return n


def _params(sem):
    return pltpu.CompilerParams(dimension_semantics=sem, vmem_limit_bytes=VMEM_LIMIT)


def _sigmoid(x):
    return 1.0 / (1.0 + jnp.exp(-x))


def _silu(x):
    return x * _sigmoid(x)


def _softplus(x):
    return jnp.maximum(x, 0.0) + jnp.log1p(jnp.exp(-jnp.abs(x)))


def _rms_rows(x, eps=RMS_EPS):
    return x * lax.rsqrt(jnp.mean(x * x, axis=-1, keepdims=True) + eps)


def _inv_rms(x, eps=RMS_EPS):
    return jnp.broadcast_to(lax.rsqrt(jnp.mean(x * x, axis=-1, keepdims=True) + eps), (x.shape[0], LANES))


def _rows(r, n):
    return pl.ds(pl.multiple_of(r * n, n), n)


def _store_inv_rms(src_ref, stat_ref):
    def body(r, carry):
        rows = _rows(r, NORM_ROWS)
        stat_ref[rows, :] = _inv_rms(src_ref[rows, :])
        return carry

    lax.fori_loop(0, src_ref.shape[0] // NORM_ROWS, body, 0, unroll=STAT_UNROLL)


def _col_chunks(d):
    w = _pick(d, (APPLY_COLS,))
    return [slice(c * w, (c + 1) * w) for c in range(d // w)]


def _ada_kernel(c_ref, w_ref, b_ref, o_ref):
    c = c_ref[...]
    s = _silu(c).astype(BF16)
    o_ref[...] = jnp.dot(s, w_ref[...].astype(BF16), preferred_element_type=F32) + b_ref[...]


def _ada(cvec, w_ada, b_ada):
    rows, d = cvec.shape
    n = w_ada.shape[1]
    tn = _pick(n, (512, 256, 128))
    return pl.pallas_call(
        _ada_kernel,
        grid=(n // tn,),
        in_specs=[pl.BlockSpec((rows, d), lambda j: (0, 0)),
                  pl.BlockSpec((d, tn), lambda j: (0, j)),
                  pl.BlockSpec((1, tn), lambda j: (0, j))],
        out_specs=pl.BlockSpec((rows, tn), lambda j: (0, j)),
        out_shape=jax.ShapeDtypeStruct((rows, n), F32),
        compiler_params=_params(("arbitrary",)),
        name="ada_mod",
    )(cvec, w_ada, b_ada.reshape(1, n))


def _row_tile(m, mod_rows, prefs=(512, 256, 128)):
    return _pick(math.gcd(m, mod_rows[1]), prefs)


def _mod_spec(d, which, mod_rows, tm):
    row0, per = mod_rows
    return pl.BlockSpec((None, None, 1, d), lambda i, j: (row0 + (i * tm) // per, which, 0, 0))


def _prenorm_kernel(x_ref, g_ref, sc_ref, sh_ref, wdt_ref, h_ref, dt_ref, stat_ref):
    _store_inv_rms(x_ref, stat_ref)

    def body(r, carry):
        rows = _rows(r, BF16_ROWS)
        inv = stat_ref[rows, 0:1]
        for cols in _col_chunks(x_ref.shape[1]):
            y = (x_ref[rows, cols] * inv) * g_ref[:, cols]
            h_ref[rows, cols] = (y * (1.0 + sc_ref[:, cols]) + sh_ref[:, cols]).astype(BF16)
        return carry

    lax.fori_loop(0, x_ref.shape[0] // BF16_ROWS, body, 0, unroll=2)
    dt_ref[...] = jnp.dot(h_ref[...], wdt_ref[...], preferred_element_type=F32)


def _prenorm(x, mod, g, w_dt, mod_rows):
    m, d = x.shape
    tm = _row_tile(m, mod_rows, (256, 128))
    return pl.pallas_call(
        _prenorm_kernel,
        grid=(m // tm, 1),
        in_specs=[pl.BlockSpec((tm, d), lambda i, j: (i, 0)),
                  pl.BlockSpec((1, d), lambda i, j: (0, 0)),
                  _mod_spec(d, 1, mod_rows, tm),
                  _mod_spec(d, 0, mod_rows, tm),
                  pl.BlockSpec((d, LANES), lambda i, j: (0, 0))],
        out_specs=[pl.BlockSpec((tm, d), lambda i, j: (i, 0)),
                   pl.BlockSpec((tm, LANES), lambda i, j: (i, 0))],
        out_shape=[jax.ShapeDtypeStruct((m, d), BF16), jax.ShapeDtypeStruct((m, LANES), F32)],
        scratch_shapes=[pltpu.VMEM((tm, LANES), F32)],
        compiler_params=_params(("arbitrary", "arbitrary")),
        name="mix_prenorm",
    )(x, g.reshape(1, d), mod, mod, w_dt)


def _in_proj_kernel(h_ref, w_ref, o_ref):
    o_ref[...] = jnp.dot(h_ref[...], w_ref[...], preferred_element_type=F32)


def _in_proj(h, w_main):
    m, d = h.shape
    n = w_main.shape[1]
    tm = _pick(m, (1024, 512, 256, 128))
    tn = _pick(n, (1024, 512, 256, 128))
    return pl.pallas_call(
        _in_proj_kernel,
        grid=(m // tm, n // tn),
        in_specs=[pl.BlockSpec((tm, d), lambda i, j: (i, 0)),
                  pl.BlockSpec((d, tn), lambda i, j: (0, j))],
        out_specs=pl.BlockSpec((tm, tn), lambda i, j: (i, j)),
        out_shape=jax.ShapeDtypeStruct((m, n), F32),
        compiler_params=_params(("arbitrary", "arbitrary")),
        name="in_proj",
    )(h, w_main)


def _rope_tile(x, cos, sin_signed, first_half):
    half = AXIS_DIM // 2
    swapped = jnp.where(first_half, pltpu.roll(x, LANES - half, axis=1), pltpu.roll(x, half, axis=1))
    return x * cos + swapped * sin_signed


def _rope_tables(l):
    rows = l // GRID_W
    row = jnp.repeat(jnp.arange(rows, dtype=F32), GRID_W)
    col = jnp.tile(jnp.arange(GRID_W, dtype=F32), rows)
    inv = 1.0 / (ROPE_BASE ** (jnp.arange(0, AXIS_DIM, 2, dtype=F32) / AXIS_DIM))

    def one(pos):
        ang = pos[:, None] * inv[None, :]
        c, s = jnp.cos(ang), jnp.sin(ang)
        return jnp.concatenate([c, c], axis=-1), jnp.concatenate([-s, s], axis=-1)

    cr, sr = one(row)
    cc, sc = one(col)
    return jnp.concatenate([cr, cc], axis=-1), jnp.concatenate([sr, sc], axis=-1)


def _attn_kernel(*refs, heads, ck, n_ctx):
    lq1_ref, lk1_ref, lq2_ref, lk2_ref, g_ref, q_ref, k_ref, v_ref = refs[:8]
    if n_ctx:
        ck_ref, cv_ref, cosq_ref, sinq_ref, cosk_ref, sink_ref, o_ref, kb_ref, vb_ref = refs[8:]
        lane = lax.broadcasted_iota(jnp.int32, (1, LANES), 1)
        first_half = (lane % AXIS_DIM) < (AXIS_DIM // 2)

        @pl.when(pl.program_id(2) == 0)
        def _():
            n_new = k_ref.shape[0]
            kb_ref[0:n_ctx, :] = ck_ref[...].astype(BF16)
            vb_ref[0:n_ctx, :] = cv_ref[...].astype(BF16)
            vb_ref[n_ctx:n_ctx + n_new, :] = v_ref[...].astype(BF16)
            step = _pick(n_new, (ROPE_ROWS,))
            for r0 in range(0, n_new, step):
                rows = slice(r0, r0 + step)
                for c0 in range(0, heads * V_HD, LANES):
                    cols = slice(c0, c0 + LANES)
                    roped = _rope_tile(k_ref[rows, cols], cosk_ref[rows, :], sink_ref[rows, :], first_half)
                    kb_ref[n_ctx + r0:n_ctx + r0 + step, cols] = roped.astype(BF16)

        k_src, v_src = kb_ref, vb_ref
    else:
        o_ref = refs[8]
        k_src, v_src = k_ref, v_ref

    lam = (jnp.exp(jnp.sum(lq1_ref[...] * lk1_ref[...], axis=-1, keepdims=True))
           - jnp.exp(jnp.sum(lq2_ref[...] * lk2_ref[...], axis=-1, keepdims=True)) + LAM_INIT)
    c2 = (QK_HD ** -0.5) * math.log2(math.e)
    nck = k_src.shape[0] // ck
    for h in range(heads):
        base = h * V_HD
        outs = []
        for half in range(2):
            cols = slice(base + half * QK_HD, base + (half + 1) * QK_HD)
            q = q_ref[:, cols]
            if n_ctx:
                q = _rope_tile(q, cosq_ref[...], sinq_ref[...], first_half)
            q = (q * c2).astype(BF16)
            m = l = acc = None
            for c in range(nck):
                rows = slice(c * ck, (c + 1) * ck)
                k = k_src[rows, cols].astype(BF16)
                s = lax.dot_general(q, k, (((1,), (1,)), ((), ())), preferred_element_type=F32)
                mc = jnp.max(s, axis=-1, keepdims=True)
                m_new = mc if c == 0 else jnp.maximum(m, mc)
                e = jnp.exp2(s - m_new)
                ls = jnp.sum(e, axis=-1, keepdims=True)
                pv = jnp.dot(e.astype(BF16), v_src[rows, base:base + V_HD].astype(BF16),
                             preferred_element_type=F32)
                if c == 0:
                    l, acc = ls, pv
                else:
                    alpha = jnp.exp2(m - m_new)
                    l = alpha * l + ls
                    acc = alpha * acc + pv
                m = m_new
            outs.append(acc * (1.0 / l))
        o = outs[0] - lam * outs[1]
        o_ref[:, base:base + V_HD] = ((_rms_rows(o) * g_ref[...]) * (1.0 - LAM_INIT)).astype(BF16)


def _attention(proj, w, lams, g_subln, heads_per_step, tq, ctx_k=None, ctx_v=None, tables=None):
    b, l, _ = proj.shape
    hw = heads_per_step * V_HD
    nh = w // hw
    n_ctx = 0 if ctx_k is None else ctx_k.shape[1]
    ck = _pick(n_ctx + l, (1536, 1024, 512, 256))
    small = pl.BlockSpec((1, QK_HD), lambda bi, hi, qi: (0, 0))
    in_specs = [small, small, small, small,
                pl.BlockSpec((1, V_HD), lambda bi, hi, qi: (0, 0)),
                pl.BlockSpec((None, tq, hw), lambda bi, hi, qi: (bi, qi, hi)),
                pl.BlockSpec((None, l, hw), lambda bi, hi, qi: (bi, 0, nh + hi)),
                pl.BlockSpec((None, l, hw), lambda bi, hi, qi: (bi, 0, 2 * nh + hi))]
    args = [*lams, g_subln.reshape(1, V_HD), proj, proj, proj]
    scratch = []
    if n_ctx:
        ctx_spec = pl.BlockSpec((None, n_ctx, hw), lambda bi, hi, qi: (bi, 0, hi))
        q_tab = pl.BlockSpec((tq, LANES), lambda bi, hi, qi: (qi, 0))
        k_tab = pl.BlockSpec((l, LANES), lambda bi, hi, qi: (0, 0))
        in_specs += [ctx_spec, ctx_spec, q_tab, q_tab, k_tab, k_tab]
        args += [ctx_k, ctx_v, tables[0], tables[1], tables[0], tables[1]]
        scratch = [pltpu.VMEM((n_ctx + l, hw), BF16), pltpu.VMEM((n_ctx + l, hw), BF16)]
    return pl.pallas_call(
        functools.partial(_attn_kernel, heads=heads_per_step, ck=ck, n_ctx=n_ctx),
        grid=(b, nh, l // tq),
        in_specs=in_specs,
        out_specs=pl.BlockSpec((None, tq, hw), lambda bi, hi, qi: (bi, qi, hi)),
        out_shape=jax.ShapeDtypeStruct((b, l, w), BF16),
        scratch_shapes=scratch,
        compiler_params=_params(("arbitrary", "arbitrary", "arbitrary")),
        name="diff_attn",
    )(*args)


def _conv_silu(win_ref, prev_ref, main_ref, next_ref, w_ref, b_ref, has_prev, has_next):
    n = main_ref.shape[0]
    win_ref[0:CONV_HALO, :] = jnp.where(has_prev, prev_ref[...], 0.0)
    win_ref[CONV_HALO:CONV_HALO + n, :] = main_ref[...]
    win_ref[CONV_HALO + n:CONV_HALO + n + CONV_HALO, :] = jnp.where(has_next, next_ref[...], 0.0)
    first = CONV_HALO - CONV_K // 2
    out = b_ref[...]
    for j in range(CONV_K):
        out = out + w_ref[j:j + 1, :] * win_ref[first + j:first + j + n, :]
    return _silu(out)


def _decay_terms(dt_raw, dt_bias, a_log, heads):
    dt = _softplus(dt_raw + dt_bias)
    a = dt * (-jnp.exp(a_log))
    row = lax.broadcasted_iota(jnp.int32, (CHUNK, LANES), 0)
    cs = a
    step = 1
    while step < CHUNK:
        cs = cs + jnp.where(row >= step, pltpu.roll(cs, step, axis=0), 0.0)
        step *= 2
    total = cs[CHUNK - 1:CHUNK, :]
    lane = lax.broadcasted_iota(jnp.int32, (CHUNK, LANES), 1)
    cum = jnp.where(lane < heads, cs, total - cs + a)
    cum_t = cum.T
    dt_t = dt.T
    tot_t = jnp.broadcast_to(total, (CHUNK, LANES)).T
    log2e = math.log2(math.e)
    return (cum * log2e,
            cum_t * log2e,
            dt_t,
            jnp.exp(tot_t - cum_t) * dt_t,
            jnp.exp(cum_t),
            jnp.exp(tot_t))


def _conv_kernel(xs_p, xs_m, xs_n, bc_p, bc_m, bc_n, cwx_ref, cbx_ref, cwbc_ref, cbbc_ref,
                 xo_ref, bco_ref, winx_ref, winbc_ref, *, nt):
    i = pl.program_id(1)
    has_prev = i > 0
    has_next = i < nt - 1
    xo_ref[...] = _conv_silu(winx_ref, xs_p, xs_m, xs_n, cwx_ref, cbx_ref, has_prev, has_next)
    bco_ref[...] = _conv_silu(winbc_ref, bc_p, bc_m, bc_n, cwbc_ref, cbbc_ref, has_prev, has_next).astype(BF16)


def _conv(proj, p, dx, nbc, xc, bcc):
    b, l, _ = proj.shape
    tr = _pick(l, (256, 128))
    nt = l // tr
    hpt = tr // CONV_HALO
    nhalo = l // CONV_HALO

    def main(wd, col):
        return pl.BlockSpec((None, tr, wd), lambda bi, i: (bi, i, col))

    def halo_prev(wd, col):
        return pl.BlockSpec((None, CONV_HALO, wd), lambda bi, i: (bi, jnp.maximum(i * hpt - 1, 0), col))

    def halo_next(wd, col):
        return pl.BlockSpec((None, CONV_HALO, wd),
                            lambda bi, i: (bi, jnp.minimum((i + 1) * hpt, nhalo - 1), col))

    def full(shape):
        return pl.BlockSpec(shape, lambda bi, i: (0,) * len(shape))

    return pl.pallas_call(
        functools.partial(_conv_kernel, nt=nt),
        grid=(b, nt),
        in_specs=[halo_prev(dx, xc), main(dx, xc), halo_next(dx, xc),
                  halo_prev(nbc, bcc), main(nbc, bcc), halo_next(nbc, bcc),
                  full((CONV_K, dx)), full((1, dx)), full((CONV_K, nbc)), full((1, nbc))],
        out_specs=[pl.BlockSpec((None, tr, dx), lambda bi, i: (bi, i, 0)),
                   pl.BlockSpec((None, tr, nbc), lambda bi, i: (bi, i, 0))],
        out_shape=[jax.ShapeDtypeStruct((b, l, dx), F32), jax.ShapeDtypeStruct((b, l, nbc), BF16)],
        scratch_shapes=[pltpu.VMEM((tr + 2 * CONV_HALO, dx), F32),
                        pltpu.VMEM((tr + 2 * CONV_HALO, nbc), F32)],
        compiler_params=_params(("arbitrary", "arbitrary")),
        name="conv_silu",
    )(proj, proj, proj, proj, proj, proj, p['conv_w_x'], p['conv_b_x'], p['conv_w_bc'], p['conv_b_bc'])


def _ssd_kernel(*refs, nc, heads, has_init):
    xs_ref, bc_ref, dt_ref, z_ref, alog_ref, dtb_ref, dskip_ref, gn_ref = refs[:8]
    idx = 8
    if has_init:
        initf_ref, initb_ref = refs[idx], refs[idx + 1]
        idx += 2
    y_ref, sf_ref, sb_ref, carf_ref, carb_ref, prev_ref = refs[idx:idx + 6]

    phase = pl.program_id(1)
    i = pl.program_id(2)
    c = jnp.where(phase == 0, i, nc - 1 - i)
    gw = heads // N_GROUPS * SSM_HD
    hpg = heads // N_GROUPS

    @pl.when((phase == 0) & (i == 0))
    def _():
        if has_init:
            carf_ref[...] = initf_ref[...]
            carb_ref[...] = initb_ref[...]
        else:
            carf_ref[...] = jnp.zeros_like(carf_ref)
            carb_ref[...] = jnp.zeros_like(carb_ref)

    row = lax.broadcasted_iota(jnp.int32, (CHUNK, CHUNK), 0)
    colm = lax.broadcasted_iota(jnp.int32, (CHUNK, CHUNK), 1)
    cum2, cum2_t, dt_t, w_t, e_t, dec_t = _decay_terms(dt_ref[...], dtb_ref[...], alog_ref[...], heads)

    xs = xs_ref[...]
    bc = bc_ref[...]
    x_t = xs.T

    def state_update(car_ref, off):
        for g in range(N_GROUPS):
            b_g = bc[:, g * D_STATE:(g + 1) * D_STATE].astype(BF16)
            rows = []
            for hh in range(hpg):
                h = g * hpg + hh
                rows.append(x_t[h * SSM_HD:(h + 1) * SSM_HD, :] * w_t[off + h:off + h + 1, :])
            xw = jnp.concatenate(rows, axis=0).astype(BF16)
            st = jnp.dot(xw, b_g, preferred_element_type=F32)
            for hh in range(hpg):
                h = g * hpg + hh
                r = slice(h * SSM_HD, (h + 1) * SSM_HD)
                car_ref[r, :] = car_ref[r, :] * dec_t[off + h:off + h + 1, :] + st[hh * SSM_HD:(hh + 1) * SSM_HD, :]

    @pl.when(phase == 0)
    def _():
        prev_ref[c] = carf_ref[...].astype(BF16)
        state_update(carf_ref, 0)

        @pl.when(i == nc - 1)
        def _():
            sf_ref[...] = carf_ref[...]

    @pl.when(phase == 1)
    def _():
        causal = colm >= row
        anti = colm <= row
        y_parts = []
        for g in range(N_GROUPS):
            b_g = bc[:, g * D_STATE:(g + 1) * D_STATE].astype(BF16)
            c_g = bc[:, (N_GROUPS + g) * D_STATE:(N_GROUPS + g + 1) * D_STATE].astype(BF16)
            cb_t = lax.dot_general(b_g, c_g, (((1,), (1,)), ((), ())), preferred_element_type=F32)
            rg = slice(g * gw, (g + 1) * gw)
            yo_f = lax.dot_general(prev_ref[c, rg, :], c_g, (((1,), (1,)), ((), ())),
                                   preferred_element_type=F32)
            yo_b = lax.dot_general(carb_ref[rg, :].astype(BF16), c_g, (((1,), (1,)), ((), ())),
                                   preferred_element_type=F32)
            for hh in range(hpg):
                h = g * hpg + hh
                hb = heads + h
                x_h = x_t[h * SSM_HD:(h + 1) * SSM_HD, :]
                seg_f = cum2_t[h:h + 1, :] - cum2[:, h:h + 1]
                seg_b = cum2_t[hb:hb + 1, :] - cum2[:, hb:hb + 1]
                w_f = cb_t * jnp.exp2(jnp.where(causal, seg_f, -1e4))
                w_b = cb_t * jnp.exp2(jnp.where(anti, seg_b, -1e4))
                lhs = jnp.concatenate([x_h * dt_t[h:h + 1, :], x_h * dt_t[hb:hb + 1, :]], axis=1)
                rhs = jnp.concatenate([w_f, w_b], axis=0)
                yd = jnp.dot(lhs.astype(BF16), rhs.astype(BF16), preferred_element_type=F32)
                r = slice(hh * SSM_HD, (hh + 1) * SSM_HD)
                y_parts.append(yd + yo_f[r, :] * e_t[h:h + 1, :] + yo_b[r, :] * e_t[hb:hb + 1, :])
        y = jnp.concatenate(y_parts, axis=0).T
        y = y + dskip_ref[...] * xs
        y = y * _silu(z_ref[...])
        for g in range(N_GROUPS):
            cols = slice(g * gw, (g + 1) * gw)
            y_ref[:, cols] = (_rms_rows(y[:, cols]) * gn_ref[:, cols]).astype(BF16)
        state_update(carb_ref, heads)

        @pl.when(i == nc - 1)
        def _():
            sb_ref[...] = carb_ref[...]


def _ssd(xs, bc, dt, proj, zc, p, init_f, init_b):
    b, l, dx = xs.shape
    nbc = bc.shape[2]
    heads = dx // SSM_HD
    nc = l // CHUNK
    has_init = init_f is not None

    def chunk_of(ph, i):
        return jnp.where(ph == 0, i, nc - 1 - i)

    def main(wd, col):
        return pl.BlockSpec((None, CHUNK, wd), lambda bi, ph, i: (bi, chunk_of(ph, i), col))

    def full(shape):
        return pl.BlockSpec(shape, lambda bi, ph, i: (0,) * len(shape))

    state_spec = pl.BlockSpec((None, dx, D_STATE), lambda bi, ph, i: (bi, 0, 0))
    in_specs = [main(dx, 0), main(nbc, 0), main(LANES, 0), main(dx, zc),
                full((1, LANES)), full((1, LANES)), full((1, dx)), full((1, dx))]
    args = [xs, bc, dt, proj, p['a_log'], p['dt_bias'], p['d_skip'], p['g_ssm_norm']]
    if has_init:
        in_specs += [state_spec, state_spec]
        args += [init_f, init_b]
    y_spec = pl.BlockSpec((None, CHUNK, dx), lambda bi, ph, i: (bi, jnp.where(ph == 0, nc - 1, nc - 1 - i), 0))
    return pl.pallas_call(
        functools.partial(_ssd_kernel, nc=nc, heads=heads, has_init=has_init),
        grid=(b, 2, nc),
        in_specs=in_specs,
        out_specs=[y_spec, state_spec, state_spec],
        out_shape=[jax.ShapeDtypeStruct((b, l, dx), BF16),
                   jax.ShapeDtypeStruct((b, dx, D_STATE), F32),
                   jax.ShapeDtypeStruct((b, dx, D_STATE), F32)],
        scratch_shapes=[pltpu.VMEM((dx, D_STATE), F32), pltpu.VMEM((dx, D_STATE), F32),
                        pltpu.VMEM((nc, dx, D_STATE), BF16)],
        compiler_params=_params(("arbitrary", "arbitrary", "arbitrary")),
        name="ssd_scan",
    )(*args)


def _out_proj_kernel(a_ref, y_ref, wa_ref, wy_ref, o_ref):
    o_ref[...] = (jnp.dot(a_ref[...], wa_ref[...], preferred_element_type=F32)
                  + jnp.dot(y_ref[...], wy_ref[...], preferred_element_type=F32))


def _out_proj(att, y, w_out):
    m, wa = att.shape
    d = w_out.shape[1]
    assert y.shape[1] == wa and w_out.shape[0] == 2 * wa
    tm = _pick(m, (1024, 512, 256, 128))
    tn = _pick(d, (1024, 512, 256, 128))
    return pl.pallas_call(
        _out_proj_kernel,
        grid=(m // tm, d // tn),
        in_specs=[pl.BlockSpec((tm, wa), lambda i, j: (i, 0)),
                  pl.BlockSpec((tm, wa), lambda i, j: (i, 0)),
                  pl.BlockSpec((wa, tn), lambda i, j: (0, j)),
                  pl.BlockSpec((wa, tn), lambda i, j: (1, j))],
        out_specs=pl.BlockSpec((tm, tn), lambda i, j: (i, j)),
        out_shape=jax.ShapeDtypeStruct((m, d), F32),
        compiler_params=_params(("arbitrary", "arbitrary")),
        name="out_proj",
    )(att, y, w_out, w_out)


def _sandwich_kernel(mix_ref, x_ref, gpost_ref, gate_ref, gpre_ref, sc_ref, sh_ref,
                     x1_ref, h2_ref, stat_ref, stat2_ref):
    chunks = _col_chunks(x1_ref.shape[1])
    _store_inv_rms(mix_ref, stat_ref)

    def residual(r, carry):
        rows = _rows(r, BF16_ROWS)
        inv = stat_ref[rows, 0:1]
        for cols in chunks:
            n = (mix_ref[rows, cols] * inv) * gpost_ref[:, cols]
            x1_ref[rows, cols] = x_ref[rows, cols] + gate_ref[:, cols] * n
        return carry

    lax.fori_loop(0, x1_ref.shape[0] // BF16_ROWS, residual, 0, unroll=2)
    _store_inv_rms(x1_ref, stat2_ref)

    def prenorm(r, carry):
        rows = _rows(r, BF16_ROWS)
        inv = stat2_ref[rows, 0:1]
        for cols in chunks:
            y = (x1_ref[rows, cols] * inv) * gpre_ref[:, cols]
            h2_ref[rows, cols] = (y * (1.0 + sc_ref[:, cols]) + sh_ref[:, cols]).astype(BF16)
        return carry

    lax.fori_loop(0, x1_ref.shape[0] // BF16_ROWS, prenorm, 0, unroll=2)


def _sandwich(mix, x, mod, g_post, g_pre, mod_rows):
    m, d = x.shape
    tm = _row_tile(m, mod_rows, (256, 128))
    vec = pl.BlockSpec((1, d), lambda i, j: (0, 0))
    tile = pl.BlockSpec((tm, d), lambda i, j: (i, 0))
    return pl.pallas_call(
        _sandwich_kernel,
        grid=(m // tm, 1),
        in_specs=[tile, tile, vec, _mod_spec(d, 2, mod_rows, tm), vec,
                  _mod_spec(d, 4, mod_rows, tm), _mod_spec(d, 3, mod_rows, tm)],
        out_specs=[tile, tile],
        out_shape=[jax.ShapeDtypeStruct((m, d), F32), jax.ShapeDtypeStruct((m, d), BF16)],
        scratch_shapes=[pltpu.VMEM((tm, LANES), F32), pltpu.VMEM((tm, LANES), F32)],
        compiler_params=_params(("arbitrary", "arbitrary")),
        name="mix_sandwich",
    )(mix, x, g_post.reshape(1, d), mod, g_pre.reshape(1, d), mod, mod)


def _up_kernel(h_ref, w_ref, o_ref):
    u = jnp.maximum(jnp.dot(h_ref[...], w_ref[...], preferred_element_type=F32), 0.0)
    o_ref[...] = (u * u).astype(BF16)


def _up(h2, w_up):
    m, d = h2.shape
    f = w_up.shape[1]
    tm = _pick(m, (1024, 512, 256, 128))
    tn = _pick(f, (1024, 512, 256, 128))
    return pl.pallas_call(
        _up_kernel,
        grid=(m // tm, f // tn),
        in_specs=[pl.BlockSpec((tm, d), lambda i, j: (i, 0)),
                  pl.BlockSpec((d, tn), lambda i, j: (0, j))],
        out_specs=pl.BlockSpec((tm, tn), lambda i, j: (i, j)),
        out_shape=jax.ShapeDtypeStruct((m, f), BF16),
        compiler_params=_params(("arbitrary", "arbitrary")),
        name="mlp_up",
    )(h2, w_up)


def _down_kernel(u_ref, w_ref, x_ref, gpost_ref, gate_ref, o_ref, stat_ref, *, nk):
    k = pl.program_id(1)

    @pl.when(k == 0)
    def _():
        o_ref[...] = jnp.dot(u_ref[...], w_ref[...], preferred_element_type=F32)

    @pl.when(k > 0)
    def _():
        o_ref[...] += jnp.dot(u_ref[...], w_ref[...], preferred_element_type=F32)

    @pl.when(k == nk - 1)
    def _():
        _store_inv_rms(o_ref, stat_ref)

        def body(r, carry):
            rows = _rows(r, BF16_ROWS)
            inv = stat_ref[rows, 0:1]
            for cols in _col_chunks(o_ref.shape[1]):
                n = (o_ref[rows, cols] * inv) * gpost_ref[:, cols]
                o_ref[rows, cols] = x_ref[rows, cols] + gate_ref[:, cols] * n
            return carry

        lax.fori_loop(0, o_ref.shape[0] // BF16_ROWS, body, 0, unroll=2)


def _down(u, w_down, x1, mod, g_post, mod_rows):
    m, f = u.shape
    d = x1.shape[1]
    tm = _row_tile(m, mod_rows)
    tk = _pick(f, (1024, 512, 256, 128))
    nk = f // tk
    return pl.pallas_call(
        functools.partial(_down_kernel, nk=nk),
        grid=(m // tm, nk),
        in_specs=[pl.BlockSpec((tm, tk), lambda i, k: (i, k)),
                  pl.BlockSpec((tk, d), lambda i, k: (k, 0)),
                  pl.BlockSpec((tm, d), lambda i, k: (i, 0)),
                  pl.BlockSpec((1, d), lambda i, k: (0, 0)),
                  _mod_spec(d, 5, mod_rows, tm)],
        out_specs=pl.BlockSpec((tm, d), lambda i, k: (i, 0)),
        out_shape=jax.ShapeDtypeStruct((m, d), F32),
        scratch_shapes=[pltpu.VMEM((tm, LANES), F32)],
        compiler_params=_params(("arbitrary", "arbitrary")),
        name="mlp_down",
    )(u, w_down, x1, g_post.reshape(1, d), mod)


def _kv_out_kernel(ki_ref, vi_ref, k_ref, v_ref, *, n_att):
    for hh in range(n_att):
        for j in range(2):
            c0 = (hh * 2 + j) * QK_HD
            k_ref[:, hh, j, :] = ki_ref[:, c0:c0 + QK_HD]
        v_ref[:, hh, :] = vi_ref[:, hh * V_HD:(hh + 1) * V_HD]


def _kv_out(proj3, att_w):
    b, l, _ = proj3.shape
    n_att = att_w // V_HD
    tl = _pick(l, (256, 128))
    return pl.pallas_call(
        functools.partial(_kv_out_kernel, n_att=n_att),
        grid=(b, l // tl),
        in_specs=[pl.BlockSpec((None, tl, att_w), lambda bi, i: (bi, i, 1)),
                  pl.BlockSpec((None, tl, att_w), lambda bi, i: (bi, i, 2))],
        out_specs=[pl.BlockSpec((None, None, tl, n_att, 2, QK_HD), lambda bi, i: (bi, 0, i, 0, 0, 0)),
                   pl.BlockSpec((None, None, tl, n_att, V_HD), lambda bi, i: (bi, 0, i, 0, 0))],
        out_shape=[jax.ShapeDtypeStruct((b, 1, l, n_att, 2, QK_HD), F32),
                   jax.ShapeDtypeStruct((b, 1, l, n_att, V_HD), F32)],
        compiler_params=_params(("arbitrary", "arbitrary")),
        name="kv_out",
    )(proj3, proj3)


def _trunk(x, mod, mod_row0, rows_per_mod, w, ctx_k, ctx_v, init_f, init_b):
    b, l, d = x.shape
    m = b * l
    att_w = d // 2
    d_inner = d - att_w
    n_bc = 2 * N_GROUPS * D_STATE
    latent = ctx_k is not None

    mod_rows = (mod_row0, rows_per_mod)
    x2d = x.reshape(m, d)
    assert att_w == d_inner and (3 * att_w + 2 * d_inner) % n_bc == 0
    h, dt = _prenorm(x2d, mod, w['g_mix_pre'], w['w_in_dt'], mod_rows)
    proj = _in_proj(h, w['w_in_main'])
    proj3 = proj.reshape(b, l, -1)
    lams = (w['lq1'], w['lk1'], w['lq2'], w['lk2'])
    if latent:
        att = _attention(proj3, att_w, lams, w['g_subln'], 1, _pick(l, (512, 256, 128)),
                         ctx_k, ctx_v, _rope_tables(l))
    else:
        att = _attention(proj3, att_w, lams, w['g_subln'], att_w // V_HD, l)
    xs, bc = _conv(proj3, w, d_inner, n_bc, 4, (3 * att_w + 2 * d_inner) // n_bc)
    y, fin_f, fin_b = _ssd(xs, bc, dt.reshape(b, l, LANES), proj3, 3, w, init_f, init_b)
    mix = _out_proj(att.reshape(m, att_w), y.reshape(m, d_inner), w['w_out'])
    x1, h2 = _sandwich(mix, x2d, mod, w['g_mix_post'], w['g_mlp_pre'], mod_rows)
    u = _up(h2, w['w_up'])
    x2 = _down(u, w['w_down'], x1, mod, w['g_mlp_post'], mod_rows)
    k_c, v_c = (None, None) if latent else _kv_out(proj3, att_w)
    return x2.reshape(b, l, d), k_c, v_c, fin_f, fin_b


def kernel(x_prompt, x_sample, c, cache_k, cache_v, state_ssm_fwd, state_ssm_bwd, c_ctx, w_ada, b_ada,
           g_mix_pre, g_mix_post, g_mlp_pre, g_mlp_post, w_in, lambda_q1, lambda_k1, lambda_q2, lambda_k2,
           g_subln, conv_w, conv_b, a_log, dt_bias, d_skip, g_ssm_norm, w_out, w_up, w_down):
    bp, lp, d = x_prompt.shape
    bd, ld, _ = x_sample.shape
    depth = w_in.shape[0]
    assert depth == 1 and bd + 1 <= MOD_ROWS
    att_w = d // 2
    d_inner = d - att_w
    heads = d_inner // SSM_HD
    n_att = att_w // V_HD
    n_main = 3 * att_w + 2 * d_inner + 2 * N_GROUPS * D_STATE
    assert 2 * heads <= LANES and w_in.shape[2] == n_main + 2 * heads

    lyr = 0
    pad_l = LANES - 2 * heads
    w = dict(
        g_mix_pre=g_mix_pre[lyr], g_mix_post=g_mix_post[lyr], g_mlp_pre=g_mlp_pre[lyr],
        g_mlp_post=g_mlp_post[lyr],
        w_in_main=w_in[lyr, :, :n_main].astype(BF16),
        w_in_dt=jnp.pad(w_in[lyr, :, n_main:], ((0, 0), (0, pad_l))).astype(BF16),
        lq1=lambda_q1[lyr].reshape(1, QK_HD), lk1=lambda_k1[lyr].reshape(1, QK_HD),
        lq2=lambda_q2[lyr].reshape(1, QK_HD), lk2=lambda_k2[lyr].reshape(1, QK_HD),
        g_subln=g_subln[lyr],
        conv_w_x=conv_w[lyr, :, :d_inner], conv_w_bc=conv_w[lyr, :, d_inner:],
        conv_b_x=conv_b[lyr, :d_inner].reshape(1, -1), conv_b_bc=conv_b[lyr, d_inner:].reshape(1, -1),
        a_log=jnp.pad(a_log[lyr].reshape(1, -1), ((0, 0), (0, pad_l))),
        dt_bias=jnp.pad(dt_bias[lyr].reshape(1, -1), ((0, 0), (0, pad_l))),
        d_skip=jnp.repeat(d_skip[lyr], SSM_HD).reshape(1, -1),
        g_ssm_norm=g_ssm_norm[lyr].reshape(1, -1),
        w_out=w_out[lyr].astype(BF16), w_up=w_up[lyr].astype(BF16), w_down=w_down[lyr].astype(BF16),
    )

    cvec = jnp.concatenate([c_ctx[None, :], c, jnp.zeros((MOD_ROWS - 1 - bd, d), F32)], axis=0)
    mod = _ada(cvec, w_ada[lyr], b_ada[lyr]).reshape(MOD_ROWS, 6, 1, d)

    yp, k_c, v_c, sf, sb = _trunk(x_prompt, mod, 0, bp * lp, w, None, None, None, None)
    ys, _, _, _, _ = _trunk(
        x_sample, mod, 1, ld, w,
        cache_k[:, lyr].reshape(bd, -1, att_w), cache_v[:, lyr].reshape(bd, -1, att_w),
        state_ssm_fwd[:, lyr].reshape(bd, d_inner, D_STATE), state_ssm_bwd[:, lyr].reshape(bd, d_inner, D_STATE))

    return (yp, ys, k_c, v_c,
            sf.reshape(bp, 1, heads, SSM_HD, D_STATE), sb.reshape(bp, 1, heads, SSM_HD, D_STATE))
```

```python
import functools
import math

import jax
import jax.numpy as jnp
from jax import lax
from jax.experimental import pallas as pl
from jax.experimental.pallas import tpu as pltpu

F32 = jnp.float32
BF16 = jnp.bfloat16

GRID_W = 64
V_HD = 256
QK_HD = V_HD // 2
AXIS_DIM = QK_HD // 2
ROPE_BASE = 10000.0
SSM_HD = 64
N_GROUPS = 4
D_STATE = 128
CONV_K = 5
CHUNK = 128
RMS_EPS = 1e-6
LAM_INIT = 0.8 - 0.6 * math.exp(-0.3 * 0)

V7X_VMEM_BYTES = 64 * 1024 * 1024
VMEM_LIMIT = V7X_VMEM_BYTES - 8 * 1024 * 1024
LANES = 128
SUBLANES = 8
CONV_HALO = SUBLANES
NORM_ROWS = SUBLANES
BF16_ROWS = 2 * SUBLANES
STAT_UNROLL = 16
APPLY_COLS = 1024
ROPE_ROWS = 512
MOD_ROWS = 8


def _pick(n, prefs):
    for p in prefs:
        if n % p == 0:
            return p
    return n


def _params(sem):
    return pltpu.CompilerParams(dimension_semantics=sem, vmem_limit_bytes=VMEM_LIMIT)


def _sigmoid(x):
    return 1.0 / (1.0 + jnp.exp(-x))


def _silu(x):
    return x * _sigmoid(x)


def _softplus(x):
    return jnp.maximum(x, 0.0) + jnp.log1p(jnp.exp(-jnp.abs(x)))


def _rms_rows(x, eps=RMS_EPS):
    return x * lax.rsqrt(jnp.mean(x * x, axis=-1, keepdims=True) + eps)


def _inv_rms(x, eps=RMS_EPS):
    return jnp.broadcast_to(lax.rsqrt(jnp.mean(x * x, axis=-1, keepdims=True) + eps), (x.shape[0], LANES))


def _rows(r, n):
    return pl.ds(pl.multiple_of(r * n, n), n)


def _store_inv_rms(src_ref, stat_ref):
    def body(r, carry):
        rows = _rows(r, NORM_ROWS)
        stat_ref[rows, :] = _inv_rms(src_ref[rows, :])
        return carry

    lax.fori_loop(0, src_ref.shape[0] // NORM_ROWS, body, 0, unroll=STAT_UNROLL)


def _col_chunks(d):
    w = _pick(d, (APPLY_COLS,))
    return [slice(c * w, (c + 1) * w) for c in range(d // w)]


def _ada_kernel(c_ref, w_ref, b_ref, o_ref):
    c = c_ref[...]
    s = _silu(c).astype(BF16)
    o_ref[...] = jnp.dot(s, w_ref[...].astype(BF16), preferred_element_type=F32) + b_ref[...]


def _ada(cvec, w_ada, b_ada):
    rows, d = cvec.shape
    n = w_ada.shape[1]
    tn = _pick(n, (512, 256, 128))
    return pl.pallas_call(
        _ada_kernel,
        grid=(n // tn,),
        in_specs=[pl.BlockSpec((rows, d), lambda j: (0, 0)),
                  pl.BlockSpec((d, tn), lambda j: (0, j)),
                  pl.BlockSpec((1, tn), lambda j: (0, j))],
        out_specs=pl.BlockSpec((rows, tn), lambda j: (0, j)),
        out_shape=jax.ShapeDtypeStruct((rows, n), F32),
        compiler_params=_params(("arbitrary",)),
        name="ada_mod",
    )(cvec, w_ada, b_ada.reshape(1, n))


def _row_tile(m, mod_rows, prefs=(512, 256, 128)):
    return _pick(math.gcd(m, mod_rows[1]), prefs)


def _mod_spec(d, which, mod_rows, tm):
    row0, per = mod_rows
    return pl.BlockSpec((None, None, 1, d), lambda i, j: (row0 + (i * tm) // per, which, 0, 0))


def _prenorm_kernel(x_ref, g_ref, sc_ref, sh_ref, wdt_ref, h_ref, dt_ref, stat_ref):
    _store_inv_rms(x_ref, stat_ref)

    def body(r, carry):
        rows = _rows(r, BF16_ROWS)
        inv = stat_ref[rows, 0:1]
        for cols in _col_chunks(x_ref.shape[1]):
            y = (x_ref[rows, cols] * inv) * g_ref[:, cols]
            h_ref[rows, cols] = (y * (1.0 + sc_ref[:, cols]) + sh_ref[:, cols]).astype(BF16)
        return carry

    lax.fori_loop(0, x_ref.shape[0] // BF16_ROWS, body, 0, unroll=2)
    dt_ref[...] = jnp.dot(h_ref[...], wdt_ref[...], preferred_element_type=F32)


def _prenorm(x, mod, g, w_dt, mod_rows):
    m, d = x.shape
    tm = _row_tile(m, mod_rows, (256, 128))
    return pl.pallas_call(
        _prenorm_kernel,
        grid=(m // tm, 1),
        in_specs=[pl.BlockSpec((tm, d), lambda i, j: (i, 0)),
                  pl.BlockSpec((1, d), lambda i, j: (0, 0)),
                  _mod_spec(d, 1, mod_rows, tm),
                  _mod_spec(d, 0, mod_rows, tm),
                  pl.BlockSpec((d, LANES), lambda i, j: (0, 0))],
        out_specs=[pl.BlockSpec((tm, d), lambda i, j: (i, 0)),
                   pl.BlockSpec((tm, LANES), lambda i, j: (i, 0))],
        out_shape=[jax.ShapeDtypeStruct((m, d), BF16), jax.ShapeDtypeStruct((m, LANES), F32)],
        scratch_shapes=[pltpu.VMEM((tm, LANES), F32)],
        compiler_params=_params(("arbitrary", "arbitrary")),
        name="mix_prenorm",
    )(x, g.reshape(1, d), mod, mod, w_dt)


def _in_proj_kernel(h_ref, w_ref, o_ref):
    o_ref[...] = jnp.dot(h_ref[...], w_ref[...], preferred_element_type=F32)


def _in_proj(h, w_in, n):
    m, d = h.shape
    tm = _pick(m, (1024, 512, 256, 128))
    tn = _pick(n, (1024, 512, 256, 128))
    return pl.pallas_call(
        _in_proj_kernel,
        grid=(m // tm, n // tn),
        in_specs=[pl.BlockSpec((tm, d), lambda i, j: (i, 0)),
                  pl.BlockSpec((d, tn), lambda i, j: (0, j))],
        out_specs=pl.BlockSpec((tm, tn), lambda i, j: (i, j)),
        out_shape=jax.ShapeDtypeStruct((m, n), F32),
        compiler_params=_params(("arbitrary", "arbitrary")),
        name="in_proj",
    )(h, w_in)


def _rope_tile(x, cos, sin_signed, first_half):
    half = AXIS_DIM // 2
    swapped = jnp.where(first_half, pltpu.roll(x, LANES - half, axis=1), pltpu.roll(x, half, axis=1))
    return x * cos + swapped * sin_signed


def _rope_tables(l):
    rows = l // GRID_W
    row = jnp.repeat(jnp.arange(rows, dtype=F32), GRID_W)
    col = jnp.tile(jnp.arange(GRID_W, dtype=F32), rows)
    inv = 1.0 / (ROPE_BASE ** (jnp.arange(0, AXIS_DIM, 2, dtype=F32) / AXIS_DIM))

    def one(pos):
        ang = pos[:, None] * inv[None, :]
        c, s = jnp.cos(ang), jnp.sin(ang)
        return jnp.concatenate([c, c], axis=-1), jnp.concatenate([-s, s], axis=-1)

    cr, sr = one(row)
    cc, sc = one(col)
    return jnp.concatenate([cr, cc], axis=-1), jnp.concatenate([sr, sc], axis=-1)


def _attn_kernel(*refs, heads, ck, n_ctx):
    lq1_ref, lk1_ref, lq2_ref, lk2_ref, g_ref, q_ref, k_ref, v_ref = refs[:8]
    if n_ctx:
        ck_ref, cv_ref, cosq_ref, sinq_ref, cosk_ref, sink_ref, o_ref, kb_ref, vb_ref = refs[8:]
        lane = lax.broadcasted_iota(jnp.int32, (1, LANES), 1)
        first_half = (lane % AXIS_DIM) < (AXIS_DIM // 2)

        @pl.when(pl.program_id(2) == 0)
        def _():
            n_new = k_ref.shape[0]
            kb_ref[0:n_ctx, :] = ck_ref[...].astype(BF16)
            vb_ref[0:n_ctx, :] = cv_ref[...].astype(BF16)
            vb_ref[n_ctx:n_ctx + n_new, :] = v_ref[...].astype(BF16)
            step = _pick(n_new, (ROPE_ROWS,))
            for r0 in range(0, n_new, step):
                rows = slice(r0, r0 + step)
                for c0 in range(0, heads * V_HD, LANES):
                    cols = slice(c0, c0 + LANES)
                    roped = _rope_tile(k_ref[rows, cols], cosk_ref[rows, :], sink_ref[rows, :], first_half)
                    kb_ref[n_ctx + r0:n_ctx + r0 + step, cols] = roped.astype(BF16)

        k_src, v_src = kb_ref, vb_ref
    else:
        o_ref = refs[8]
        k_src, v_src = k_ref, v_ref

    lam = (jnp.exp(jnp.sum(lq1_ref[...] * lk1_ref[...], axis=-1, keepdims=True))
           - jnp.exp(jnp.sum(lq2_ref[...] * lk2_ref[...], axis=-1, keepdims=True)) + LAM_INIT)
    c2 = (QK_HD ** -0.5) * math.log2(math.e)
    nck = k_src.shape[0] // ck
    for h in range(heads):
        base = h * V_HD
        outs = []
        for half in range(2):
            cols = slice(base + half * QK_HD, base + (half + 1) * QK_HD)
            q = q_ref[:, cols]
            if n_ctx:
                q = _rope_tile(q, cosq_ref[...], sinq_ref[...], first_half)
            q = (q * c2).astype(BF16)
            m = l = acc = None
            for c in range(nck):
                rows = slice(c * ck, (c + 1) * ck)
                k = k_src[rows, cols].astype(BF16)
                s = lax.dot_general(q, k, (((1,), (1,)), ((), ())), preferred_element_type=F32)
                mc = jnp.max(s, axis=-1, keepdims=True)
                m_new = mc if c == 0 else jnp.maximum(m, mc)
                e = jnp.exp2(s - m_new)
                ls = jnp.sum(e, axis=-1, keepdims=True)
                pv = jnp.dot(e.astype(BF16), v_src[rows, base:base + V_HD].astype(BF16),
                             preferred_element_type=F32)
                if c == 0:
                    l, acc = ls, pv
                else:
                    alpha = jnp.exp2(m - m_new)
                    l = alpha * l + ls
                    acc = alpha * acc + pv
                m = m_new
            outs.append(acc * (1.0 / l))
        o = outs[0] - lam * outs[1]
        o_ref[:, base:base + V_HD] = ((_rms_rows(o) * g_ref[...]) * (1.0 - LAM_INIT)).astype(BF16)


def _attention(proj, w, lams, g_subln, heads_per_step, tq, ctx_k=None, ctx_v=None, tables=None):
    b, l, _ = proj.shape
    hw = heads_per_step * V_HD
    nh = w // hw
    n_ctx = 0 if ctx_k is None else ctx_k.shape[1]
    ck = _pick(n_ctx + l, (768, 1536, 1024, 512, 256))
    small = pl.BlockSpec((1, QK_HD), lambda bi, hi, qi: (0, 0))
    in_specs = [small, small, small, small,
                pl.BlockSpec((1, V_HD), lambda bi, hi, qi: (0, 0)),
                pl.BlockSpec((None, tq, hw), lambda bi, hi, qi: (bi, qi, hi)),
                pl.BlockSpec((None, l, hw), lambda bi, hi, qi: (bi, 0, nh + hi)),
                pl.BlockSpec((None, l, hw), lambda bi, hi, qi: (bi, 0, 2 * nh + hi))]
    args = [*lams, g_subln.reshape(1, V_HD), proj, proj, proj]
    scratch = []
    if n_ctx:
        ctx_spec = pl.BlockSpec((None, n_ctx, hw), lambda bi, hi, qi: (bi, 0, hi))
        q_tab = pl.BlockSpec((tq, LANES), lambda bi, hi, qi: (qi, 0))
        k_tab = pl.BlockSpec((l, LANES), lambda bi, hi, qi: (0, 0))
        in_specs += [ctx_spec, ctx_spec, q_tab, q_tab, k_tab, k_tab]
        args += [ctx_k, ctx_v, tables[0], tables[1], tables[0], tables[1]]
        scratch = [pltpu.VMEM((n_ctx + l, hw), BF16), pltpu.VMEM((n_ctx + l, hw), BF16)]
    return pl.pallas_call(
        functools.partial(_attn_kernel, heads=heads_per_step, ck=ck, n_ctx=n_ctx),
        grid=(b, nh, l // tq),
        in_specs=in_specs,
        out_specs=pl.BlockSpec((None, tq, hw), lambda bi, hi, qi: (bi, qi, hi)),
        out_shape=jax.ShapeDtypeStruct((b, l, w), BF16),
        scratch_shapes=scratch,
        compiler_params=_params(("arbitrary", "arbitrary", "arbitrary")),
        name="diff_attn",
    )(*args)


def _conv_silu(win_ref, prev_ref, main_ref, next_ref, w_ref, b_ref, has_prev, has_next):
    n = main_ref.shape[0]
    win_ref[0:CONV_HALO, :] = jnp.where(has_prev, prev_ref[...], 0.0)
    win_ref[CONV_HALO:CONV_HALO + n, :] = main_ref[...]
    win_ref[CONV_HALO + n:CONV_HALO + n + CONV_HALO, :] = jnp.where(has_next, next_ref[...], 0.0)
    first = CONV_HALO - CONV_K // 2
    out = b_ref[...]
    for j in range(CONV_K):
        out = out + w_ref[j:j + 1, :] * win_ref[first + j:first + j + n, :]
    return _silu(out)


def _decay_terms(dt_raw, dt_bias, a_log, heads):
    dt = _softplus(dt_raw + dt_bias)
    a = dt * (-jnp.exp(a_log))
    row = lax.broadcasted_iota(jnp.int32, (CHUNK, LANES), 0)
    cs = a
    step = 1
    while step < CHUNK:
        cs = cs + jnp.where(row >= step, pltpu.roll(cs, step, axis=0), 0.0)
        step *= 2
    total = cs[CHUNK - 1:CHUNK, :]
    lane = lax.broadcasted_iota(jnp.int32, (CHUNK, LANES), 1)
    cum = jnp.where(lane < heads, cs, total - cs + a)
    cum_t = cum.T
    dt_t = dt.T
    tot_t = jnp.broadcast_to(total, (CHUNK, LANES)).T
    log2e = math.log2(math.e)
    return (cum * log2e,
            cum_t * log2e,
            dt_t,
            jnp.exp(tot_t - cum_t) * dt_t,
            jnp.exp(cum_t),
            jnp.exp(tot_t))


def _conv_kernel(xs_p, xs_m, xs_n, bc_p, bc_m, bc_n, cwx_ref, cbx_ref, cwbc_ref, cbbc_ref,
                 xo_ref, bco_ref, winx_ref, winbc_ref, *, nt):
    i = pl.program_id(1)
    has_prev = i > 0
    has_next = i < nt - 1
    xo_ref[...] = _conv_silu(winx_ref, xs_p, xs_m, xs_n, cwx_ref, cbx_ref, has_prev, has_next)
    bco_ref[...] = _conv_silu(winbc_ref, bc_p, bc_m, bc_n, cwbc_ref, cbbc_ref, has_prev, has_next).astype(BF16)


def _conv(proj, p, dx, nbc, xc, bcc):
    b, l, _ = proj.shape
    tr = _pick(l, (256, 128))
    nt = l // tr
    hpt = tr // CONV_HALO
    nhalo = l // CONV_HALO

    def main(wd, col):
        return pl.BlockSpec((None, tr, wd), lambda bi, i: (bi, i, col))

    def halo_prev(wd, col):
        return pl.BlockSpec((None, CONV_HALO, wd), lambda bi, i: (bi, jnp.maximum(i * hpt - 1, 0), col))

    def halo_next(wd, col):
        return pl.BlockSpec((None, CONV_HALO, wd),
                            lambda bi, i: (bi, jnp.minimum((i + 1) * hpt, nhalo - 1), col))

    def full(shape):
        return pl.BlockSpec(shape, lambda bi, i: (0,) * len(shape))

    return pl.pallas_call(
        functools.partial(_conv_kernel, nt=nt),
        grid=(b, nt),
        in_specs=[halo_prev(dx, xc), main(dx, xc), halo_next(dx, xc),
                  halo_prev(nbc, bcc), main(nbc, bcc), halo_next(nbc, bcc),
                  full((CONV_K, dx)), full((1, dx)), full((CONV_K, nbc)), full((1, nbc))],
        out_specs=[pl.BlockSpec((None, tr, dx), lambda bi, i: (bi, i, 0)),
                   pl.BlockSpec((None, tr, nbc), lambda bi, i: (bi, i, 0))],
        out_shape=[jax.ShapeDtypeStruct((b, l, dx), F32), jax.ShapeDtypeStruct((b, l, nbc), BF16)],
        scratch_shapes=[pltpu.VMEM((tr + 2 * CONV_HALO, dx), F32),
                        pltpu.VMEM((tr + 2 * CONV_HALO, nbc), F32)],
        compiler_params=_params(("arbitrary", "arbitrary")),
        name="conv_silu",
    )(proj, proj, proj, proj, proj, proj, p['conv_w_x'], p['conv_b_x'], p['conv_w_bc'], p['conv_b_bc'])


def _ssd_kernel(*refs, nc, heads, has_init):
    xs_ref, bc_ref, dt_ref, z_ref, alog_ref, dtb_ref, dskip_ref, gn_ref = refs[:8]
    idx = 8
    if has_init:
        initf_ref, initb_ref = refs[idx], refs[idx + 1]
        idx += 2
    y_ref, sf_ref, sb_ref, carf_ref, carb_ref, prev_ref = refs[idx:idx + 6]

    phase = pl.program_id(1)
    i = pl.program_id(2)
    c = jnp.where(phase == 0, i, nc - 1 - i)
    gw = heads // N_GROUPS * SSM_HD
    hpg = heads // N_GROUPS

    @pl.when((phase == 0) & (i == 0))
    def _():
        if has_init:
            carf_ref[...] = initf_ref[...]
            carb_ref[...] = initb_ref[...]
        else:
            carf_ref[...] = jnp.zeros_like(carf_ref)
            carb_ref[...] = jnp.zeros_like(carb_ref)

    row = lax.broadcasted_iota(jnp.int32, (CHUNK, CHUNK), 0)
    colm = lax.broadcasted_iota(jnp.int32, (CHUNK, CHUNK), 1)
    cum2, cum2_t, dt_t, w_t, e_t, dec_t = _decay_terms(dt_ref[...], dtb_ref[...], alog_ref[...], heads)

    xs = xs_ref[...]
    bc = bc_ref[...]
    x_t = xs.T

    def state_update(car_ref, off):
        for g in range(N_GROUPS):
            b_g = bc[:, g * D_STATE:(g + 1) * D_STATE].astype(BF16)
            rows = []
            for hh in range(hpg):
                h = g * hpg + hh
                rows.append(x_t[h * SSM_HD:(h + 1) * SSM_HD, :] * w_t[off + h:off + h + 1, :])
            xw = jnp.concatenate(rows, axis=0).astype(BF16)
            st = jnp.dot(xw, b_g, preferred_element_type=F32)
            for hh in range(hpg):
                h = g * hpg + hh
                r = slice(h * SSM_HD, (h + 1) * SSM_HD)
                car_ref[r, :] = car_ref[r, :] * dec_t[off + h:off + h + 1, :] + st[hh * SSM_HD:(hh + 1) * SSM_HD, :]

    @pl.when(phase == 0)
    def _():
        prev_ref[c] = carf_ref[...].astype(BF16)
        state_update(carf_ref, 0)

        @pl.when(i == nc - 1)
        def _():
            sf_ref[...] = carf_ref[...]

    @pl.when(phase == 1)
    def _():
        causal = colm >= row
        anti = colm <= row
        y_parts = []
        for g in range(N_GROUPS):
            b_g = bc[:, g * D_STATE:(g + 1) * D_STATE].astype(BF16)
            c_g = bc[:, (N_GROUPS + g) * D_STATE:(N_GROUPS + g + 1) * D_STATE].astype(BF16)
            cb_t = lax.dot_general(b_g, c_g, (((1,), (1,)), ((), ())), preferred_element_type=F32)
            rg = slice(g * gw, (g + 1) * gw)
            yo_f = lax.dot_general(prev_ref[c, rg, :], c_g, (((1,), (1,)), ((), ())),
                                   preferred_element_type=F32)
            yo_b = lax.dot_general(carb_ref[rg, :].astype(BF16), c_g, (((1,), (1,)), ((), ())),
                                   preferred_element_type=F32)
            for hh in range(hpg):
                h = g * hpg + hh
                hb = heads + h
                x_h = x_t[h * SSM_HD:(h + 1) * SSM_HD, :]
                seg_f = cum2_t[h:h + 1, :] - cum2[:, h:h + 1]
                seg_b = cum2_t[hb:hb + 1, :] - cum2[:, hb:hb + 1]
                w_f = cb_t * jnp.exp2(jnp.where(causal, seg_f, -1e4))
                w_b = cb_t * jnp.exp2(jnp.where(anti, seg_b, -1e4))
                lhs = jnp.concatenate([x_h * dt_t[h:h + 1, :], x_h * dt_t[hb:hb + 1, :]], axis=1)
                rhs = jnp.concatenate([w_f, w_b], axis=0)
                yd = jnp.dot(lhs.astype(BF16), rhs.astype(BF16), preferred_element_type=F32)
                r = slice(hh * SSM_HD, (hh + 1) * SSM_HD)
                y_parts.append(yd + yo_f[r, :] * e_t[h:h + 1, :] + yo_b[r, :] * e_t[hb:hb + 1, :])
        y = jnp.concatenate(y_parts, axis=0).T
        y = y + dskip_ref[...] * xs
        y = y * _silu(z_ref[...])
        for g in range(N_GROUPS):
            cols = slice(g * gw, (g + 1) * gw)
            y_ref[:, cols] = (_rms_rows(y[:, cols]) * gn_ref[:, cols]).astype(BF16)
        state_update(carb_ref, heads)

        @pl.when(i == nc - 1)
        def _():
            sb_ref[...] = carb_ref[...]


def _ssd(xs, bc, dt, proj, zc, p, init_f, init_b):
    b, l, dx = xs.shape
    nbc = bc.shape[2]
    heads = dx // SSM_HD
    nc = l // CHUNK
    has_init = init_f is not None

    def chunk_of(ph, i):
        return jnp.where(ph == 0, i, nc - 1 - i)

    def main(wd, col):
        return pl.BlockSpec((None, CHUNK, wd), lambda bi, ph, i: (bi, chunk_of(ph, i), col))

    def full(shape):
        return pl.BlockSpec(shape, lambda bi, ph, i: (0,) * len(shape))

    state_spec = pl.BlockSpec((None, dx, D_STATE), lambda bi, ph, i: (bi, 0, 0))
    in_specs = [main(dx, 0), main(nbc, 0), main(LANES, 0), main(dx, zc),
                full((1, LANES)), full((1, LANES)), full((1, dx)), full((1, dx))]
    args = [xs, bc, dt, proj, p['a_log'], p['dt_bias'], p['d_skip'], p['g_ssm_norm']]
    if has_init:
        in_specs += [state_spec, state_spec]
        args += [init_f, init_b]
    y_spec = pl.BlockSpec((None, CHUNK, dx), lambda bi, ph, i: (bi, jnp.where(ph == 0, nc - 1, nc - 1 - i), 0))
    return pl.pallas_call(
        functools.partial(_ssd_kernel, nc=nc, heads=heads, has_init=has_init),
        grid=(b, 2, nc),
        in_specs=in_specs,
        out_specs=[y_spec, state_spec, state_spec],
        out_shape=[jax.ShapeDtypeStruct((b, l, dx), BF16),
                   jax.ShapeDtypeStruct((b, dx, D_STATE), F32),
                   jax.ShapeDtypeStruct((b, dx, D_STATE), F32)],
        scratch_shapes=[pltpu.VMEM((dx, D_STATE), F32), pltpu.VMEM((dx, D_STATE), F32),
                        pltpu.VMEM((nc, dx, D_STATE), BF16)],
        compiler_params=_params(("arbitrary", "arbitrary", "arbitrary")),
        name="ssd_scan",
    )(*args)


def _out_proj_kernel(a_ref, y_ref, wa_ref, wy_ref, o_ref):
    o_ref[...] = (jnp.dot(a_ref[...], wa_ref[...], preferred_element_type=F32)
                  + jnp.dot(y_ref[...], wy_ref[...], preferred_element_type=F32))


def _out_proj(att, y, w_out):
    m, wa = att.shape
    d = w_out.shape[1]
    assert y.shape[1] == wa and w_out.shape[0] == 2 * wa
    tm = _pick(m, (1024, 512, 256, 128))
    tn = _pick(d, (1024, 512, 256, 128))
    return pl.pallas_call(
        _out_proj_kernel,
        grid=(m // tm, d // tn),
        in_specs=[pl.BlockSpec((tm, wa), lambda i, j: (i, 0)),
                  pl.BlockSpec((tm, wa), lambda i, j: (i, 0)),
                  pl.BlockSpec((wa, tn), lambda i, j: (0, j)),
                  pl.BlockSpec((wa, tn), lambda i, j: (1, j))],
        out_specs=pl.BlockSpec((tm, tn), lambda i, j: (i, j)),
        out_shape=jax.ShapeDtypeStruct((m, d), F32),
        compiler_params=_params(("arbitrary", "arbitrary")),
        name="out_proj",
    )(att, y, w_out, w_out)


def _sandwich_kernel(mix_ref, x_ref, gpost_ref, gate_ref, gpre_ref, sc_ref, sh_ref,
                     x1_ref, h2_ref, stat_ref, stat2_ref):
    chunks = _col_chunks(x1_ref.shape[1])
    _store_inv_rms(mix_ref, stat_ref)

    def residual(r, carry):
        rows = _rows(r, BF16_ROWS)
        inv = stat_ref[rows, 0:1]
        for cols in chunks:
            n = (mix_ref[rows, cols] * inv) * gpost_ref[:, cols]
            x1_ref[rows, cols] = x_ref[rows, cols] + gate_ref[:, cols] * n
        return carry

    lax.fori_loop(0, x1_ref.shape[0] // BF16_ROWS, residual, 0, unroll=2)
    _store_inv_rms(x1_ref, stat2_ref)

    def prenorm(r, carry):
        rows = _rows(r, BF16_ROWS)
        inv = stat2_ref[rows, 0:1]
        for cols in chunks:
            y = (x1_ref[rows, cols] * inv) * gpre_ref[:, cols]
            h2_ref[rows, cols] = (y * (1.0 + sc_ref[:, cols]) + sh_ref[:, cols]).astype(BF16)
        return carry

    lax.fori_loop(0, x1_ref.shape[0] // BF16_ROWS, prenorm, 0, unroll=2)


def _sandwich(mix, x, mod, g_post, g_pre, mod_rows):
    m, d = x.shape
    tm = _row_tile(m, mod_rows, (256, 128))
    vec = pl.BlockSpec((1, d), lambda i, j: (0, 0))
    tile = pl.BlockSpec((tm, d), lambda i, j: (i, 0))
    return pl.pallas_call(
        _sandwich_kernel,
        grid=(m // tm, 1),
        in_specs=[tile, tile, vec, _mod_spec(d, 2, mod_rows, tm), vec,
                  _mod_spec(d, 4, mod_rows, tm), _mod_spec(d, 3, mod_rows, tm)],
        out_specs=[tile, tile],
        out_shape=[jax.ShapeDtypeStruct((m, d), F32), jax.ShapeDtypeStruct((m, d), BF16)],
        scratch_shapes=[pltpu.VMEM((tm, LANES), F32), pltpu.VMEM((tm, LANES), F32)],
        compiler_params=_params(("arbitrary", "arbitrary")),
        name="mix_sandwich",
    )(mix, x, g_post.reshape(1, d), mod, g_pre.reshape(1, d), mod, mod)


def _up_kernel(h_ref, w_ref, o_ref):
    u = jnp.maximum(jnp.dot(h_ref[...], w_ref[...], preferred_element_type=F32), 0.0)
    o_ref[...] = (u * u).astype(BF16)


def _up(h2, w_up):
    m, d = h2.shape
    f = w_up.shape[1]
    tm = _pick(m, (1024, 512, 256, 128))
    tn = _pick(f, (1024, 512, 256, 128))
    return pl.pallas_call(
        _up_kernel,
        grid=(m // tm, f // tn),
        in_specs=[pl.BlockSpec((tm, d), lambda i, j: (i, 0)),
                  pl.BlockSpec((d, tn), lambda i, j: (0, j))],
        out_specs=pl.BlockSpec((tm, tn), lambda i, j: (i, j)),
        out_shape=jax.ShapeDtypeStruct((m, f), BF16),
        compiler_params=_params(("arbitrary", "arbitrary")),
        name="mlp_up",
    )(h2, w_up)


def _down_kernel(u_ref, w_ref, x_ref, gpost_ref, gate_ref, o_ref, stat_ref, *, nk):
    k = pl.program_id(1)

    @pl.when(k == 0)
    def _():
        o_ref[...] = jnp.dot(u_ref[...], w_ref[...], preferred_element_type=F32)

    @pl.when(k > 0)
    def _():
        o_ref[...] += jnp.dot(u_ref[...], w_ref[...], preferred_element_type=F32)

    @pl.when(k == nk - 1)
    def _():
        _store_inv_rms(o_ref, stat_ref)

        def body(r, carry):
            rows = _rows(r, BF16_ROWS)
            inv = stat_ref[rows, 0:1]
            for cols in _col_chunks(o_ref.shape[1]):
                n = (o_ref[rows, cols] * inv) * gpost_ref[:, cols]
                o_ref[rows, cols] = x_ref[rows, cols] + gate_ref[:, cols] * n
            return carry

        lax.fori_loop(0, o_ref.shape[0] // BF16_ROWS, body, 0, unroll=2)


def _down(u, w_down, x1, mod, g_post, mod_rows):
    m, f = u.shape
    d = x1.shape[1]
    tm = _row_tile(m, mod_rows)
    tk = _pick(f, (1024, 512, 256, 128))
    nk = f // tk
    return pl.pallas_call(
        functools.partial(_down_kernel, nk=nk),
        grid=(m // tm, nk),
        in_specs=[pl.BlockSpec((tm, tk), lambda i, k: (i, k)),
                  pl.BlockSpec((tk, d), lambda i, k: (k, 0)),
                  pl.BlockSpec((tm, d), lambda i, k: (i, 0)),
                  pl.BlockSpec((1, d), lambda i, k: (0, 0)),
                  _mod_spec(d, 5, mod_rows, tm)],
        out_specs=pl.BlockSpec((tm, d), lambda i, k: (i, 0)),
        out_shape=jax.ShapeDtypeStruct((m, d), F32),
        scratch_shapes=[pltpu.VMEM((tm, LANES), F32)],
        compiler_params=_params(("arbitrary", "arbitrary")),
        name="mlp_down",
    )(u, w_down, x1, g_post.reshape(1, d), mod)


def _kv_out_kernel(ki_ref, vi_ref, k_ref, v_ref, *, n_att):
    for hh in range(n_att):
        for j in range(2):
            c0 = (hh * 2 + j) * QK_HD
            k_ref[:, hh, j, :] = ki_ref[:, c0:c0 + QK_HD]
        v_ref[:, hh, :] = vi_ref[:, hh * V_HD:(hh + 1) * V_HD]


def _kv_out(proj3, att_w):
    b, l, _ = proj3.shape
    n_att = att_w // V_HD
    tl = _pick(l, (256, 128))
    return pl.pallas_call(
        functools.partial(_kv_out_kernel, n_att=n_att),
        grid=(b, l // tl),
        in_specs=[pl.BlockSpec((None, tl, att_w), lambda bi, i: (bi, i, 1)),
                  pl.BlockSpec((None, tl, att_w), lambda bi, i: (bi, i, 2))],
        out_specs=[pl.BlockSpec((None, None, tl, n_att, 2, QK_HD), lambda bi, i: (bi, 0, i, 0, 0, 0)),
                   pl.BlockSpec((None, None, tl, n_att, V_HD), lambda bi, i: (bi, 0, i, 0, 0))],
        out_shape=[jax.ShapeDtypeStruct((b, 1, l, n_att, 2, QK_HD), F32),
                   jax.ShapeDtypeStruct((b, 1, l, n_att, V_HD), F32)],
        compiler_params=_params(("arbitrary", "arbitrary")),
        name="kv_out",
    )(proj3, proj3)


def _trunk(x, mod, mod_row0, rows_per_mod, w, ctx_k, ctx_v, init_f, init_b):
    b, l, d = x.shape
    m = b * l
    att_w = d // 2
    d_inner = d - att_w
    n_bc = 2 * N_GROUPS * D_STATE
    latent = ctx_k is not None

    mod_rows = (mod_row0, rows_per_mod)
    x2d = x.reshape(m, d)
    assert att_w == d_inner and (3 * att_w + 2 * d_inner) % n_bc == 0
    h, dt = _prenorm(x2d, mod, w['g_mix_pre'], w['w_in_dt'], mod_rows)
    proj = _in_proj(h, w['w_in'], 3 * att_w + 2 * d_inner + n_bc)
    proj3 = proj.reshape(b, l, -1)
    lams = (w['lq1'], w['lk1'], w['lq2'], w['lk2'])
    if latent:
        att = _attention(proj3, att_w, lams, w['g_subln'], 1, _pick(l, (512, 256, 128)),
                         ctx_k, ctx_v, _rope_tables(l))
    else:
        att = _attention(proj3, att_w, lams, w['g_subln'], att_w // V_HD, l)
    xs, bc = _conv(proj3, w, d_inner, n_bc, 4, (3 * att_w + 2 * d_inner) // n_bc)
    y, fin_f, fin_b = _ssd(xs, bc, dt.reshape(b, l, LANES), proj3, 3, w, init_f, init_b)
    mix = _out_proj(att.reshape(m, att_w), y.reshape(m, d_inner), w['w_out'])
    x1, h2 = _sandwich(mix, x2d, mod, w['g_mix_post'], w['g_mlp_pre'], mod_rows)
    u = _up(h2, w['w_up'])
    x2 = _down(u, w['w_down'], x1, mod, w['g_mlp_post'], mod_rows)
    k_c, v_c = (None, None) if latent else _kv_out(proj3, att_w)
    return x2.reshape(b, l, d), k_c, v_c, fin_f, fin_b


def kernel(x_prompt, x_sample, c, cache_k, cache_v, state_ssm_fwd, state_ssm_bwd, c_ctx, w_ada, b_ada,
           g_mix_pre, g_mix_post, g_mlp_pre, g_mlp_post, w_in, lambda_q1, lambda_k1, lambda_q2, lambda_k2,
           g_subln, conv_w, conv_b, a_log, dt_bias, d_skip, g_ssm_norm, w_out, w_up, w_down):
    bp, lp, d = x_prompt.shape
    bd, ld, _ = x_sample.shape
    depth = w_in.shape[0]
    assert depth == 1 and bd + 1 <= MOD_ROWS
    att_w = d // 2
    d_inner = d - att_w
    heads = d_inner // SSM_HD
    n_att = att_w // V_HD
    n_main = 3 * att_w + 2 * d_inner + 2 * N_GROUPS * D_STATE
    assert 2 * heads <= LANES and w_in.shape[2] == n_main + 2 * heads

    lyr = 0
    pad_l = LANES - 2 * heads
    w = dict(
        g_mix_pre=g_mix_pre[lyr], g_mix_post=g_mix_post[lyr], g_mlp_pre=g_mlp_pre[lyr],
        g_mlp_post=g_mlp_post[lyr],
        w_in=w_in[lyr].astype(BF16),
        w_in_dt=jnp.pad(w_in[lyr, :, n_main:], ((0, 0), (0, pad_l))).astype(BF16),
        lq1=lambda_q1[lyr].reshape(1, QK_HD), lk1=lambda_k1[lyr].reshape(1, QK_HD),
        lq2=lambda_q2[lyr].reshape(1, QK_HD), lk2=lambda_k2[lyr].reshape(1, QK_HD),
        g_subln=g_subln[lyr],
        conv_w_x=conv_w[lyr, :, :d_inner], conv_w_bc=conv_w[lyr, :, d_inner:],
        conv_b_x=conv_b[lyr, :d_inner].reshape(1, -1), conv_b_bc=conv_b[lyr, d_inner:].reshape(1, -1),
        a_log=jnp.pad(a_log[lyr].reshape(1, -1), ((0, 0), (0, pad_l))),
        dt_bias=jnp.pad(dt_bias[lyr].reshape(1, -1), ((0, 0), (0, pad_l))),
        d_skip=jnp.repeat(d_skip[lyr], SSM_HD).reshape(1, -1),
        g_ssm_norm=g_ssm_norm[lyr].reshape(1, -1),
        w_out=w_out[lyr].astype(BF16), w_up=w_up[lyr].astype(BF16), w_down=w_down[lyr].astype(BF16),
    )

    cvec = jnp.concatenate([c_ctx[None, :], c, jnp.zeros((MOD_ROWS - 1 - bd, d), F32)], axis=0)
    mod = _ada(cvec, w_ada[lyr], b_ada[lyr]).reshape(MOD_ROWS, 6, 1, d)

    yp, k_c, v_c, sf, sb = _trunk(x_prompt, mod, 0, bp * lp, w, None, None, None, None)
    ys, _, _, _, _ = _trunk(
        x_sample, mod, 1, ld, w,
        cache_k[:, lyr].reshape(bd, -1, att_w), cache_v[:, lyr].reshape(bd, -1, att_w),
        state_ssm_fwd[:, lyr].reshape(bd, d_inner, D_STATE), state_ssm_bwd[:, lyr].reshape(bd, d_inner, D_STATE))

    return (yp, ys, k_c, v_c,
            sf.reshape(bp, 1, heads, SSM_HD, D_STATE), sb.reshape(bp, 1, heads, SSM_HD, D_STATE))
```

```python
import functools
import math

import jax
import jax.numpy as jnp
from jax import lax
from jax.experimental import pallas as pl
from jax.experimental.pallas import tpu as pltpu

F32 = jnp.float32
BF16 = jnp.bfloat16

GRID_W = 64
V_HD = 256
QK_HD = V_HD // 2
AXIS_DIM = QK_HD // 2
ROPE_BASE = 10000.0
SSM_HD = 64
N_GROUPS = 4
D_STATE = 128
CONV_K = 5
CHUNK = 128
RMS_EPS = 1e-6
LAM_INIT = 0.8 - 0.6 * math.exp(-0.3 * 0)

V7X_VMEM_BYTES = 64 * 1024 * 1024
VMEM_LIMIT = V7X_VMEM_BYTES - 8 * 1024 * 1024
LANES = 128
SUBLANES = 8
CONV_HALO = SUBLANES
NORM_ROWS = SUBLANES
BF16_ROWS = 2 * SUBLANES
STAT_UNROLL = 16
APPLY_COLS = 1024
ROPE_ROWS = 512
CAST_GRIDS = ((16, 8), (8, 8), (4, 4), (2, 2))
MOD_ROWS = 8


def _pick(n, prefs):
    for p in prefs:
        if n % p == 0:
            return p
    return n


def _params(sem):
    return pltpu.CompilerParams(dimension_semantics=sem, vmem_limit_bytes=VMEM_LIMIT)


def _sigmoid(x):
    return 1.0 / (1.0 + jnp.exp(-x))


def _silu(x):
    return x * _sigmoid(x)


def _softplus(x):
    return jnp.maximum(x, 0.0) + jnp.log1p(jnp.exp(-jnp.abs(x)))


def _rms_rows(x, eps=RMS_EPS):
    return x * lax.rsqrt(jnp.mean(x * x, axis=-1, keepdims=True) + eps)


def _inv_rms(x, eps=RMS_EPS):
    return jnp.broadcast_to(lax.rsqrt(jnp.mean(x * x, axis=-1, keepdims=True) + eps), (x.shape[0], LANES))


def _rows(r, n):
    return pl.ds(pl.multiple_of(r * n, n), n)


def _store_inv_rms(src_ref, stat_ref):
    def body(r, carry):
        rows = _rows(r, NORM_ROWS)
        stat_ref[rows, :] = _inv_rms(src_ref[rows, :])
        return carry

    lax.fori_loop(0, src_ref.shape[0] // NORM_ROWS, body, 0, unroll=STAT_UNROLL)


def _col_chunks(d):
    w = _pick(d, (APPLY_COLS,))
    return [slice(c * w, (c + 1) * w) for c in range(d // w)]


def _ada_kernel(c_ref, w_ref, b_ref, o_ref):
    c = c_ref[...]
    s = _silu(c).astype(BF16)
    o_ref[...] = jnp.dot(s, w_ref[...].astype(BF16), preferred_element_type=F32) + b_ref[...]


def _ada(cvec, w_ada, b_ada):
    rows, d = cvec.shape
    n = w_ada.shape[1]
    tn = _pick(n, (512, 256, 128))
    return pl.pallas_call(
        _ada_kernel,
        grid=(n // tn,),
        in_specs=[pl.BlockSpec((rows, d), lambda j: (0, 0)),
                  pl.BlockSpec((d, tn), lambda j: (0, j)),
                  pl.BlockSpec((1, tn), lambda j: (0, j))],
        out_specs=pl.BlockSpec((rows, tn), lambda j: (0, j)),
        out_shape=jax.ShapeDtypeStruct((rows, n), F32),
        compiler_params=_params(("arbitrary",)),
        name="ada_mod",
    )(cvec, w_ada, b_ada.reshape(1, n))


def _row_tile(m, mod_rows, prefs=(512, 256, 128)):
    return _pick(math.gcd(m, mod_rows[1]), prefs)


def _mod_spec(d, which, mod_rows, tm):
    row0, per = mod_rows
    return pl.BlockSpec((None, None, 1, d), lambda i, j: (row0 + (i * tm) // per, which, 0, 0))


def _prenorm_kernel(x_ref, g_ref, sc_ref, sh_ref, wdt_ref, h_ref, dt_ref, stat_ref):
    _store_inv_rms(x_ref, stat_ref)

    def body(r, carry):
        rows = _rows(r, BF16_ROWS)
        inv = stat_ref[rows, 0:1]
        for cols in _col_chunks(x_ref.shape[1]):
            y = (x_ref[rows, cols] * inv) * g_ref[:, cols]
            h_ref[rows, cols] = (y * (1.0 + sc_ref[:, cols]) + sh_ref[:, cols]).astype(BF16)
        return carry

    lax.fori_loop(0, x_ref.shape[0] // BF16_ROWS, body, 0, unroll=2)
    dt_ref[...] = jnp.dot(h_ref[...], wdt_ref[...], preferred_element_type=F32)


def _prenorm(x, mod, g, w_dt, mod_rows):
    m, d = x.shape
    tm = _row_tile(m, mod_rows, (256, 128))
    return pl.pallas_call(
        _prenorm_kernel,
        grid=(m // tm, 1),
        in_specs=[pl.BlockSpec((tm, d), lambda i, j: (i, 0)),
                  pl.BlockSpec((1, d), lambda i, j: (0, 0)),
                  _mod_spec(d, 1, mod_rows, tm),
                  _mod_spec(d, 0, mod_rows, tm),
                  pl.BlockSpec((d, LANES), lambda i, j: (0, 0))],
        out_specs=[pl.BlockSpec((tm, d), lambda i, j: (i, 0)),
                   pl.BlockSpec((tm, LANES), lambda i, j: (i, 0))],
        out_shape=[jax.ShapeDtypeStruct((m, d), BF16), jax.ShapeDtypeStruct((m, LANES), F32)],
        scratch_shapes=[pltpu.VMEM((tm, LANES), F32)],
        compiler_params=_params(("arbitrary", "arbitrary")),
        name="mix_prenorm",
    )(x, g.reshape(1, d), mod, mod, w_dt)


def _side_cast(src, grid):
    r, c = src.shape
    for gr, gc in CAST_GRIDS:
        if r % (gr * BF16_ROWS) == 0 and c % (gc * LANES) == 0 and gr * gc <= grid[0] * grid[1]:
            break
    else:
        return None

    def index(i, j):
        t = jnp.minimum(i * grid[1] + j, gr * gc - 1)
        return (t // gc, t % gc)

    return pl.BlockSpec((r // gr, c // gc), index), jax.ShapeDtypeStruct((r, c), BF16)


def _in_proj_kernel(h_ref, w_ref, *refs):
    if len(refs) == 3:
        src_ref, o_ref, dst_ref = refs
        dst_ref[...] = src_ref[...].astype(BF16)
    else:
        o_ref, = refs
    o_ref[...] = jnp.dot(h_ref[...], w_ref[...], preferred_element_type=F32)


def _in_proj(h, w_in, n, cast_src=None):
    m, d = h.shape
    tm = _pick(m, (1024, 512, 256, 128))
    tn = _pick(n, (1024, 512, 256, 128))
    grid = (m // tm, n // tn)
    in_specs = [pl.BlockSpec((tm, d), lambda i, j: (i, 0)),
                pl.BlockSpec((d, tn), lambda i, j: (0, j))]
    out_specs = [pl.BlockSpec((tm, tn), lambda i, j: (i, j))]
    out_shape = [jax.ShapeDtypeStruct((m, n), F32)]
    args = [h, w_in]
    side = None if cast_src is None else _side_cast(cast_src, grid)
    if side is not None:
        in_specs.append(side[0])
        out_specs.append(side[0])
        out_shape.append(side[1])
        args.append(cast_src)
    outs = pl.pallas_call(
        _in_proj_kernel,
        grid=grid,
        in_specs=in_specs,
        out_specs=out_specs,
        out_shape=out_shape,
        compiler_params=_params(("arbitrary", "arbitrary")),
        name="in_proj",
    )(*args)
    if side is not None:
        return outs[0], outs[1]
    return outs[0], (None if cast_src is None else cast_src.astype(BF16))


def _rope_tile(x, cos, sin_signed, first_half):
    half = AXIS_DIM // 2
    swapped = jnp.where(first_half, pltpu.roll(x, LANES - half, axis=1), pltpu.roll(x, half, axis=1))
    return x * cos + swapped * sin_signed


def _rope_tables(l):
    rows = l // GRID_W
    row = jnp.repeat(jnp.arange(rows, dtype=F32), GRID_W)
    col = jnp.tile(jnp.arange(GRID_W, dtype=F32), rows)
    inv = 1.0 / (ROPE_BASE ** (jnp.arange(0, AXIS_DIM, 2, dtype=F32) / AXIS_DIM))

    def one(pos):
        ang = pos[:, None] * inv[None, :]
        c, s = jnp.cos(ang), jnp.sin(ang)
        return jnp.concatenate([c, c], axis=-1), jnp.concatenate([-s, s], axis=-1)

    cr, sr = one(row)
    cc, sc = one(col)
    return jnp.concatenate([cr, cc], axis=-1), jnp.concatenate([sr, sc], axis=-1)


def _attn_kernel(*refs, heads, ck, n_ctx):
    lq1_ref, lk1_ref, lq2_ref, lk2_ref, g_ref, q_ref, k_ref, v_ref = refs[:8]
    if n_ctx:
        ck_ref, cv_ref, cosq_ref, sinq_ref, cosk_ref, sink_ref, o_ref, kb_ref, vb_ref = refs[8:]
        lane = lax.broadcasted_iota(jnp.int32, (1, LANES), 1)
        first_half = (lane % AXIS_DIM) < (AXIS_DIM // 2)

        @pl.when(pl.program_id(2) == 0)
        def _():
            n_new = k_ref.shape[0]
            kb_ref[0:n_ctx, :] = ck_ref[...].astype(BF16)
            vb_ref[0:n_ctx, :] = cv_ref[...].astype(BF16)
            vb_ref[n_ctx:n_ctx + n_new, :] = v_ref[...].astype(BF16)
            step = _pick(n_new, (ROPE_ROWS,))
            for r0 in range(0, n_new, step):
                rows = slice(r0, r0 + step)
                for c0 in range(0, heads * V_HD, LANES):
                    cols = slice(c0, c0 + LANES)
                    roped = _rope_tile(k_ref[rows, cols], cosk_ref[rows, :], sink_ref[rows, :], first_half)
                    kb_ref[n_ctx + r0:n_ctx + r0 + step, cols] = roped.astype(BF16)

        k_src, v_src = kb_ref, vb_ref
    else:
        o_ref = refs[8]
        k_src, v_src = k_ref, v_ref

    lam = (jnp.exp(jnp.sum(lq1_ref[...] * lk1_ref[...], axis=-1, keepdims=True))
           - jnp.exp(jnp.sum(lq2_ref[...] * lk2_ref[...], axis=-1, keepdims=True)) + LAM_INIT)
    c2 = (QK_HD ** -0.5) * math.log2(math.e)
    nck = k_src.shape[0] // ck
    for h in range(heads):
        base = h * V_HD
        outs = []
        for half in range(2):
            cols = slice(base + half * QK_HD, base + (half + 1) * QK_HD)
            q = q_ref[:, cols]
            if n_ctx:
                q = _rope_tile(q, cosq_ref[...], sinq_ref[...], first_half)
            q = (q * c2).astype(BF16)
            m = l = acc = None
            for c in range(nck):
                rows = slice(c * ck, (c + 1) * ck)
                k = k_src[rows, cols].astype(BF16)
                s = lax.dot_general(q, k, (((1,), (1,)), ((), ())), preferred_element_type=F32)
                mc = jnp.max(s, axis=-1, keepdims=True)
                m_new = mc if c == 0 else jnp.maximum(m, mc)
                e = jnp.exp2(s - m_new)
                ls = jnp.sum(e, axis=-1, keepdims=True)
                pv = jnp.dot(e.astype(BF16), v_src[rows, base:base + V_HD].astype(BF16),
                             preferred_element_type=F32)
                if c == 0:
                    l, acc = ls, pv
                else:
                    alpha = jnp.exp2(m - m_new)
                    l = alpha * l + ls
                    acc = alpha * acc + pv
                m = m_new
            outs.append(acc * (1.0 / l))
        o = outs[0] - lam * outs[1]
        o_ref[:, base:base + V_HD] = ((_rms_rows(o) * g_ref[...]) * (1.0 - LAM_INIT)).astype(BF16)


def _attention(proj, w, lams, g_subln, heads_per_step, tq, ctx_k=None, ctx_v=None, tables=None):
    b, l, _ = proj.shape
    hw = heads_per_step * V_HD
    nh = w // hw
    n_ctx = 0 if ctx_k is None else ctx_k.shape[1]
    ck = _pick(n_ctx + l, (768, 1536, 1024, 512, 256))
    small = pl.BlockSpec((1, QK_HD), lambda bi, hi, qi: (0, 0))
    in_specs = [small, small, small, small,
                pl.BlockSpec((1, V_HD), lambda bi, hi, qi: (0, 0)),
                pl.BlockSpec((None, tq, hw), lambda bi, hi, qi: (bi, qi, hi)),
                pl.BlockSpec((None, l, hw), lambda bi, hi, qi: (bi, 0, nh + hi)),
                pl.BlockSpec((None, l, hw), lambda bi, hi, qi: (bi, 0, 2 * nh + hi))]
    args = [*lams, g_subln.reshape(1, V_HD), proj, proj, proj]
    scratch = []
    if n_ctx:
        ctx_spec = pl.BlockSpec((None, n_ctx, hw), lambda bi, hi, qi: (bi, 0, hi))
        q_tab = pl.BlockSpec((tq, LANES), lambda bi, hi, qi: (qi, 0))
        k_tab = pl.BlockSpec((l, LANES), lambda bi, hi, qi: (0, 0))
        in_specs += [ctx_spec, ctx_spec, q_tab, q_tab, k_tab, k_tab]
        args += [ctx_k, ctx_v, tables[0], tables[1], tables[0], tables[1]]
        scratch = [pltpu.VMEM((n_ctx + l, hw), BF16), pltpu.VMEM((n_ctx + l, hw), BF16)]
    return pl.pallas_call(
        functools.partial(_attn_kernel, heads=heads_per_step, ck=ck, n_ctx=n_ctx),
        grid=(b, nh, l // tq),
        in_specs=in_specs,
        out_specs=pl.BlockSpec((None, tq, hw), lambda bi, hi, qi: (bi, qi, hi)),
        out_shape=jax.ShapeDtypeStruct((b, l, w), BF16),
        scratch_shapes=scratch,
        compiler_params=_params(("arbitrary", "arbitrary", "arbitrary")),
        name="diff_attn",
    )(*args)


def _conv_silu(win_ref, prev_ref, main_ref, next_ref, w_ref, b_ref, has_prev, has_next):
    n = main_ref.shape[0]
    win_ref[0:CONV_HALO, :] = jnp.where(has_prev, prev_ref[...], 0.0)
    win_ref[CONV_HALO:CONV_HALO + n, :] = main_ref[...]
    win_ref[CONV_HALO + n:CONV_HALO + n + CONV_HALO, :] = jnp.where(has_next, next_ref[...], 0.0)
    first = CONV_HALO - CONV_K // 2
    out = b_ref[...]
    for j in range(CONV_K):
        out = out + w_ref[j:j + 1, :] * win_ref[first + j:first + j + n, :]
    return _silu(out)


def _decay_terms(dt_raw, dt_bias, a_log, heads):
    dt = _softplus(dt_raw + dt_bias)
    a = dt * (-jnp.exp(a_log))
    row = lax.broadcasted_iota(jnp.int32, (CHUNK, LANES), 0)
    cs = a
    step = 1
    while step < CHUNK:
        cs = cs + jnp.where(row >= step, pltpu.roll(cs, step, axis=0), 0.0)
        step *= 2
    total = cs[CHUNK - 1:CHUNK, :]
    lane = lax.broadcasted_iota(jnp.int32, (CHUNK, LANES), 1)
    cum = jnp.where(lane < heads, cs, total - cs + a)
    cum_t = cum.T
    dt_t = dt.T
    tot_t = jnp.broadcast_to(total, (CHUNK, LANES)).T
    log2e = math.log2(math.e)
    return (cum * log2e,
            cum_t * log2e,
            dt_t,
            jnp.exp(tot_t - cum_t) * dt_t,
            jnp.exp(cum_t),
            jnp.exp(tot_t))


def _conv_kernel(xs_p, xs_m, xs_n, bc_p, bc_m, bc_n, cwx_ref, cbx_ref, cwbc_ref, cbbc_ref,
                 xo_ref, bco_ref, winx_ref, winbc_ref, *, nt):
    i = pl.program_id(1)
    has_prev = i > 0
    has_next = i < nt - 1
    xo_ref[...] = _conv_silu(winx_ref, xs_p, xs_m, xs_n, cwx_ref, cbx_ref, has_prev, has_next)
    bco_ref[...] = _conv_silu(winbc_ref, bc_p, bc_m, bc_n, cwbc_ref, cbbc_ref, has_prev, has_next).astype(BF16)


def _conv(proj, p, dx, nbc, xc, bcc):
    b, l, _ = proj.shape
    tr = _pick(l, (256, 128))
    nt = l // tr
    hpt = tr // CONV_HALO
    nhalo = l // CONV_HALO

    def main(wd, col):
        return pl.BlockSpec((None, tr, wd), lambda bi, i: (bi, i, col))

    def halo_prev(wd, col):
        return pl.BlockSpec((None, CONV_HALO, wd), lambda bi, i: (bi, jnp.maximum(i * hpt - 1, 0), col))

    def halo_next(wd, col):
        return pl.BlockSpec((None, CONV_HALO, wd),
                            lambda bi, i: (bi, jnp.minimum((i + 1) * hpt, nhalo - 1), col))

    def full(shape):
        return pl.BlockSpec(shape, lambda bi, i: (0,) * len(shape))

    return pl.pallas_call(
        functools.partial(_conv_kernel, nt=nt),
        grid=(b, nt),
        in_specs=[halo_prev(dx, xc), main(dx, xc), halo_next(dx, xc),
                  halo_prev(nbc, bcc), main(nbc, bcc), halo_next(nbc, bcc),
                  full((CONV_K, dx)), full((1, dx)), full((CONV_K, nbc)), full((1, nbc))],
        out_specs=[pl.BlockSpec((None, tr, dx), lambda bi, i: (bi, i, 0)),
                   pl.BlockSpec((None, tr, nbc), lambda bi, i: (bi, i, 0))],
        out_shape=[jax.ShapeDtypeStruct((b, l, dx), F32), jax.ShapeDtypeStruct((b, l, nbc), BF16)],
        scratch_shapes=[pltpu.VMEM((tr + 2 * CONV_HALO, dx), F32),
                        pltpu.VMEM((tr + 2 * CONV_HALO, nbc), F32)],
        compiler_params=_params(("arbitrary", "arbitrary")),
        name="conv_silu",
    )(proj, proj, proj, proj, proj, proj, p['conv_w_x'], p['conv_b_x'], p['conv_w_bc'], p['conv_b_bc'])


def _ssd_kernel(*refs, nc, heads, has_init):
    xs_ref, bc_ref, dt_ref, z_ref, alog_ref, dtb_ref, dskip_ref, gn_ref = refs[:8]
    idx = 8
    if has_init:
        initf_ref, initb_ref = refs[idx], refs[idx + 1]
        idx += 2
    y_ref, sf_ref, sb_ref, carf_ref, carb_ref, prev_ref = refs[idx:idx + 6]

    phase = pl.program_id(1)
    i = pl.program_id(2)
    c = jnp.where(phase == 0, i, nc - 1 - i)
    gw = heads // N_GROUPS * SSM_HD
    hpg = heads // N_GROUPS

    @pl.when((phase == 0) & (i == 0))
    def _():
        if has_init:
            carf_ref[...] = initf_ref[...]
            carb_ref[...] = initb_ref[...]
        else:
            carf_ref[...] = jnp.zeros_like(carf_ref)
            carb_ref[...] = jnp.zeros_like(carb_ref)

    row = lax.broadcasted_iota(jnp.int32, (CHUNK, CHUNK), 0)
    colm = lax.broadcasted_iota(jnp.int32, (CHUNK, CHUNK), 1)
    cum2, cum2_t, dt_t, w_t, e_t, dec_t = _decay_terms(dt_ref[...], dtb_ref[...], alog_ref[...], heads)

    xs = xs_ref[...]
    bc = bc_ref[...]
    x_t = xs.T

    def state_update(car_ref, off):
        for g in range(N_GROUPS):
            b_g = bc[:, g * D_STATE:(g + 1) * D_STATE].astype(BF16)
            rows = []
            for hh in range(hpg):
                h = g * hpg + hh
                rows.append(x_t[h * SSM_HD:(h + 1) * SSM_HD, :] * w_t[off + h:off + h + 1, :])
            xw = jnp.concatenate(rows, axis=0).astype(BF16)
            st = jnp.dot(xw, b_g, preferred_element_type=F32)
            for hh in range(hpg):
                h = g * hpg + hh
                r = slice(h * SSM_HD, (h + 1) * SSM_HD)
                car_ref[r, :] = car_ref[r, :] * dec_t[off + h:off + h + 1, :] + st[hh * SSM_HD:(hh + 1) * SSM_HD, :]

    @pl.when(phase == 0)
    def _():
        prev_ref[c] = carf_ref[...].astype(BF16)
        state_update(carf_ref, 0)

        @pl.when(i == nc - 1)
        def _():
            sf_ref[...] = carf_ref[...]

    @pl.when(phase == 1)
    def _():
        causal = colm >= row
        anti = colm <= row
        y_parts = []
        for g in range(N_GROUPS):
            b_g = bc[:, g * D_STATE:(g + 1) * D_STATE].astype(BF16)
            c_g = bc[:, (N_GROUPS + g) * D_STATE:(N_GROUPS + g + 1) * D_STATE].astype(BF16)
            cb_t = lax.dot_general(b_g, c_g, (((1,), (1,)), ((), ())), preferred_element_type=F32)
            rg = slice(g * gw, (g + 1) * gw)
            yo_f = lax.dot_general(prev_ref[c, rg, :], c_g, (((1,), (1,)), ((), ())),
                                   preferred_element_type=F32)
            yo_b = lax.dot_general(carb_ref[rg, :].astype(BF16), c_g, (((1,), (1,)), ((), ())),
                                   preferred_element_type=F32)
            for hh in range(hpg):
                h = g * hpg + hh
                hb = heads + h
                x_h = x_t[h * SSM_HD:(h + 1) * SSM_HD, :]
                seg_f = cum2_t[h:h + 1, :] - cum2[:, h:h + 1]
                seg_b = cum2_t[hb:hb + 1, :] - cum2[:, hb:hb + 1]
                w_f = cb_t * jnp.exp2(jnp.where(causal, seg_f, -1e4))
                w_b = cb_t * jnp.exp2(jnp.where(anti, seg_b, -1e4))
                lhs = jnp.concatenate([x_h * dt_t[h:h + 1, :], x_h * dt_t[hb:hb + 1, :]], axis=1)
                rhs = jnp.concatenate([w_f, w_b], axis=0)
                yd = jnp.dot(lhs.astype(BF16), rhs.astype(BF16), preferred_element_type=F32)
                r = slice(hh * SSM_HD, (hh + 1) * SSM_HD)
                y_parts.append(yd + yo_f[r, :] * e_t[h:h + 1, :] + yo_b[r, :] * e_t[hb:hb + 1, :])
        y = jnp.concatenate(y_parts, axis=0).T
        y = y + dskip_ref[...] * xs
        y = y * _silu(z_ref[...])
        for g in range(N_GROUPS):
            cols = slice(g * gw, (g + 1) * gw)
            y_ref[:, cols] = (_rms_rows(y[:, cols]) * gn_ref[:, cols]).astype(BF16)
        state_update(carb_ref, heads)

        @pl.when(i == nc - 1)
        def _():
            sb_ref[...] = carb_ref[...]


def _ssd(xs, bc, dt, proj, zc, p, init_f, init_b):
    b, l, dx = xs.shape
    nbc = bc.shape[2]
    heads = dx // SSM_HD
    nc = l // CHUNK
    has_init = init_f is not None

    def chunk_of(ph, i):
        return jnp.where(ph == 0, i, nc - 1 - i)

    def main(wd, col):
        return pl.BlockSpec((None, CHUNK, wd), lambda bi, ph, i: (bi, chunk_of(ph, i), col))

    def full(shape):
        return pl.BlockSpec(shape, lambda bi, ph, i: (0,) * len(shape))

    state_spec = pl.BlockSpec((None, dx, D_STATE), lambda bi, ph, i: (bi, 0, 0))
    in_specs = [main(dx, 0), main(nbc, 0), main(LANES, 0), main(dx, zc),
                full((1, LANES)), full((1, LANES)), full((1, dx)), full((1, dx))]
    args = [xs, bc, dt, proj, p['a_log'], p['dt_bias'], p['d_skip'], p['g_ssm_norm']]
    if has_init:
        in_specs += [state_spec, state_spec]
        args += [init_f, init_b]
    y_spec = pl.BlockSpec((None, CHUNK, dx), lambda bi, ph, i: (bi, jnp.where(ph == 0, nc - 1, nc - 1 - i), 0))
    return pl.pallas_call(
        functools.partial(_ssd_kernel, nc=nc, heads=heads, has_init=has_init),
        grid=(b, 2, nc),
        in_specs=in_specs,
        out_specs=[y_spec, state_spec, state_spec],
        out_shape=[jax.ShapeDtypeStruct((b, l, dx), BF16),
                   jax.ShapeDtypeStruct((b, dx, D_STATE), F32),
                   jax.ShapeDtypeStruct((b, dx, D_STATE), F32)],
        scratch_shapes=[pltpu.VMEM((dx, D_STATE), F32), pltpu.VMEM((dx, D_STATE), F32),
                        pltpu.VMEM((nc, dx, D_STATE), BF16)],
        compiler_params=_params(("arbitrary", "arbitrary", "arbitrary")),
        name="ssd_scan",
    )(*args)


def _out_proj_kernel(a_ref, y_ref, wa_ref, wy_ref, o_ref):
    o_ref[...] = (jnp.dot(a_ref[...], wa_ref[...], preferred_element_type=F32)
                  + jnp.dot(y_ref[...], wy_ref[...], preferred_element_type=F32))


def _out_proj(att, y, w_out):
    m, wa = att.shape
    d = w_out.shape[1]
    assert y.shape[1] == wa and w_out.shape[0] == 2 * wa
    tm = _pick(m, (1024, 512, 256, 128))
    tn = _pick(d, (1024, 512, 256, 128))
    return pl.pallas_call(
        _out_proj_kernel,
        grid=(m // tm, d // tn),
        in_specs=[pl.BlockSpec((tm, wa), lambda i, j: (i, 0)),
                  pl.BlockSpec((tm, wa), lambda i, j: (i, 0)),
                  pl.BlockSpec((wa, tn), lambda i, j: (0, j)),
                  pl.BlockSpec((wa, tn), lambda i, j: (1, j))],
        out_specs=pl.BlockSpec((tm, tn), lambda i, j: (i, j)),
        out_shape=jax.ShapeDtypeStruct((m, d), F32),
        compiler_params=_params(("arbitrary", "arbitrary")),
        name="out_proj",
    )(att, y, w_out, w_out)


def _sandwich_kernel(mix_ref, x_ref, gpost_ref, gate_ref, gpre_ref, sc_ref, sh_ref,
                     x1_ref, h2_ref, stat_ref, stat2_ref):
    chunks = _col_chunks(x1_ref.shape[1])
    _store_inv_rms(mix_ref, stat_ref)

    def residual(r, carry):
        rows = _rows(r, BF16_ROWS)
        inv = stat_ref[rows, 0:1]
        for cols in chunks:
            n = (mix_ref[rows, cols] * inv) * gpost_ref[:, cols]
            x1_ref[rows, cols] = x_ref[rows, cols] + gate_ref[:, cols] * n
        return carry

    lax.fori_loop(0, x1_ref.shape[0] // BF16_ROWS, residual, 0, unroll=2)
    _store_inv_rms(x1_ref, stat2_ref)

    def prenorm(r, carry):
        rows = _rows(r, BF16_ROWS)
        inv = stat2_ref[rows, 0:1]
        for cols in chunks:
            y = (x1_ref[rows, cols] * inv) * gpre_ref[:, cols]
            h2_ref[rows, cols] = (y * (1.0 + sc_ref[:, cols]) + sh_ref[:, cols]).astype(BF16)
        return carry

    lax.fori_loop(0, x1_ref.shape[0] // BF16_ROWS, prenorm, 0, unroll=2)


def _sandwich(mix, x, mod, g_post, g_pre, mod_rows):
    m, d = x.shape
    tm = _row_tile(m, mod_rows, (256, 128))
    vec = pl.BlockSpec((1, d), lambda i, j: (0, 0))
    tile = pl.BlockSpec((tm, d), lambda i, j: (i, 0))
    return pl.pallas_call(
        _sandwich_kernel,
        grid=(m // tm, 1),
        in_specs=[tile, tile, vec, _mod_spec(d, 2, mod_rows, tm), vec,
                  _mod_spec(d, 4, mod_rows, tm), _mod_spec(d, 3, mod_rows, tm)],
        out_specs=[tile, tile],
        out_shape=[jax.ShapeDtypeStruct((m, d), F32), jax.ShapeDtypeStruct((m, d), BF16)],
        scratch_shapes=[pltpu.VMEM((tm, LANES), F32), pltpu.VMEM((tm, LANES), F32)],
        compiler_params=_params(("arbitrary", "arbitrary")),
        name="mix_sandwich",
    )(mix, x, g_post.reshape(1, d), mod, g_pre.reshape(1, d), mod, mod)


def _up_kernel(h_ref, w_ref, *refs):
    if len(refs) == 3:
        src_ref, o_ref, dst_ref = refs
        dst_ref[...] = src_ref[...].astype(BF16)
    else:
        o_ref, = refs
    u = jnp.maximum(jnp.dot(h_ref[...], w_ref[...], preferred_element_type=F32), 0.0)
    o_ref[...] = (u * u).astype(BF16)


def _up(h2, w_up, cast_src=None):
    m, d = h2.shape
    f = w_up.shape[1]
    tm = _pick(m, (1024, 512, 256, 128))
    tn = _pick(f, (1024, 512, 256, 128))
    grid = (m // tm, f // tn)
    in_specs = [pl.BlockSpec((tm, d), lambda i, j: (i, 0)),
                pl.BlockSpec((d, tn), lambda i, j: (0, j))]
    out_specs = [pl.BlockSpec((tm, tn), lambda i, j: (i, j))]
    out_shape = [jax.ShapeDtypeStruct((m, f), BF16)]
    args = [h2, w_up]
    side = None if cast_src is None else _side_cast(cast_src, grid)
    if side is not None:
        in_specs.append(side[0])
        out_specs.append(side[0])
        out_shape.append(side[1])
        args.append(cast_src)
    outs = pl.pallas_call(
        _up_kernel,
        grid=grid,
        in_specs=in_specs,
        out_specs=out_specs,
        out_shape=out_shape,
        compiler_params=_params(("arbitrary", "arbitrary")),
        name="mlp_up",
    )(*args)
    if side is not None:
        return outs[0], outs[1]
    return outs[0], (None if cast_src is None else cast_src.astype(BF16))


def _down_kernel(u_ref, w_ref, x_ref, gpost_ref, gate_ref, o_ref, stat_ref, *, nk):
    k = pl.program_id(1)

    @pl.when(k == 0)
    def _():
        o_ref[...] = jnp.dot(u_ref[...], w_ref[...], preferred_element_type=F32)

    @pl.when(k > 0)
    def _():
        o_ref[...] += jnp.dot(u_ref[...], w_ref[...], preferred_element_type=F32)

    @pl.when(k == nk - 1)
    def _():
        _store_inv_rms(o_ref, stat_ref)

        def body(r, carry):
            rows = _rows(r, BF16_ROWS)
            inv = stat_ref[rows, 0:1]
            for cols in _col_chunks(o_ref.shape[1]):
                n = (o_ref[rows, cols] * inv) * gpost_ref[:, cols]
                o_ref[rows, cols] = x_ref[rows, cols] + gate_ref[:, cols] * n
            return carry

        lax.fori_loop(0, o_ref.shape[0] // BF16_ROWS, body, 0, unroll=2)


def _down(u, w_down, x1, mod, g_post, mod_rows):
    m, f = u.shape
    d = x1.shape[1]
    tm = _row_tile(m, mod_rows)
    tk = _pick(f, (1024, 512, 256, 128))
    nk = f // tk
    return pl.pallas_call(
        functools.partial(_down_kernel, nk=nk),
        grid=(m // tm, nk),
        in_specs=[pl.BlockSpec((tm, tk), lambda i, k: (i, k)),
                  pl.BlockSpec((tk, d), lambda i, k: (k, 0)),
                  pl.BlockSpec((tm, d), lambda i, k: (i, 0)),
                  pl.BlockSpec((1, d), lambda i, k: (0, 0)),
                  _mod_spec(d, 5, mod_rows, tm)],
        out_specs=pl.BlockSpec((tm, d), lambda i, k: (i, 0)),
        out_shape=jax.ShapeDtypeStruct((m, d), F32),
        scratch_shapes=[pltpu.VMEM((tm, LANES), F32)],
        compiler_params=_params(("arbitrary", "arbitrary")),
        name="mlp_down",
    )(u, w_down, x1, g_post.reshape(1, d), mod)


def _kv_out_kernel(ki_ref, vi_ref, k_ref, v_ref, *, n_att):
    for hh in range(n_att):
        for j in range(2):
            c0 = (hh * 2 + j) * QK_HD
            k_ref[:, hh, j, :] = ki_ref[:, c0:c0 + QK_HD]
        v_ref[:, hh, :] = vi_ref[:, hh * V_HD:(hh + 1) * V_HD]


def _kv_out(proj3, att_w):
    b, l, _ = proj3.shape
    n_att = att_w // V_HD
    tl = _pick(l, (256, 128))
    return pl.pallas_call(
        functools.partial(_kv_out_kernel, n_att=n_att),
        grid=(b, l // tl),
        in_specs=[pl.BlockSpec((None, tl, att_w), lambda bi, i: (bi, i, 1)),
                  pl.BlockSpec((None, tl, att_w), lambda bi, i: (bi, i, 2))],
        out_specs=[pl.BlockSpec((None, None, tl, n_att, 2, QK_HD), lambda bi, i: (bi, 0, i, 0, 0, 0)),
                   pl.BlockSpec((None, None, tl, n_att, V_HD), lambda bi, i: (bi, 0, i, 0, 0))],
        out_shape=[jax.ShapeDtypeStruct((b, 1, l, n_att, 2, QK_HD), F32),
                   jax.ShapeDtypeStruct((b, 1, l, n_att, V_HD), F32)],
        compiler_params=_params(("arbitrary", "arbitrary")),
        name="kv_out",
    )(proj3, proj3)


def _trunk(x, mod, mod_row0, rows_per_mod, w, ctx_k, ctx_v, init_f, init_b):
    b, l, d = x.shape
    m = b * l
    att_w = d // 2
    d_inner = d - att_w
    n_bc = 2 * N_GROUPS * D_STATE
    latent = ctx_k is not None

    mod_rows = (mod_row0, rows_per_mod)
    x2d = x.reshape(m, d)
    assert att_w == d_inner and (3 * att_w + 2 * d_inner) % n_bc == 0
    h, dt = _prenorm(x2d, mod, w['g_mix_pre'], w['w_in_dt'], mod_rows)
    w_up, w_down = w['w_up'], w['w_down']
    proj, w_up_cast = _in_proj(h, w['w_in'], 3 * att_w + 2 * d_inner + n_bc,
                               w_up if w_up.dtype == F32 else None)
    w_up = w_up if w_up_cast is None else w_up_cast
    proj3 = proj.reshape(b, l, -1)
    lams = (w['lq1'], w['lk1'], w['lq2'], w['lk2'])
    if latent:
        att = _attention(proj3, att_w, lams, w['g_subln'], 1, _pick(l, (512, 256, 128)),
                         ctx_k, ctx_v, _rope_tables(l))
    else:
        att = _attention(proj3, att_w, lams, w['g_subln'], att_w // V_HD, l)
    xs, bc = _conv(proj3, w, d_inner, n_bc, 4, (3 * att_w + 2 * d_inner) // n_bc)
    y, fin_f, fin_b = _ssd(xs, bc, dt.reshape(b, l, LANES), proj3, 3, w, init_f, init_b)
    mix = _out_proj(att.reshape(m, att_w), y.reshape(m, d_inner), w['w_out'])
    x1, h2 = _sandwich(mix, x2d, mod, w['g_mix_post'], w['g_mlp_pre'], mod_rows)
    u, w_down_cast = _up(h2, w_up, w_down if w_down.dtype == F32 else None)
    w_down = w_down if w_down_cast is None else w_down_cast
    x2 = _down(u, w_down, x1, mod, w['g_mlp_post'], mod_rows)
    k_c, v_c = (None, None) if latent else _kv_out(proj3, att_w)
    return x2.reshape(b, l, d), k_c, v_c, fin_f, fin_b, dict(w, w_up=w_up, w_down=w_down)


def kernel(x_prompt, x_sample, c, cache_k, cache_v, state_ssm_fwd, state_ssm_bwd, c_ctx, w_ada, b_ada,
           g_mix_pre, g_mix_post, g_mlp_pre, g_mlp_post, w_in, lambda_q1, lambda_k1, lambda_q2, lambda_k2,
           g_subln, conv_w, conv_b, a_log, dt_bias, d_skip, g_ssm_norm, w_out, w_up, w_down):
    bp, lp, d = x_prompt.shape
    bd, ld, _ = x_sample.shape
    depth = w_in.shape[0]
    assert depth == 1 and bd + 1 <= MOD_ROWS
    att_w = d // 2
    d_inner = d - att_w
    heads = d_inner // SSM_HD
    n_att = att_w // V_HD
    n_main = 3 * att_w + 2 * d_inner + 2 * N_GROUPS * D_STATE
    assert 2 * heads <= LANES and w_in.shape[2] == n_main + 2 * heads

    lyr = 0
    pad_l = LANES - 2 * heads
    w = dict(
        g_mix_pre=g_mix_pre[lyr], g_mix_post=g_mix_post[lyr], g_mlp_pre=g_mlp_pre[lyr],
        g_mlp_post=g_mlp_post[lyr],
        w_in=w_in[lyr].astype(BF16),
        w_in_dt=jnp.pad(w_in[lyr, :, n_main:], ((0, 0), (0, pad_l))).astype(BF16),
        lq1=lambda_q1[lyr].reshape(1, QK_HD), lk1=lambda_k1[lyr].reshape(1, QK_HD),
        lq2=lambda_q2[lyr].reshape(1, QK_HD), lk2=lambda_k2[lyr].reshape(1, QK_HD),
        g_subln=g_subln[lyr],
        conv_w_x=conv_w[lyr, :, :d_inner], conv_w_bc=conv_w[lyr, :, d_inner:],
        conv_b_x=conv_b[lyr, :d_inner].reshape(1, -1), conv_b_bc=conv_b[lyr, d_inner:].reshape(1, -1),
        a_log=jnp.pad(a_log[lyr].reshape(1, -1), ((0, 0), (0, pad_l))),
        dt_bias=jnp.pad(dt_bias[lyr].reshape(1, -1), ((0, 0), (0, pad_l))),
        d_skip=jnp.repeat(d_skip[lyr], SSM_HD).reshape(1, -1),
        g_ssm_norm=g_ssm_norm[lyr].reshape(1, -1),
        w_out=w_out[lyr].astype(BF16), w_up=w_up[lyr], w_down=w_down[lyr],
    )

    cvec = jnp.concatenate([c_ctx[None, :], c, jnp.zeros((MOD_ROWS - 1 - bd, d), F32)], axis=0)
    mod = _ada(cvec, w_ada[lyr], b_ada[lyr]).reshape(MOD_ROWS, 6, 1, d)

    ys, _, _, _, _, w = _trunk(
        x_sample, mod, 1, ld, w,
        cache_k[:, lyr].reshape(bd, -1, att_w), cache_v[:, lyr].reshape(bd, -1, att_w),
        state_ssm_fwd[:, lyr].reshape(bd, d_inner, D_STATE), state_ssm_bwd[:, lyr].reshape(bd, d_inner, D_STATE))
    yp, k_c, v_c, sf, sb, _ = _trunk(x_prompt, mod, 0, bp * lp, w, None, None, None, None)

    return (yp, ys, k_c, v_c,
            sf.reshape(bp, 1, heads, SSM_HD, D_STATE), sb.reshape(bp, 1, heads, SSM_HD, D_STATE))
```

```python
import functools
import math

import jax
import jax.numpy as jnp
from jax import lax
from jax.experimental import pallas as pl
from jax.experimental.pallas import tpu as pltpu

F32 = jnp.float32
BF16 = jnp.bfloat16

GRID_W = 64
V_HD = 256
QK_HD = V_HD // 2
AXIS_DIM = QK_HD // 2
ROPE_BASE = 10000.0
SSM_HD = 64
N_GROUPS = 4
D_STATE = 128
CONV_K = 5
CHUNK = 128
RMS_EPS = 1e-6
LAM_INIT = 0.8 - 0.6 * math.exp(-0.3 * 0)

V7X_VMEM_BYTES = 64 * 1024 * 1024
VMEM_LIMIT = V7X_VMEM_BYTES - 8 * 1024 * 1024
LANES = 128
SUBLANES = 8
CONV_HALO = SUBLANES
NORM_ROWS = SUBLANES
BF16_ROWS = 2 * SUBLANES
STAT_UNROLL = 16
APPLY_COLS = 1024
ROPE_ROWS = 512
CAST_GRIDS = ((16, 8), (8, 8), (4, 4), (2, 2))
KV_SIDE_ROWS = (64, 32, 16)
MOD_ROWS = 8


def _pick(n, prefs):
    for p in prefs:
        if n % p == 0:
            return p
    return n


def _params(sem):
    return pltpu.CompilerParams(dimension_semantics=sem, vmem_limit_bytes=VMEM_LIMIT)


def _sigmoid(x):
    return 1.0 / (1.0 + jnp.exp(-x))


def _silu(x):
    return x * _sigmoid(x)


def _softplus(x):
    return jnp.maximum(x, 0.0) + jnp.log1p(jnp.exp(-jnp.abs(x)))


def _rms_rows(x, eps=RMS_EPS):
    return x * lax.rsqrt(jnp.mean(x * x, axis=-1, keepdims=True) + eps)


def _inv_rms(x, eps=RMS_EPS):
    return jnp.broadcast_to(lax.rsqrt(jnp.mean(x * x, axis=-1, keepdims=True) + eps), (x.shape[0], LANES))


def _rows(r, n):
    return pl.ds(pl.multiple_of(r * n, n), n)


def _store_inv_rms(src_ref, stat_ref):
    def body(r, carry):
        rows = _rows(r, NORM_ROWS)
        stat_ref[rows, :] = _inv_rms(src_ref[rows, :])
        return carry

    lax.fori_loop(0, src_ref.shape[0] // NORM_ROWS, body, 0, unroll=STAT_UNROLL)


def _col_chunks(d):
    w = _pick(d, (APPLY_COLS,))
    return [slice(c * w, (c + 1) * w) for c in range(d // w)]


def _ada_kernel(c_ref, w_ref, b_ref, o_ref):
    c = c_ref[...]
    s = _silu(c).astype(BF16)
    o_ref[...] = jnp.dot(s, w_ref[...].astype(BF16), preferred_element_type=F32) + b_ref[...]


def _ada(cvec, w_ada, b_ada):
    rows, d = cvec.shape
    n = w_ada.shape[1]
    tn = _pick(n, (512, 256, 128))
    return pl.pallas_call(
        _ada_kernel,
        grid=(n // tn,),
        in_specs=[pl.BlockSpec((rows, d), lambda j: (0, 0)),
                  pl.BlockSpec((d, tn), lambda j: (0, j)),
                  pl.BlockSpec((1, tn), lambda j: (0, j))],
        out_specs=pl.BlockSpec((rows, tn), lambda j: (0, j)),
        out_shape=jax.ShapeDtypeStruct((rows, n), F32),
        compiler_params=_params(("arbitrary",)),
        name="ada_mod",
    )(cvec, w_ada, b_ada.reshape(1, n))


def _row_tile(m, mod_rows, prefs=(512, 256, 128)):
    return _pick(math.gcd(m, mod_rows[1]), prefs)


def _mod_spec(d, which, mod_rows, tm):
    row0, per = mod_rows
    return pl.BlockSpec((None, None, 1, d), lambda i, j: (row0 + (i * tm) // per, which, 0, 0))


def _prenorm_kernel(x_ref, g_ref, sc_ref, sh_ref, wdt_ref, h_ref, dt_ref, stat_ref):
    _store_inv_rms(x_ref, stat_ref)

    def body(r, carry):
        rows = _rows(r, BF16_ROWS)
        inv = stat_ref[rows, 0:1]
        for cols in _col_chunks(x_ref.shape[1]):
            y = (x_ref[rows, cols] * inv) * g_ref[:, cols]
            h_ref[rows, cols] = (y * (1.0 + sc_ref[:, cols]) + sh_ref[:, cols]).astype(BF16)
        return carry

    lax.fori_loop(0, x_ref.shape[0] // BF16_ROWS, body, 0, unroll=2)
    dt_ref[...] = jnp.dot(h_ref[...], wdt_ref[...], preferred_element_type=F32)


def _prenorm(x, mod, g, w_dt, mod_rows):
    m, d = x.shape
    tm = _row_tile(m, mod_rows, (256, 128))
    return pl.pallas_call(
        _prenorm_kernel,
        grid=(m // tm, 1),
        in_specs=[pl.BlockSpec((tm, d), lambda i, j: (i, 0)),
                  pl.BlockSpec((1, d), lambda i, j: (0, 0)),
                  _mod_spec(d, 1, mod_rows, tm),
                  _mod_spec(d, 0, mod_rows, tm),
                  pl.BlockSpec((d, LANES), lambda i, j: (0, 0))],
        out_specs=[pl.BlockSpec((tm, d), lambda i, j: (i, 0)),
                   pl.BlockSpec((tm, LANES), lambda i, j: (i, 0))],
        out_shape=[jax.ShapeDtypeStruct((m, d), BF16), jax.ShapeDtypeStruct((m, LANES), F32)],
        scratch_shapes=[pltpu.VMEM((tm, LANES), F32)],
        compiler_params=_params(("arbitrary", "arbitrary")),
        name="mix_prenorm",
    )(x, g.reshape(1, d), mod, mod, w_dt)


def _cast_kernel(x_ref, o_ref):
    o_ref[...] = x_ref[...].astype(BF16)


def _cast_cols(src, n):
    r = src.shape[0]
    tr = _pick(r, (1024, 512, 256, 128))
    tc = _pick(n, (2816, 2048, 1408, 1024, 512, 256, 128))
    return pl.pallas_call(
        _cast_kernel,
        grid=(r // tr, n // tc),
        in_specs=[pl.BlockSpec((tr, tc), lambda i, j: (i, j))],
        out_specs=pl.BlockSpec((tr, tc), lambda i, j: (i, j)),
        out_shape=jax.ShapeDtypeStruct((r, n), BF16),
        compiler_params=_params(("arbitrary", "arbitrary")),
        name="cast_w_in",
    )(src)


def _side_cast(src, grid):
    r, c = src.shape
    for gr, gc in CAST_GRIDS:
        if r % (gr * BF16_ROWS) == 0 and c % (gc * LANES) == 0 and gr * gc <= grid[0] * grid[1]:
            break
    else:
        return None

    def index(i, j):
        t = jnp.minimum(i * grid[1] + j, gr * gc - 1)
        return (t // gc, t % gc)

    return pl.BlockSpec((r // gr, c // gc), index), jax.ShapeDtypeStruct((r, c), BF16)


def _in_proj_kernel(h_ref, w_ref, *refs):
    if len(refs) == 3:
        src_ref, o_ref, dst_ref = refs
        dst_ref[...] = src_ref[...].astype(BF16)
    else:
        o_ref, = refs
    o_ref[...] = jnp.dot(h_ref[...], w_ref[...], preferred_element_type=F32)


def _in_proj(h, w_in, n, cast_src=None):
    m, d = h.shape
    tm = _pick(m, (1024, 512, 256, 128))
    tn = _pick(n, (1024, 512, 256, 128))
    grid = (m // tm, n // tn)
    in_specs = [pl.BlockSpec((tm, d), lambda i, j: (i, 0)),
                pl.BlockSpec((d, tn), lambda i, j: (0, j))]
    out_specs = [pl.BlockSpec((tm, tn), lambda i, j: (i, j))]
    out_shape = [jax.ShapeDtypeStruct((m, n), F32)]
    args = [h, w_in]
    side = None if cast_src is None else _side_cast(cast_src, grid)
    if side is not None:
        in_specs.append(side[0])
        out_specs.append(side[0])
        out_shape.append(side[1])
        args.append(cast_src)
    outs = pl.pallas_call(
        _in_proj_kernel,
        grid=grid,
        in_specs=in_specs,
        out_specs=out_specs,
        out_shape=out_shape,
        compiler_params=_params(("arbitrary", "arbitrary")),
        name="in_proj",
    )(*args)
    if side is not None:
        return outs[0], outs[1]
    return outs[0], (None if cast_src is None else cast_src.astype(BF16))


def _rope_tile(x, cos, sin_signed, first_half):
    half = AXIS_DIM // 2
    swapped = jnp.where(first_half, pltpu.roll(x, LANES - half, axis=1), pltpu.roll(x, half, axis=1))
    return x * cos + swapped * sin_signed


def _rope_tables(l):
    rows = l // GRID_W
    row = jnp.repeat(jnp.arange(rows, dtype=F32), GRID_W)
    col = jnp.tile(jnp.arange(GRID_W, dtype=F32), rows)
    inv = 1.0 / (ROPE_BASE ** (jnp.arange(0, AXIS_DIM, 2, dtype=F32) / AXIS_DIM))

    def one(pos):
        ang = pos[:, None] * inv[None, :]
        c, s = jnp.cos(ang), jnp.sin(ang)
        return jnp.concatenate([c, c], axis=-1), jnp.concatenate([-s, s], axis=-1)

    cr, sr = one(row)
    cc, sc = one(col)
    return jnp.concatenate([cr, cc], axis=-1), jnp.concatenate([sr, sc], axis=-1)


def _attn_kernel(*refs, heads, ck, n_ctx):
    lq1_ref, lk1_ref, lq2_ref, lk2_ref, g_ref, q_ref, k_ref, v_ref = refs[:8]
    if n_ctx:
        ck_ref, cv_ref, cosq_ref, sinq_ref, cosk_ref, sink_ref, o_ref, kb_ref, vb_ref = refs[8:]
        lane = lax.broadcasted_iota(jnp.int32, (1, LANES), 1)
        first_half = (lane % AXIS_DIM) < (AXIS_DIM // 2)

        @pl.when(pl.program_id(2) == 0)
        def _():
            n_new = k_ref.shape[0]
            kb_ref[0:n_ctx, :] = ck_ref[...].astype(BF16)
            vb_ref[0:n_ctx, :] = cv_ref[...].astype(BF16)
            vb_ref[n_ctx:n_ctx + n_new, :] = v_ref[...].astype(BF16)
            step = _pick(n_new, (ROPE_ROWS,))
            for r0 in range(0, n_new, step):
                rows = slice(r0, r0 + step)
                for c0 in range(0, heads * V_HD, LANES):
                    cols = slice(c0, c0 + LANES)
                    roped = _rope_tile(k_ref[rows, cols], cosk_ref[rows, :], sink_ref[rows, :], first_half)
                    kb_ref[n_ctx + r0:n_ctx + r0 + step, cols] = roped.astype(BF16)

        k_src, v_src = kb_ref, vb_ref
    else:
        o_ref = refs[8]
        k_src, v_src = k_ref, v_ref

    lam = (jnp.exp(jnp.sum(lq1_ref[...] * lk1_ref[...], axis=-1, keepdims=True))
           - jnp.exp(jnp.sum(lq2_ref[...] * lk2_ref[...], axis=-1, keepdims=True)) + LAM_INIT)
    c2 = (QK_HD ** -0.5) * math.log2(math.e)
    nck = k_src.shape[0] // ck
    for h in range(heads):
        base = h * V_HD
        outs = []
        for half in range(2):
            cols = slice(base + half * QK_HD, base + (half + 1) * QK_HD)
            q = q_ref[:, cols]
            if n_ctx:
                q = _rope_tile(q, cosq_ref[...], sinq_ref[...], first_half)
            q = (q * c2).astype(BF16)
            m = l = acc = None
            for c in range(nck):
                rows = slice(c * ck, (c + 1) * ck)
                k = k_src[rows, cols].astype(BF16)
                s = lax.dot_general(q, k, (((1,), (1,)), ((), ())), preferred_element_type=F32)
                mc = jnp.max(s, axis=-1, keepdims=True)
                m_new = mc if c == 0 else jnp.maximum(m, mc)
                e = jnp.exp2(s - m_new)
                ls = jnp.sum(e, axis=-1, keepdims=True)
                pv = jnp.dot(e.astype(BF16), v_src[rows, base:base + V_HD].astype(BF16),
                             preferred_element_type=F32)
                if c == 0:
                    l, acc = ls, pv
                else:
                    alpha = jnp.exp2(m - m_new)
                    l = alpha * l + ls
                    acc = alpha * acc + pv
                m = m_new
            outs.append(acc * (1.0 / l))
        o = outs[0] - lam * outs[1]
        o_ref[:, base:base + V_HD] = ((_rms_rows(o) * g_ref[...]) * (1.0 - LAM_INIT)).astype(BF16)


def _attention(proj, w, lams, g_subln, heads_per_step, tq, ctx_k=None, ctx_v=None, tables=None):
    b, l, _ = proj.shape
    hw = heads_per_step * V_HD
    nh = w // hw
    n_ctx = 0 if ctx_k is None else ctx_k.shape[1]
    ck = _pick(n_ctx + l, (768, 1536, 1024, 512, 256))
    small = pl.BlockSpec((1, QK_HD), lambda bi, hi, qi: (0, 0))
    in_specs = [small, small, small, small,
                pl.BlockSpec((1, V_HD), lambda bi, hi, qi: (0, 0)),
                pl.BlockSpec((None, tq, hw), lambda bi, hi, qi: (bi, qi, hi)),
                pl.BlockSpec((None, l, hw), lambda bi, hi, qi: (bi, 0, nh + hi)),
                pl.BlockSpec((None, l, hw), lambda bi, hi, qi: (bi, 0, 2 * nh + hi))]
    args = [*lams, g_subln.reshape(1, V_HD), proj, proj, proj]
    scratch = []
    if n_ctx:
        ctx_spec = pl.BlockSpec((None, n_ctx, hw), lambda bi, hi, qi: (bi, 0, hi))
        q_tab = pl.BlockSpec((tq, LANES), lambda bi, hi, qi: (qi, 0))
        k_tab = pl.BlockSpec((l, LANES), lambda bi, hi, qi: (0, 0))
        in_specs += [ctx_spec, ctx_spec, q_tab, q_tab, k_tab, k_tab]
        args += [ctx_k, ctx_v, tables[0], tables[1], tables[0], tables[1]]
        scratch = [pltpu.VMEM((n_ctx + l, hw), BF16), pltpu.VMEM((n_ctx + l, hw), BF16)]
    return pl.pallas_call(
        functools.partial(_attn_kernel, heads=heads_per_step, ck=ck, n_ctx=n_ctx),
        grid=(b, nh, l // tq),
        in_specs=in_specs,
        out_specs=pl.BlockSpec((None, tq, hw), lambda bi, hi, qi: (bi, qi, hi)),
        out_shape=jax.ShapeDtypeStruct((b, l, w), BF16),
        scratch_shapes=scratch,
        compiler_params=_params(("arbitrary", "arbitrary", "arbitrary")),
        name="diff_attn",
    )(*args)


def _conv_silu(win_ref, prev_ref, main_ref, next_ref, w_ref, b_ref, has_prev, has_next):
    n = main_ref.shape[0]
    win_ref[0:CONV_HALO, :] = jnp.where(has_prev, prev_ref[...], 0.0)
    win_ref[CONV_HALO:CONV_HALO + n, :] = main_ref[...]
    win_ref[CONV_HALO + n:CONV_HALO + n + CONV_HALO, :] = jnp.where(has_next, next_ref[...], 0.0)
    first = CONV_HALO - CONV_K // 2
    out = b_ref[...]
    for j in range(CONV_K):
        out = out + w_ref[j:j + 1, :] * win_ref[first + j:first + j + n, :]
    return _silu(out)


def _decay_terms(dt_raw, dt_bias, a_log, heads):
    dt = _softplus(dt_raw + dt_bias)
    a = dt * (-jnp.exp(a_log))
    row = lax.broadcasted_iota(jnp.int32, (CHUNK, LANES), 0)
    cs = a
    step = 1
    while step < CHUNK:
        cs = cs + jnp.where(row >= step, pltpu.roll(cs, step, axis=0), 0.0)
        step *= 2
    total = cs[CHUNK - 1:CHUNK, :]
    lane = lax.broadcasted_iota(jnp.int32, (CHUNK, LANES), 1)
    cum = jnp.where(lane < heads, cs, total - cs + a)
    cum_t = cum.T
    dt_t = dt.T
    tot_t = jnp.broadcast_to(total, (CHUNK, LANES)).T
    log2e = math.log2(math.e)
    return (cum * log2e,
            cum_t * log2e,
            dt_t,
            jnp.exp(tot_t - cum_t) * dt_t,
            jnp.exp(cum_t),
            jnp.exp(tot_t))


def _conv_kernel(xs_p, xs_m, xs_n, bc_p, bc_m, bc_n, cwx_ref, cbx_ref, cwbc_ref, cbbc_ref,
                 xo_ref, bco_ref, winx_ref, winbc_ref, *, nt):
    i = pl.program_id(1)
    has_prev = i > 0
    has_next = i < nt - 1
    xo_ref[...] = _conv_silu(winx_ref, xs_p, xs_m, xs_n, cwx_ref, cbx_ref, has_prev, has_next)
    bco_ref[...] = _conv_silu(winbc_ref, bc_p, bc_m, bc_n, cwbc_ref, cbbc_ref, has_prev, has_next).astype(BF16)


def _conv(proj, p, dx, nbc, xc, bcc):
    b, l, _ = proj.shape
    tr = _pick(l, (256, 128))
    nt = l // tr
    hpt = tr // CONV_HALO
    nhalo = l // CONV_HALO

    def main(wd, col):
        return pl.BlockSpec((None, tr, wd), lambda bi, i: (bi, i, col))

    def halo_prev(wd, col):
        return pl.BlockSpec((None, CONV_HALO, wd), lambda bi, i: (bi, jnp.maximum(i * hpt - 1, 0), col))

    def halo_next(wd, col):
        return pl.BlockSpec((None, CONV_HALO, wd),
                            lambda bi, i: (bi, jnp.minimum((i + 1) * hpt, nhalo - 1), col))

    def full(shape):
        return pl.BlockSpec(shape, lambda bi, i: (0,) * len(shape))

    return pl.pallas_call(
        functools.partial(_conv_kernel, nt=nt),
        grid=(b, nt),
        in_specs=[halo_prev(dx, xc), main(dx, xc), halo_next(dx, xc),
                  halo_prev(nbc, bcc), main(nbc, bcc), halo_next(nbc, bcc),
                  full((CONV_K, dx)), full((1, dx)), full((CONV_K, nbc)), full((1, nbc))],
        out_specs=[pl.BlockSpec((None, tr, dx), lambda bi, i: (bi, i, 0)),
                   pl.BlockSpec((None, tr, nbc), lambda bi, i: (bi, i, 0))],
        out_shape=[jax.ShapeDtypeStruct((b, l, dx), F32), jax.ShapeDtypeStruct((b, l, nbc), BF16)],
        scratch_shapes=[pltpu.VMEM((tr + 2 * CONV_HALO, dx), F32),
                        pltpu.VMEM((tr + 2 * CONV_HALO, nbc), F32)],
        compiler_params=_params(("arbitrary", "arbitrary")),
        name="conv_silu",
    )(proj, proj, proj, proj, proj, proj, p['conv_w_x'], p['conv_b_x'], p['conv_w_bc'], p['conv_b_bc'])


def _ssd_kernel(*refs, nc, heads, has_init):
    xs_ref, bc_ref, dt_ref, z_ref, alog_ref, dtb_ref, dskip_ref, gn_ref = refs[:8]
    idx = 8
    if has_init:
        initf_ref, initb_ref = refs[idx], refs[idx + 1]
        idx += 2
    y_ref, sf_ref, sb_ref, carf_ref, carb_ref, prev_ref = refs[idx:idx + 6]

    phase = pl.program_id(1)
    i = pl.program_id(2)
    c = jnp.where(phase == 0, i, nc - 1 - i)
    gw = heads // N_GROUPS * SSM_HD
    hpg = heads // N_GROUPS

    @pl.when((phase == 0) & (i == 0))
    def _():
        if has_init:
            carf_ref[...] = initf_ref[...]
            carb_ref[...] = initb_ref[...]
        else:
            carf_ref[...] = jnp.zeros_like(carf_ref)
            carb_ref[...] = jnp.zeros_like(carb_ref)

    row = lax.broadcasted_iota(jnp.int32, (CHUNK, CHUNK), 0)
    colm = lax.broadcasted_iota(jnp.int32, (CHUNK, CHUNK), 1)
    cum2, cum2_t, dt_t, w_t, e_t, dec_t = _decay_terms(dt_ref[...], dtb_ref[...], alog_ref[...], heads)

    xs = xs_ref[...]
    bc = bc_ref[...]
    x_t = xs.T

    def state_update(car_ref, off):
        for g in range(N_GROUPS):
            b_g = bc[:, g * D_STATE:(g + 1) * D_STATE].astype(BF16)
            rows = []
            for hh in range(hpg):
                h = g * hpg + hh
                rows.append(x_t[h * SSM_HD:(h + 1) * SSM_HD, :] * w_t[off + h:off + h + 1, :])
            xw = jnp.concatenate(rows, axis=0).astype(BF16)
            st = jnp.dot(xw, b_g, preferred_element_type=F32)
            for hh in range(hpg):
                h = g * hpg + hh
                r = slice(h * SSM_HD, (h + 1) * SSM_HD)
                car_ref[r, :] = car_ref[r, :] * dec_t[off + h:off + h + 1, :] + st[hh * SSM_HD:(hh + 1) * SSM_HD, :]

    @pl.when(phase == 0)
    def _():
        prev_ref[c] = carf_ref[...].astype(BF16)
        state_update(carf_ref, 0)

        @pl.when(i == nc - 1)
        def _():
            sf_ref[...] = carf_ref[...]

    @pl.when(phase == 1)
    def _():
        causal = colm >= row
        anti = colm <= row
        y_parts = []
        for g in range(N_GROUPS):
            b_g = bc[:, g * D_STATE:(g + 1) * D_STATE].astype(BF16)
            c_g = bc[:, (N_GROUPS + g) * D_STATE:(N_GROUPS + g + 1) * D_STATE].astype(BF16)
            cb_t = lax.dot_general(b_g, c_g, (((1,), (1,)), ((), ())), preferred_element_type=F32)
            rg = slice(g * gw, (g + 1) * gw)
            yo_f = lax.dot_general(prev_ref[c, rg, :], c_g, (((1,), (1,)), ((), ())),
                                   preferred_element_type=F32)
            yo_b = lax.dot_general(carb_ref[rg, :].astype(BF16), c_g, (((1,), (1,)), ((), ())),
                                   preferred_element_type=F32)
            for hh in range(hpg):
                h = g * hpg + hh
                hb = heads + h
                x_h = x_t[h * SSM_HD:(h + 1) * SSM_HD, :]
                seg_f = cum2_t[h:h + 1, :] - cum2[:, h:h + 1]
                seg_b = cum2_t[hb:hb + 1, :] - cum2[:, hb:hb + 1]
                w_f = cb_t * jnp.exp2(jnp.where(causal, seg_f, -1e4))
                w_b = cb_t * jnp.exp2(jnp.where(anti, seg_b, -1e4))
                lhs = jnp.concatenate([x_h * dt_t[h:h + 1, :], x_h * dt_t[hb:hb + 1, :]], axis=1)
                rhs = jnp.concatenate([w_f, w_b], axis=0)
                yd = jnp.dot(lhs.astype(BF16), rhs.astype(BF16), preferred_element_type=F32)
                r = slice(hh * SSM_HD, (hh + 1) * SSM_HD)
                y_parts.append(yd + yo_f[r, :] * e_t[h:h + 1, :] + yo_b[r, :] * e_t[hb:hb + 1, :])
        y = jnp.concatenate(y_parts, axis=0).T
        y = y + dskip_ref[...] * xs
        y = y * _silu(z_ref[...])
        for g in range(N_GROUPS):
            cols = slice(g * gw, (g + 1) * gw)
            y_ref[:, cols] = (_rms_rows(y[:, cols]) * gn_ref[:, cols]).astype(BF16)
        state_update(carb_ref, heads)

        @pl.when(i == nc - 1)
        def _():
            sb_ref[...] = carb_ref[...]


def _ssd(xs, bc, dt, proj, zc, p, init_f, init_b):
    b, l, dx = xs.shape
    nbc = bc.shape[2]
    heads = dx // SSM_HD
    nc = l // CHUNK
    has_init = init_f is not None

    def chunk_of(ph, i):
        return jnp.where(ph == 0, i, nc - 1 - i)

    def main(wd, col):
        return pl.BlockSpec((None, CHUNK, wd), lambda bi, ph, i: (bi, chunk_of(ph, i), col))

    def full(shape):
        return pl.BlockSpec(shape, lambda bi, ph, i: (0,) * len(shape))

    state_spec = pl.BlockSpec((None, dx, D_STATE), lambda bi, ph, i: (bi, 0, 0))
    in_specs = [main(dx, 0), main(nbc, 0), main(LANES, 0), main(dx, zc),
                full((1, LANES)), full((1, LANES)), full((1, dx)), full((1, dx))]
    args = [xs, bc, dt, proj, p['a_log'], p['dt_bias'], p['d_skip'], p['g_ssm_norm']]
    if has_init:
        in_specs += [state_spec, state_spec]
        args += [init_f, init_b]
    y_spec = pl.BlockSpec((None, CHUNK, dx), lambda bi, ph, i: (bi, jnp.where(ph == 0, nc - 1, nc - 1 - i), 0))
    return pl.pallas_call(
        functools.partial(_ssd_kernel, nc=nc, heads=heads, has_init=has_init),
        grid=(b, 2, nc),
        in_specs=in_specs,
        out_specs=[y_spec, state_spec, state_spec],
        out_shape=[jax.ShapeDtypeStruct((b, l, dx), BF16),
                   jax.ShapeDtypeStruct((b, dx, D_STATE), F32),
                   jax.ShapeDtypeStruct((b, dx, D_STATE), F32)],
        scratch_shapes=[pltpu.VMEM((dx, D_STATE), F32), pltpu.VMEM((dx, D_STATE), F32),
                        pltpu.VMEM((nc, dx, D_STATE), BF16)],
        compiler_params=_params(("arbitrary", "arbitrary", "arbitrary")),
        name="ssd_scan",
    )(*args)


def _out_proj_kernel(a_ref, y_ref, wa_ref, wy_ref, o_ref):
    o_ref[...] = (jnp.dot(a_ref[...], wa_ref[...], preferred_element_type=F32)
                  + jnp.dot(y_ref[...], wy_ref[...], preferred_element_type=F32))


def _out_proj(att, y, w_out):
    m, wa = att.shape
    d = w_out.shape[1]
    assert y.shape[1] == wa and w_out.shape[0] == 2 * wa
    tm = _pick(m, (1024, 512, 256, 128))
    tn = _pick(d, (1024, 512, 256, 128))
    return pl.pallas_call(
        _out_proj_kernel,
        grid=(m // tm, d // tn),
        in_specs=[pl.BlockSpec((tm, wa), lambda i, j: (i, 0)),
                  pl.BlockSpec((tm, wa), lambda i, j: (i, 0)),
                  pl.BlockSpec((wa, tn), lambda i, j: (0, j)),
                  pl.BlockSpec((wa, tn), lambda i, j: (1, j))],
        out_specs=pl.BlockSpec((tm, tn), lambda i, j: (i, j)),
        out_shape=jax.ShapeDtypeStruct((m, d), F32),
        compiler_params=_params(("arbitrary", "arbitrary")),
        name="out_proj",
    )(att, y, w_out, w_out)


def _sandwich_kernel(mix_ref, x_ref, gpost_ref, gate_ref, gpre_ref, sc_ref, sh_ref,
                     x1_ref, h2_ref, stat_ref, stat2_ref):
    chunks = _col_chunks(x1_ref.shape[1])
    _store_inv_rms(mix_ref, stat_ref)

    def residual(r, carry):
        rows = _rows(r, BF16_ROWS)
        inv = stat_ref[rows, 0:1]
        for cols in chunks:
            n = (mix_ref[rows, cols] * inv) * gpost_ref[:, cols]
            x1_ref[rows, cols] = x_ref[rows, cols] + gate_ref[:, cols] * n
        return carry

    lax.fori_loop(0, x1_ref.shape[0] // BF16_ROWS, residual, 0, unroll=2)
    _store_inv_rms(x1_ref, stat2_ref)

    def prenorm(r, carry):
        rows = _rows(r, BF16_ROWS)
        inv = stat2_ref[rows, 0:1]
        for cols in chunks:
            y = (x1_ref[rows, cols] * inv) * gpre_ref[:, cols]
            h2_ref[rows, cols] = (y * (1.0 + sc_ref[:, cols]) + sh_ref[:, cols]).astype(BF16)
        return carry

    lax.fori_loop(0, x1_ref.shape[0] // BF16_ROWS, prenorm, 0, unroll=2)


def _sandwich(mix, x, mod, g_post, g_pre, mod_rows):
    m, d = x.shape
    tm = _row_tile(m, mod_rows, (256, 128))
    vec = pl.BlockSpec((1, d), lambda i, j: (0, 0))
    tile = pl.BlockSpec((tm, d), lambda i, j: (i, 0))
    return pl.pallas_call(
        _sandwich_kernel,
        grid=(m // tm, 1),
        in_specs=[tile, tile, vec, _mod_spec(d, 2, mod_rows, tm), vec,
                  _mod_spec(d, 4, mod_rows, tm), _mod_spec(d, 3, mod_rows, tm)],
        out_specs=[tile, tile],
        out_shape=[jax.ShapeDtypeStruct((m, d), F32), jax.ShapeDtypeStruct((m, d), BF16)],
        scratch_shapes=[pltpu.VMEM((tm, LANES), F32), pltpu.VMEM((tm, LANES), F32)],
        compiler_params=_params(("arbitrary", "arbitrary")),
        name="mix_sandwich",
    )(mix, x, g_post.reshape(1, d), mod, g_pre.reshape(1, d), mod, mod)


def _up_kernel(h_ref, w_ref, *refs, cast, n_att):
    n_side_in = int(cast) + 2 * int(n_att > 0)
    side_in, o_ref, side_out = refs[:n_side_in], refs[n_side_in], refs[n_side_in + 1:]
    if cast:
        side_out[0][...] = side_in[0][...].astype(BF16)
    if n_att:
        _kv_retile(side_in[-2], side_in[-1], side_out[-2], side_out[-1], n_att)
    u = jnp.maximum(jnp.dot(h_ref[...], w_ref[...], preferred_element_type=F32), 0.0)
    o_ref[...] = (u * u).astype(BF16)


def _up(h2, w_up, cast_src=None, kv=None):
    m, d = h2.shape
    f = w_up.shape[1]
    tm = _pick(m, (1024, 512, 256, 128))
    tn = _pick(f, (1024, 512, 256, 128))
    grid = (m // tm, f // tn)
    steps = grid[0] * grid[1]
    in_specs = [pl.BlockSpec((tm, d), lambda i, j: (i, 0)),
                pl.BlockSpec((d, tn), lambda i, j: (0, j))]
    out_specs = [pl.BlockSpec((tm, tn), lambda i, j: (i, j))]
    out_shape = [jax.ShapeDtypeStruct((m, f), BF16)]
    args = [h2, w_up]
    side = None if cast_src is None else _side_cast(cast_src, grid)
    if side is not None:
        in_specs.append(side[0])
        out_specs.append(side[0])
        out_shape.append(side[1])
        args.append(cast_src)
    n_att = 0
    if kv is not None:
        proj3, att_w = kv
        b, l, _ = proj3.shape
        tl = next((t for t in KV_SIDE_ROWS if l % t == 0 and b * (l // t) <= steps), None)
        if tl is not None:
            per, nb = l // tl, b * (l // tl)

            def block_of(i, j):
                t = jnp.minimum(i * grid[1] + j, nb - 1)
                return t // per, t % per

            n_att = att_w // V_HD
            kv_in, kv_out, kv_shape = _kv_specs(proj3, att_w, tl, block_of)
            in_specs += kv_in
            out_specs += kv_out
            out_shape += kv_shape
            args += [proj3, proj3]
    outs = pl.pallas_call(
        functools.partial(_up_kernel, cast=side is not None, n_att=n_att),
        grid=grid,
        in_specs=in_specs,
        out_specs=out_specs,
        out_shape=out_shape,
        compiler_params=_params(("arbitrary", "arbitrary")),
        name="mlp_up",
    )(*args)
    cast_out = outs[1] if side is not None else (None if cast_src is None else cast_src.astype(BF16))
    kv_out = (outs[-2], outs[-1]) if n_att else None
    return outs[0], cast_out, kv_out


def _down_kernel(u_ref, w_ref, x_ref, gpost_ref, gate_ref, o_ref, stat_ref, gg_ref, *, nk):
    k = pl.program_id(1)

    @pl.when(k == 0)
    def _():
        o_ref[...] = jnp.dot(u_ref[...], w_ref[...], preferred_element_type=F32)

    @pl.when(k > 0)
    def _():
        o_ref[...] += jnp.dot(u_ref[...], w_ref[...], preferred_element_type=F32)

    @pl.when(k == nk - 1)
    def _():
        _store_inv_rms(o_ref, stat_ref)
        gg_ref[...] = gate_ref[...] * gpost_ref[...]

        def body(r, carry):
            rows = _rows(r, BF16_ROWS)
            inv = stat_ref[rows, 0:1]
            for cols in _col_chunks(o_ref.shape[1]):
                o_ref[rows, cols] = x_ref[rows, cols] + (o_ref[rows, cols] * inv) * gg_ref[:, cols]
            return carry

        lax.fori_loop(0, o_ref.shape[0] // BF16_ROWS, body, 0, unroll=2)


def _down(u, w_down, x1, mod, g_post, mod_rows):
    m, f = u.shape
    d = x1.shape[1]
    tm = _row_tile(m, mod_rows)
    tk = _pick(f, (1024, 512, 256, 128))
    nk = f // tk
    return pl.pallas_call(
        functools.partial(_down_kernel, nk=nk),
        grid=(m // tm, nk),
        in_specs=[pl.BlockSpec((tm, tk), lambda i, k: (i, k)),
                  pl.BlockSpec((tk, d), lambda i, k: (k, 0)),
                  pl.BlockSpec((tm, d), lambda i, k: (i, 0)),
                  pl.BlockSpec((1, d), lambda i, k: (0, 0)),
                  _mod_spec(d, 5, mod_rows, tm)],
        out_specs=pl.BlockSpec((tm, d), lambda i, k: (i, 0)),
        out_shape=jax.ShapeDtypeStruct((m, d), F32),
        scratch_shapes=[pltpu.VMEM((tm, LANES), F32), pltpu.VMEM((1, d), F32)],
        compiler_params=_params(("arbitrary", "arbitrary")),
        name="mlp_down",
    )(u, w_down, x1, g_post.reshape(1, d), mod)


def _kv_retile(ki_ref, vi_ref, k_ref, v_ref, n_att):
    for hh in range(n_att):
        for j in range(2):
            c0 = (hh * 2 + j) * QK_HD
            k_ref[:, hh, j, :] = ki_ref[:, c0:c0 + QK_HD]
        v_ref[:, hh, :] = vi_ref[:, hh * V_HD:(hh + 1) * V_HD]


def _kv_specs(proj3, att_w, tl, block_of):
    b, l, _ = proj3.shape
    n_att = att_w // V_HD

    def tokens(col):
        return pl.BlockSpec((None, tl, att_w), lambda *g: (*block_of(*g), col))

    def cache(tail):
        zeros = (0,) * len(tail)
        return pl.BlockSpec((None, None, tl, *tail),
                            lambda *g: (block_of(*g)[0], 0, block_of(*g)[1], *zeros))

    return ([tokens(1), tokens(2)],
            [cache((n_att, 2, QK_HD)), cache((n_att, V_HD))],
            [jax.ShapeDtypeStruct((b, 1, l, n_att, 2, QK_HD), F32),
             jax.ShapeDtypeStruct((b, 1, l, n_att, V_HD), F32)])


def _kv_out_kernel(ki_ref, vi_ref, k_ref, v_ref, *, n_att):
    _kv_retile(ki_ref, vi_ref, k_ref, v_ref, n_att)


def _kv_out(proj3, att_w):
    b, l, _ = proj3.shape
    tl = _pick(l, (256, 128))
    in_specs, out_specs, out_shape = _kv_specs(proj3, att_w, tl, lambda bi, i: (bi, i))
    return pl.pallas_call(
        functools.partial(_kv_out_kernel, n_att=att_w // V_HD),
        grid=(b, l // tl),
        in_specs=in_specs,
        out_specs=out_specs,
        out_shape=out_shape,
        compiler_params=_params(("arbitrary", "arbitrary")),
        name="kv_out",
    )(proj3, proj3)


def _trunk(x, mod, mod_row0, rows_per_mod, w, ctx_k, ctx_v, init_f, init_b):
    b, l, d = x.shape
    m = b * l
    att_w = d // 2
    d_inner = d - att_w
    n_bc = 2 * N_GROUPS * D_STATE
    latent = ctx_k is not None

    mod_rows = (mod_row0, rows_per_mod)
    x2d = x.reshape(m, d)
    assert att_w == d_inner and (3 * att_w + 2 * d_inner) % n_bc == 0
    h, dt = _prenorm(x2d, mod, w['g_mix_pre'], w['w_in_dt'], mod_rows)
    w_up, w_down = w['w_up'], w['w_down']
    proj, w_up_cast = _in_proj(h, w['w_in'], 3 * att_w + 2 * d_inner + n_bc,
                               w_up if w_up.dtype == F32 else None)
    w_up = w_up if w_up_cast is None else w_up_cast
    proj3 = proj.reshape(b, l, -1)
    lams = (w['lq1'], w['lk1'], w['lq2'], w['lk2'])
    if latent:
        att = _attention(proj3, att_w, lams, w['g_subln'], 1, _pick(l, (512, 256, 128)),
                         ctx_k, ctx_v, _rope_tables(l))
    else:
        att = _attention(proj3, att_w, lams, w['g_subln'], att_w // V_HD, l)
    xs, bc = _conv(proj3, w, d_inner, n_bc, 4, (3 * att_w + 2 * d_inner) // n_bc)
    y, fin_f, fin_b = _ssd(xs, bc, dt.reshape(b, l, LANES), proj3, 3, w, init_f, init_b)
    mix = _out_proj(att.reshape(m, att_w), y.reshape(m, d_inner), w['w_out'])
    x1, h2 = _sandwich(mix, x2d, mod, w['g_mix_post'], w['g_mlp_pre'], mod_rows)
    u, w_down_cast, kv = _up(h2, w_up, w_down if w_down.dtype == F32 else None,
                             None if latent else (proj3, att_w))
    w_down = w_down if w_down_cast is None else w_down_cast
    x2 = _down(u, w_down, x1, mod, w['g_mlp_post'], mod_rows)
    k_c, v_c = (None, None) if latent else (kv or _kv_out(proj3, att_w))
    return x2.reshape(b, l, d), k_c, v_c, fin_f, fin_b, dict(w, w_up=w_up, w_down=w_down)


def kernel(x_prompt, x_sample, c, cache_k, cache_v, state_ssm_fwd, state_ssm_bwd, c_ctx, w_ada, b_ada,
           g_mix_pre, g_mix_post, g_mlp_pre, g_mlp_post, w_in, lambda_q1, lambda_k1, lambda_q2, lambda_k2,
           g_subln, conv_w, conv_b, a_log, dt_bias, d_skip, g_ssm_norm, w_out, w_up, w_down):
    bp, lp, d = x_prompt.shape
    bd, ld, _ = x_sample.shape
    depth = w_in.shape[0]
    assert depth == 1 and bd + 1 <= MOD_ROWS
    att_w = d // 2
    d_inner = d - att_w
    heads = d_inner // SSM_HD
    n_att = att_w // V_HD
    n_main = 3 * att_w + 2 * d_inner + 2 * N_GROUPS * D_STATE
    assert 2 * heads <= LANES and w_in.shape[2] == n_main + 2 * heads

    lyr = 0
    pad_l = LANES - 2 * heads
    w = dict(
        g_mix_pre=g_mix_pre[lyr], g_mix_post=g_mix_post[lyr], g_mlp_pre=g_mlp_pre[lyr],
        g_mlp_post=g_mlp_post[lyr],
        w_in=_cast_cols(w_in[lyr], n_main),
        w_in_dt=jnp.pad(w_in[lyr, :, n_main:], ((0, 0), (0, pad_l))).astype(BF16),
        lq1=lambda_q1[lyr].reshape(1, QK_HD), lk1=lambda_k1[lyr].reshape(1, QK_HD),
        lq2=lambda_q2[lyr].reshape(1, QK_HD), lk2=lambda_k2[lyr].reshape(1, QK_HD),
        g_subln=g_subln[lyr],
        conv_w_x=conv_w[lyr, :, :d_inner], conv_w_bc=conv_w[lyr, :, d_inner:],
        conv_b_x=conv_b[lyr, :d_inner].reshape(1, -1), conv_b_bc=conv_b[lyr, d_inner:].reshape(1, -1),
        a_log=jnp.pad(a_log[lyr].reshape(1, -1), ((0, 0), (0, pad_l))),
        dt_bias=jnp.pad(dt_bias[lyr].reshape(1, -1), ((0, 0), (0, pad_l))),
        d_skip=jnp.repeat(d_skip[lyr], SSM_HD).reshape(1, -1),
        g_ssm_norm=g_ssm_norm[lyr].reshape(1, -1),
        w_out=w_out[lyr].astype(BF16), w_up=w_up[lyr], w_down=w_down[lyr],
    )

    cvec = jnp.concatenate([c_ctx[None, :], c, jnp.zeros((MOD_ROWS - 1 - bd, d), F32)], axis=0)
    mod = _ada(cvec, w_ada[lyr], b_ada[lyr]).reshape(MOD_ROWS, 6, 1, d)

    ys, _, _, _, _, w = _trunk(
        x_sample, mod, 1, ld, w,
        cache_k[:, lyr].reshape(bd, -1, att_w), cache_v[:, lyr].reshape(bd, -1, att_w),
        state_ssm_fwd[:, lyr].reshape(bd, d_inner, D_STATE), state_ssm_bwd[:, lyr].reshape(bd, d_inner, D_STATE))
    yp, k_c, v_c, sf, sb, _ = _trunk(x_prompt, mod, 0, bp * lp, w, None, None, None, None)

    return (yp, ys, k_c, v_c,
            sf.reshape(bp, 1, heads, SSM_HD, D_STATE), sb.reshape(bp, 1, heads, SSM_HD, D_STATE))
```

```python
import functools
import math

import jax
import jax.numpy as jnp
from jax import lax
from jax.experimental import pallas as pl
from jax.experimental.pallas import tpu as pltpu

F32 = jnp.float32
BF16 = jnp.bfloat16

GRID_W = 64
V_HD = 256
QK_HD = V_HD // 2
AXIS_DIM = QK_HD // 2
ROPE_BASE = 10000.0
SSM_HD = 64
N_GROUPS = 4
D_STATE = 128
CONV_K = 5
CHUNK = 128
RMS_EPS = 1e-6
LAM_INIT = 0.8 - 0.6 * math.exp(-0.3 * 0)

V7X_VMEM_BYTES = 64 * 1024 * 1024
VMEM_LIMIT = V7X_VMEM_BYTES - 8 * 1024 * 1024
LANES = 128
SUBLANES = 8
CONV_HALO = SUBLANES
NORM_ROWS = SUBLANES
BF16_ROWS = 2 * SUBLANES
STAT_UNROLL = 16
APPLY_COLS = 1024
ROPE_ROWS = 512
CAST_GRIDS = ((16, 8), (8, 8), (4, 4), (2, 2))
HOST_ROWS = 64
MOD_ROWS = 8


def _pick(n, prefs):
    for p in prefs:
        if n % p == 0:
            return p
    return n


def _params(sem):
    return pltpu.CompilerParams(dimension_semantics=sem, vmem_limit_bytes=VMEM_LIMIT)


def _sigmoid(x):
    return 1.0 / (1.0 + jnp.exp(-x))


def _silu(x):
    return x * _sigmoid(x)


def _softplus(x):
    return jnp.maximum(x, 0.0) + jnp.log1p(jnp.exp(-jnp.abs(x)))


def _rms_rows(x, eps=RMS_EPS):
    return x * lax.rsqrt(jnp.mean(x * x, axis=-1, keepdims=True) + eps)


def _inv_rms(x, eps=RMS_EPS):
    return jnp.broadcast_to(lax.rsqrt(jnp.mean(x * x, axis=-1, keepdims=True) + eps), (x.shape[0], LANES))


def _rows(r, n):
    return pl.ds(pl.multiple_of(r * n, n), n)


def _store_inv_rms(src_ref, stat_ref):
    def body(r, carry):
        rows = _rows(r, NORM_ROWS)
        stat_ref[rows, :] = _inv_rms(src_ref[rows, :])
        return carry

    lax.fori_loop(0, src_ref.shape[0] // NORM_ROWS, body, 0, unroll=STAT_UNROLL)


def _col_chunks(d):
    w = _pick(d, (APPLY_COLS,))
    return [slice(c * w, (c + 1) * w) for c in range(d // w)]


def _ada_kernel(c_ref, w_ref, b_ref, o_ref):
    c = c_ref[...]
    s = _silu(c).astype(BF16)
    o_ref[...] = jnp.dot(s, w_ref[...].astype(BF16), preferred_element_type=F32) + b_ref[...]


def _ada(cvec, w_ada, b_ada):
    rows, d = cvec.shape
    n = w_ada.shape[1]
    tn = _pick(n, (512, 256, 128))
    return pl.pallas_call(
        _ada_kernel,
        grid=(n // tn,),
        in_specs=[pl.BlockSpec((rows, d), lambda j: (0, 0)),
                  pl.BlockSpec((d, tn), lambda j: (0, j)),
                  pl.BlockSpec((1, tn), lambda j: (0, j))],
        out_specs=pl.BlockSpec((rows, tn), lambda j: (0, j)),
        out_shape=jax.ShapeDtypeStruct((rows, n), F32),
        compiler_params=_params(("arbitrary",)),
        name="ada_mod",
    )(cvec, w_ada, b_ada.reshape(1, n))


def _row_tile(m, mod_rows, prefs=(512, 256, 128)):
    return _pick(math.gcd(m, mod_rows[1]), prefs)


def _mod_spec(d, which, mod_rows, tm):
    row0, per = mod_rows
    return pl.BlockSpec((None, None, 1, d), lambda i, j: (row0 + (i * tm) // per, which, 0, 0))


def _prenorm_kernel(x_ref, g_ref, sc_ref, sh_ref, wdt_ref, h_ref, dt_ref, stat_ref):
    _store_inv_rms(x_ref, stat_ref)

    def body(r, carry):
        rows = _rows(r, BF16_ROWS)
        inv = stat_ref[rows, 0:1]
        for cols in _col_chunks(x_ref.shape[1]):
            y = (x_ref[rows, cols] * inv) * g_ref[:, cols]
            h_ref[rows, cols] = (y * (1.0 + sc_ref[:, cols]) + sh_ref[:, cols]).astype(BF16)
        return carry

    lax.fori_loop(0, x_ref.shape[0] // BF16_ROWS, body, 0, unroll=2)
    dt_ref[...] = jnp.dot(h_ref[...], wdt_ref[...], preferred_element_type=F32)


def _prenorm(x, mod, g, w_dt, mod_rows):
    m, d = x.shape
    tm = _row_tile(m, mod_rows, (256, 128))
    return pl.pallas_call(
        _prenorm_kernel,
        grid=(m // tm, 1),
        in_specs=[pl.BlockSpec((tm, d), lambda i, j: (i, 0)),
                  pl.BlockSpec((1, d), lambda i, j: (0, 0)),
                  _mod_spec(d, 1, mod_rows, tm),
                  _mod_spec(d, 0, mod_rows, tm),
                  pl.BlockSpec((d, LANES), lambda i, j: (0, 0))],
        out_specs=[pl.BlockSpec((tm, d), lambda i, j: (i, 0)),
                   pl.BlockSpec((tm, LANES), lambda i, j: (i, 0))],
        out_shape=[jax.ShapeDtypeStruct((m, d), BF16), jax.ShapeDtypeStruct((m, LANES), F32)],
        scratch_shapes=[pltpu.VMEM((tm, LANES), F32)],
        compiler_params=_params(("arbitrary", "arbitrary")),
        name="mix_prenorm",
    )(x, g.reshape(1, d), mod, mod, w_dt)


def _side_cast(src, grid):
    r, c = src.shape
    for gr, gc in CAST_GRIDS:
        if r % (gr * BF16_ROWS) == 0 and c % (gc * LANES) == 0 and gr * gc <= grid[0] * grid[1]:
            break
    else:
        return None

    def index(i, j):
        t = jnp.minimum(i * grid[1] + j, gr * gc - 1)
        return (t // gc, t % gc)

    return pl.BlockSpec((r // gr, c // gc), index), jax.ShapeDtypeStruct((r, c), BF16)


def _in_proj_kernel(h_ref, w_ref, *refs):
    if len(refs) == 3:
        src_ref, o_ref, dst_ref = refs
        dst_ref[...] = src_ref[...].astype(BF16)
    else:
        o_ref, = refs
    o_ref[...] = jnp.dot(h_ref[...], w_ref[...], preferred_element_type=F32)


def _in_proj(h, w_in, n, cast_src=None):
    m, d = h.shape
    tm = _pick(m, (1024, 512, 256, 128))
    tn = _pick(n, (1024, 512, 256, 128))
    grid = (m // tm, n // tn)
    in_specs = [pl.BlockSpec((tm, d), lambda i, j: (i, 0)),
                pl.BlockSpec((d, tn), lambda i, j: (0, j))]
    out_specs = [pl.BlockSpec((tm, tn), lambda i, j: (i, j))]
    out_shape = [jax.ShapeDtypeStruct((m, n), F32)]
    args = [h, w_in]
    side = None if cast_src is None else _side_cast(cast_src, grid)
    if side is not None:
        in_specs.append(side[0])
        out_specs.append(side[0])
        out_shape.append(side[1])
        args.append(cast_src)
    outs = pl.pallas_call(
        _in_proj_kernel,
        grid=grid,
        in_specs=in_specs,
        out_specs=out_specs,
        out_shape=out_shape,
        compiler_params=_params(("arbitrary", "arbitrary")),
        name="in_proj",
    )(*args)
    if side is not None:
        return outs[0], outs[1]
    return outs[0], (None if cast_src is None else cast_src.astype(BF16))


def _rope_tile(x, cos, sin_signed, first_half):
    half = AXIS_DIM // 2
    swapped = jnp.where(first_half, pltpu.roll(x, LANES - half, axis=1), pltpu.roll(x, half, axis=1))
    return x * cos + swapped * sin_signed


def _rope_tables(l):
    rows = l // GRID_W
    row = jnp.repeat(jnp.arange(rows, dtype=F32), GRID_W)
    col = jnp.tile(jnp.arange(GRID_W, dtype=F32), rows)
    inv = 1.0 / (ROPE_BASE ** (jnp.arange(0, AXIS_DIM, 2, dtype=F32) / AXIS_DIM))

    def one(pos):
        ang = pos[:, None] * inv[None, :]
        c, s = jnp.cos(ang), jnp.sin(ang)
        return jnp.concatenate([c, c], axis=-1), jnp.concatenate([-s, s], axis=-1)

    cr, sr = one(row)
    cc, sc = one(col)
    return jnp.concatenate([cr, cc], axis=-1), jnp.concatenate([sr, sc], axis=-1)


def _attn_kernel(*refs, heads, ck, n_ctx):
    lq1_ref, lk1_ref, lq2_ref, lk2_ref, g_ref, q_ref, k_ref, v_ref = refs[:8]
    if n_ctx:
        ck_ref, cv_ref, cosq_ref, sinq_ref, cosk_ref, sink_ref, o_ref, kb_ref, vb_ref = refs[8:]
        lane = lax.broadcasted_iota(jnp.int32, (1, LANES), 1)
        first_half = (lane % AXIS_DIM) < (AXIS_DIM // 2)

        @pl.when(pl.program_id(2) == 0)
        def _():
            n_new = k_ref.shape[0]
            kb_ref[0:n_ctx, :] = ck_ref[...].astype(BF16)
            vb_ref[0:n_ctx, :] = cv_ref[...].astype(BF16)
            vb_ref[n_ctx:n_ctx + n_new, :] = v_ref[...].astype(BF16)
            step = _pick(n_new, (ROPE_ROWS,))
            for r0 in range(0, n_new, step):
                rows = slice(r0, r0 + step)
                for c0 in range(0, heads * V_HD, LANES):
                    cols = slice(c0, c0 + LANES)
                    roped = _rope_tile(k_ref[rows, cols], cosk_ref[rows, :], sink_ref[rows, :], first_half)
                    kb_ref[n_ctx + r0:n_ctx + r0 + step, cols] = roped.astype(BF16)

        k_src, v_src = kb_ref, vb_ref
    else:
        o_ref = refs[8]
        k_src, v_src = k_ref, v_ref

    lam = (jnp.exp(jnp.sum(lq1_ref[...] * lk1_ref[...], axis=-1, keepdims=True))
           - jnp.exp(jnp.sum(lq2_ref[...] * lk2_ref[...], axis=-1, keepdims=True)) + LAM_INIT)
    c2 = (QK_HD ** -0.5) * math.log2(math.e)
    nck = k_src.shape[0] // ck
    for h in range(heads):
        base = h * V_HD
        outs = []
        for half in range(2):
            cols = slice(base + half * QK_HD, base + (half + 1) * QK_HD)
            q = q_ref[:, cols]
            if n_ctx:
                q = _rope_tile(q, cosq_ref[...], sinq_ref[...], first_half)
            q = (q * c2).astype(BF16)
            m = l = acc = None
            for c in range(nck):
                rows = slice(c * ck, (c + 1) * ck)
                k = k_src[rows, cols].astype(BF16)
                s = lax.dot_general(q, k, (((1,), (1,)), ((), ())), preferred_element_type=F32)
                mc = jnp.max(s, axis=-1, keepdims=True)
                m_new = mc if c == 0 else jnp.maximum(m, mc)
                e = jnp.exp2(s - m_new)
                ls = jnp.sum(e, axis=-1, keepdims=True)
                pv = jnp.dot(e.astype(BF16), v_src[rows, base:base + V_HD].astype(BF16),
                             preferred_element_type=F32)
                if c == 0:
                    l, acc = ls, pv
                else:
                    alpha = jnp.exp2(m - m_new)
                    l = alpha * l + ls
                    acc = alpha * acc + pv
                m = m_new
            outs.append(acc * (1.0 / l))
        o = outs[0] - lam * outs[1]
        o_ref[:, base:base + V_HD] = ((_rms_rows(o) * g_ref[...]) * (1.0 - LAM_INIT)).astype(BF16)


def _attention(proj, w, lams, g_subln, heads_per_step, tq, ctx_k=None, ctx_v=None, tables=None):
    b, l, _ = proj.shape
    hw = heads_per_step * V_HD
    nh = w // hw
    n_ctx = 0 if ctx_k is None else ctx_k.shape[1]
    ck = _pick(n_ctx + l, (768, 1536, 1024, 512, 256))
    small = pl.BlockSpec((1, QK_HD), lambda bi, hi, qi: (0, 0))
    in_specs = [small, small, small, small,
                pl.BlockSpec((1, V_HD), lambda bi, hi, qi: (0, 0)),
                pl.BlockSpec((None, tq, hw), lambda bi, hi, qi: (bi, qi, hi)),
                pl.BlockSpec((None, l, hw), lambda bi, hi, qi: (bi, 0, nh + hi)),
                pl.BlockSpec((None, l, hw), lambda bi, hi, qi: (bi, 0, 2 * nh + hi))]
    args = [*lams, g_subln.reshape(1, V_HD), proj, proj, proj]
    scratch = []
    if n_ctx:
        ctx_spec = pl.BlockSpec((None, n_ctx, hw), lambda bi, hi, qi: (bi, 0, hi))
        q_tab = pl.BlockSpec((tq, LANES), lambda bi, hi, qi: (qi, 0))
        k_tab = pl.BlockSpec((l, LANES), lambda bi, hi, qi: (0, 0))
        in_specs += [ctx_spec, ctx_spec, q_tab, q_tab, k_tab, k_tab]
        args += [ctx_k, ctx_v, tables[0], tables[1], tables[0], tables[1]]
        scratch = [pltpu.VMEM((n_ctx + l, hw), BF16), pltpu.VMEM((n_ctx + l, hw), BF16)]
    return pl.pallas_call(
        functools.partial(_attn_kernel, heads=heads_per_step, ck=ck, n_ctx=n_ctx),
        grid=(b, nh, l // tq),
        in_specs=in_specs,
        out_specs=pl.BlockSpec((None, tq, hw), lambda bi, hi, qi: (bi, qi, hi)),
        out_shape=jax.ShapeDtypeStruct((b, l, w), BF16),
        scratch_shapes=scratch,
        compiler_params=_params(("arbitrary", "arbitrary", "arbitrary")),
        name="diff_attn",
    )(*args)


def _conv_silu(win_ref, prev_ref, main_ref, next_ref, w_ref, b_ref, has_prev, has_next):
    n = main_ref.shape[0]
    win_ref[0:CONV_HALO, :] = jnp.where(has_prev, prev_ref[...], 0.0)
    win_ref[CONV_HALO:CONV_HALO + n, :] = main_ref[...]
    win_ref[CONV_HALO + n:CONV_HALO + n + CONV_HALO, :] = jnp.where(has_next, next_ref[...], 0.0)
    first = CONV_HALO - CONV_K // 2
    out = b_ref[...]
    for j in range(CONV_K):
        out = out + w_ref[j:j + 1, :] * win_ref[first + j:first + j + n, :]
    return _silu(out)


def _decay_terms(dt_raw, dt_bias, a_log, heads):
    dt = _softplus(dt_raw + dt_bias)
    a = dt * (-jnp.exp(a_log))
    row = lax.broadcasted_iota(jnp.int32, (CHUNK, LANES), 0)
    cs = a
    step = 1
    while step < CHUNK:
        cs = cs + jnp.where(row >= step, pltpu.roll(cs, step, axis=0), 0.0)
        step *= 2
    total = cs[CHUNK - 1:CHUNK, :]
    lane = lax.broadcasted_iota(jnp.int32, (CHUNK, LANES), 1)
    cum = jnp.where(lane < heads, cs, total - cs + a)
    cum_t = cum.T
    dt_t = dt.T
    tot_t = jnp.broadcast_to(total, (CHUNK, LANES)).T
    log2e = math.log2(math.e)
    return (cum * log2e,
            cum_t * log2e,
            dt_t,
            jnp.exp(tot_t - cum_t) * dt_t,
            jnp.exp(cum_t),
            jnp.exp(tot_t))


def _conv_kernel(xs_p, xs_m, xs_n, bc_p, bc_m, bc_n, cwx_ref, cbx_ref, cwbc_ref, cbbc_ref,
                 xo_ref, bco_ref, winx_ref, winbc_ref, *, nt):
    i = pl.program_id(1)
    has_prev = i > 0
    has_next = i < nt - 1
    xo_ref[...] = _conv_silu(winx_ref, xs_p, xs_m, xs_n, cwx_ref, cbx_ref, has_prev, has_next)
    bco_ref[...] = _conv_silu(winbc_ref, bc_p, bc_m, bc_n, cwbc_ref, cbbc_ref, has_prev, has_next).astype(BF16)


def _conv(proj, p, dx, nbc, xc, bcc):
    b, l, _ = proj.shape
    tr = _pick(l, (256, 128))
    nt = l // tr
    hpt = tr // CONV_HALO
    nhalo = l // CONV_HALO

    def main(wd, col):
        return pl.BlockSpec((None, tr, wd), lambda bi, i: (bi, i, col))

    def halo_prev(wd, col):
        return pl.BlockSpec((None, CONV_HALO, wd), lambda bi, i: (bi, jnp.maximum(i * hpt - 1, 0), col))

    def halo_next(wd, col):
        return pl.BlockSpec((None, CONV_HALO, wd),
                            lambda bi, i: (bi, jnp.minimum((i + 1) * hpt, nhalo - 1), col))

    def full(shape):
        return pl.BlockSpec(shape, lambda bi, i: (0,) * len(shape))

    return pl.pallas_call(
        functools.partial(_conv_kernel, nt=nt),
        grid=(b, nt),
        in_specs=[halo_prev(dx, xc), main(dx, xc), halo_next(dx, xc),
                  halo_prev(nbc, bcc), main(nbc, bcc), halo_next(nbc, bcc),
                  full((CONV_K, dx)), full((1, dx)), full((CONV_K, nbc)), full((1, nbc))],
        out_specs=[pl.BlockSpec((None, tr, dx), lambda bi, i: (bi, i, 0)),
                   pl.BlockSpec((None, tr, nbc), lambda bi, i: (bi, i, 0))],
        out_shape=[jax.ShapeDtypeStruct((b, l, dx), F32), jax.ShapeDtypeStruct((b, l, nbc), BF16)],
        scratch_shapes=[pltpu.VMEM((tr + 2 * CONV_HALO, dx), F32),
                        pltpu.VMEM((tr + 2 * CONV_HALO, nbc), F32)],
        compiler_params=_params(("arbitrary", "arbitrary")),
        name="conv_silu",
    )(proj, proj, proj, proj, proj, proj, p['conv_w_x'], p['conv_b_x'], p['conv_w_bc'], p['conv_b_bc'])


def _ssd_kernel(*refs, nc, heads, has_init):
    xs_ref, bc_ref, dt_ref, z_ref, alog_ref, dtb_ref, dskip_ref, gn_ref = refs[:8]
    idx = 8
    if has_init:
        initf_ref, initb_ref = refs[idx], refs[idx + 1]
        idx += 2
    y_ref, sf_ref, sb_ref, carf_ref, carb_ref, prev_ref = refs[idx:idx + 6]

    phase = pl.program_id(1)
    i = pl.program_id(2)
    c = jnp.where(phase == 0, i, nc - 1 - i)
    gw = heads // N_GROUPS * SSM_HD
    hpg = heads // N_GROUPS

    @pl.when((phase == 0) & (i == 0))
    def _():
        if has_init:
            carf_ref[...] = initf_ref[...]
            carb_ref[...] = initb_ref[...]
        else:
            carf_ref[...] = jnp.zeros_like(carf_ref)
            carb_ref[...] = jnp.zeros_like(carb_ref)

    row = lax.broadcasted_iota(jnp.int32, (CHUNK, CHUNK), 0)
    colm = lax.broadcasted_iota(jnp.int32, (CHUNK, CHUNK), 1)
    cum2, cum2_t, dt_t, w_t, e_t, dec_t = _decay_terms(dt_ref[...], dtb_ref[...], alog_ref[...], heads)

    xs = xs_ref[...]
    bc = bc_ref[...]
    x_t = xs.T

    def state_update(car_ref, off):
        for g in range(N_GROUPS):
            b_g = bc[:, g * D_STATE:(g + 1) * D_STATE].astype(BF16)
            rows = []
            for hh in range(hpg):
                h = g * hpg + hh
                rows.append(x_t[h * SSM_HD:(h + 1) * SSM_HD, :] * w_t[off + h:off + h + 1, :])
            xw = jnp.concatenate(rows, axis=0).astype(BF16)
            st = jnp.dot(xw, b_g, preferred_element_type=F32)
            for hh in range(hpg):
                h = g * hpg + hh
                r = slice(h * SSM_HD, (h + 1) * SSM_HD)
                car_ref[r, :] = car_ref[r, :] * dec_t[off + h:off + h + 1, :] + st[hh * SSM_HD:(hh + 1) * SSM_HD, :]

    @pl.when(phase == 0)
    def _():
        prev_ref[c] = carf_ref[...].astype(BF16)
        state_update(carf_ref, 0)

        @pl.when(i == nc - 1)
        def _():
            sf_ref[...] = carf_ref[...]

    @pl.when(phase == 1)
    def _():
        causal = colm >= row
        anti = colm <= row
        y_parts = []
        for g in range(N_GROUPS):
            b_g = bc[:, g * D_STATE:(g + 1) * D_STATE].astype(BF16)
            c_g = bc[:, (N_GROUPS + g) * D_STATE:(N_GROUPS + g + 1) * D_STATE].astype(BF16)
            cb_t = lax.dot_general(b_g, c_g, (((1,), (1,)), ((), ())), preferred_element_type=F32)
            rg = slice(g * gw, (g + 1) * gw)
            yo_f = lax.dot_general(prev_ref[c, rg, :], c_g, (((1,), (1,)), ((), ())),
                                   preferred_element_type=F32)
            yo_b = lax.dot_general(carb_ref[rg, :].astype(BF16), c_g, (((1,), (1,)), ((), ())),
                                   preferred_element_type=F32)
            for hh in range(hpg):
                h = g * hpg + hh
                hb = heads + h
                x_h = x_t[h * SSM_HD:(h + 1) * SSM_HD, :]
                seg_f = cum2_t[h:h + 1, :] - cum2[:, h:h + 1]
                seg_b = cum2_t[hb:hb + 1, :] - cum2[:, hb:hb + 1]
                w_f = cb_t * jnp.exp2(jnp.where(causal, seg_f, -1e4))
                w_b = cb_t * jnp.exp2(jnp.where(anti, seg_b, -1e4))
                lhs = jnp.concatenate([x_h * dt_t[h:h + 1, :], x_h * dt_t[hb:hb + 1, :]], axis=1)
                rhs = jnp.concatenate([w_f, w_b], axis=0)
                yd = jnp.dot(lhs.astype(BF16), rhs.astype(BF16), preferred_element_type=F32)
                r = slice(hh * SSM_HD, (hh + 1) * SSM_HD)
                y_parts.append(yd + yo_f[r, :] * e_t[h:h + 1, :] + yo_b[r, :] * e_t[hb:hb + 1, :])
        y = jnp.concatenate(y_parts, axis=0).T
        y = y + dskip_ref[...] * xs
        y = y * _silu(z_ref[...])
        for g in range(N_GROUPS):
            cols = slice(g * gw, (g + 1) * gw)
            y_ref[:, cols] = (_rms_rows(y[:, cols]) * gn_ref[:, cols]).astype(BF16)
        state_update(carb_ref, heads)

        @pl.when(i == nc - 1)
        def _():
            sb_ref[...] = carb_ref[...]


def _ssd(xs, bc, dt, proj, zc, p, init_f, init_b):
    b, l, dx = xs.shape
    nbc = bc.shape[2]
    heads = dx // SSM_HD
    nc = l // CHUNK
    has_init = init_f is not None

    def chunk_of(ph, i):
        return jnp.where(ph == 0, i, nc - 1 - i)

    def main(wd, col):
        return pl.BlockSpec((None, CHUNK, wd), lambda bi, ph, i: (bi, chunk_of(ph, i), col))

    def full(shape):
        return pl.BlockSpec(shape, lambda bi, ph, i: (0,) * len(shape))

    state_spec = pl.BlockSpec((None, dx, D_STATE), lambda bi, ph, i: (bi, 0, 0))
    in_specs = [main(dx, 0), main(nbc, 0), main(LANES, 0), main(dx, zc),
                full((1, LANES)), full((1, LANES)), full((1, dx)), full((1, dx))]
    args = [xs, bc, dt, proj, p['a_log'], p['dt_bias'], p['d_skip'], p['g_ssm_norm']]
    if has_init:
        in_specs += [state_spec, state_spec]
        args += [init_f, init_b]
    y_spec = pl.BlockSpec((None, CHUNK, dx), lambda bi, ph, i: (bi, jnp.where(ph == 0, nc - 1, nc - 1 - i), 0))
    return pl.pallas_call(
        functools.partial(_ssd_kernel, nc=nc, heads=heads, has_init=has_init),
        grid=(b, 2, nc),
        in_specs=in_specs,
        out_specs=[y_spec, state_spec, state_spec],
        out_shape=[jax.ShapeDtypeStruct((b, l, dx), BF16),
                   jax.ShapeDtypeStruct((b, dx, D_STATE), F32),
                   jax.ShapeDtypeStruct((b, dx, D_STATE), F32)],
        scratch_shapes=[pltpu.VMEM((dx, D_STATE), F32), pltpu.VMEM((dx, D_STATE), F32),
                        pltpu.VMEM((nc, dx, D_STATE), BF16)],
        compiler_params=_params(("arbitrary", "arbitrary", "arbitrary")),
        name="ssd_scan",
    )(*args)


def _out_proj_kernel(a_ref, y_ref, wa_ref, wy_ref, o_ref):
    o_ref[...] = (jnp.dot(a_ref[...], wa_ref[...], preferred_element_type=F32)
                  + jnp.dot(y_ref[...], wy_ref[...], preferred_element_type=F32))


def _out_proj(att, y, w_out):
    m, wa = att.shape
    d = w_out.shape[1]
    assert y.shape[1] == wa and w_out.shape[0] == 2 * wa
    tm = _pick(m, (1024, 512, 256, 128))
    tn = _pick(d, (1024, 512, 256, 128))
    return pl.pallas_call(
        _out_proj_kernel,
        grid=(m // tm, d // tn),
        in_specs=[pl.BlockSpec((tm, wa), lambda i, j: (i, 0)),
                  pl.BlockSpec((tm, wa), lambda i, j: (i, 0)),
                  pl.BlockSpec((wa, tn), lambda i, j: (0, j)),
                  pl.BlockSpec((wa, tn), lambda i, j: (1, j))],
        out_specs=pl.BlockSpec((tm, tn), lambda i, j: (i, j)),
        out_shape=jax.ShapeDtypeStruct((m, d), F32),
        compiler_params=_params(("arbitrary", "arbitrary")),
        name="out_proj",
    )(att, y, w_out, w_out)


def _sandwich_kernel(mix_ref, x_ref, gpost_ref, gate_ref, gpre_ref, sc_ref, sh_ref,
                     x1_ref, h2_ref, stat_ref, stat2_ref):
    chunks = _col_chunks(x1_ref.shape[1])
    _store_inv_rms(mix_ref, stat_ref)

    def residual(r, carry):
        rows = _rows(r, BF16_ROWS)
        inv = stat_ref[rows, 0:1]
        for cols in chunks:
            n = (mix_ref[rows, cols] * inv) * gpost_ref[:, cols]
            x1_ref[rows, cols] = x_ref[rows, cols] + gate_ref[:, cols] * n
        return carry

    lax.fori_loop(0, x1_ref.shape[0] // BF16_ROWS, residual, 0, unroll=2)
    _store_inv_rms(x1_ref, stat2_ref)

    def prenorm(r, carry):
        rows = _rows(r, BF16_ROWS)
        inv = stat2_ref[rows, 0:1]
        for cols in chunks:
            y = (x1_ref[rows, cols] * inv) * gpre_ref[:, cols]
            h2_ref[rows, cols] = (y * (1.0 + sc_ref[:, cols]) + sh_ref[:, cols]).astype(BF16)
        return carry

    lax.fori_loop(0, x1_ref.shape[0] // BF16_ROWS, prenorm, 0, unroll=2)


def _sandwich(mix, x, mod, g_post, g_pre, mod_rows):
    m, d = x.shape
    tm = _row_tile(m, mod_rows, (256, 128))
    vec = pl.BlockSpec((1, d), lambda i, j: (0, 0))
    tile = pl.BlockSpec((tm, d), lambda i, j: (i, 0))
    return pl.pallas_call(
        _sandwich_kernel,
        grid=(m // tm, 1),
        in_specs=[tile, tile, vec, _mod_spec(d, 2, mod_rows, tm), vec,
                  _mod_spec(d, 4, mod_rows, tm), _mod_spec(d, 3, mod_rows, tm)],
        out_specs=[tile, tile],
        out_shape=[jax.ShapeDtypeStruct((m, d), F32), jax.ShapeDtypeStruct((m, d), BF16)],
        scratch_shapes=[pltpu.VMEM((tm, LANES), F32), pltpu.VMEM((tm, LANES), F32)],
        compiler_params=_params(("arbitrary", "arbitrary")),
        name="mix_sandwich",
    )(mix, x, g_post.reshape(1, d), mod, g_pre.reshape(1, d), mod, mod)


def _sandwich_rows(mix_ref, x_ref, gpost_ref, gate_ref, gpre_ref, sc_ref, sh_ref, x1_ref, h2_ref):
    for r0 in range(0, x_ref.shape[0], BF16_ROWS):
        hs = []
        for r in range(r0, r0 + BF16_ROWS, NORM_ROWS):
            rows = slice(r, r + NORM_ROWS)
            x1 = x_ref[rows, :] + gate_ref[...] * (_rms_rows(mix_ref[rows, :]) * gpost_ref[...])
            x1_ref[rows, :] = x1
            hs.append((_rms_rows(x1) * gpre_ref[...]) * (1.0 + sc_ref[...]) + sh_ref[...])
        h2_ref[r0:r0 + BF16_ROWS, :] = jnp.concatenate(hs, axis=0).astype(BF16)


def _up_kernel(h_ref, w_ref, *refs, cast, hosted):
    n_in = int(cast) + 7 * int(hosted)
    side_in, o_ref, side_out = refs[:n_in], refs[n_in], refs[n_in + 1:]
    if cast:
        side_out[0][...] = side_in[0][...].astype(BF16)
    if hosted:
        _sandwich_rows(*side_in[-7:], *side_out[-2:])
    u = jnp.maximum(jnp.dot(h_ref[...], w_ref[...], preferred_element_type=F32), 0.0)
    o_ref[...] = (u * u).astype(BF16)


def _up(h2, w_up, cast_src=None, host=None):
    m, d = h2.shape
    f = w_up.shape[1]
    tm = _pick(m, (1024, 512, 256, 128))
    tn = _pick(f, (1024, 512, 256, 128))
    nb = 0
    if host is not None:
        mix, x, mod, g_post, g_pre, (row0, per) = host
        nb = x.shape[0] // HOST_ROWS
        tn = next((t for t in (1024, 512, 256) if f % t == 0 and (m // tm) * (f // t) >= nb), None)
        if tn is None or x.shape[0] % HOST_ROWS or per % HOST_ROWS:
            return _up(h2, w_up, cast_src) + (None,)
    grid = (m // tm, f // tn)
    in_specs = [pl.BlockSpec((tm, d), lambda i, j: (i, 0)),
                pl.BlockSpec((d, tn), lambda i, j: (0, j))]
    out_specs = [pl.BlockSpec((tm, tn), lambda i, j: (i, j))]
    out_shape = [jax.ShapeDtypeStruct((m, f), BF16)]
    args = [h2, w_up]
    side = None if cast_src is None else _side_cast(cast_src, grid)
    if side is not None:
        in_specs.append(side[0])
        out_specs.append(side[0])
        out_shape.append(side[1])
        args.append(cast_src)
    if host is not None:
        def blk(i, j):
            return jnp.minimum(i * grid[1] + j, nb - 1)

        rows = pl.BlockSpec((HOST_ROWS, d), lambda i, j: (blk(i, j), 0))
        vec = pl.BlockSpec((1, d), lambda i, j: (0, 0))

        def mvec(which):
            return pl.BlockSpec((None, None, 1, d),
                                lambda i, j: (row0 + (blk(i, j) * HOST_ROWS) // per, which, 0, 0))

        in_specs += [rows, rows, vec, mvec(2), vec, mvec(4), mvec(3)]
        args += [mix, x, g_post.reshape(1, d), mod, g_pre.reshape(1, d), mod, mod]
        out_specs += [rows, rows]
        out_shape += [jax.ShapeDtypeStruct(x.shape, F32), jax.ShapeDtypeStruct(x.shape, BF16)]
    outs = pl.pallas_call(
        functools.partial(_up_kernel, cast=side is not None, hosted=host is not None),
        grid=grid,
        in_specs=in_specs,
        out_specs=out_specs,
        out_shape=out_shape,
        compiler_params=_params(("arbitrary", "arbitrary")),
        name="mlp_up",
    )(*args)
    cast_out = outs[1] if side is not None else (None if cast_src is None else cast_src.astype(BF16))
    if host is None:
        return outs[0], cast_out
    return outs[0], cast_out, (outs[-2], outs[-1])


def _down_kernel(u_ref, w_ref, x_ref, gpost_ref, gate_ref, o_ref, stat_ref, *, nk):
    k = pl.program_id(1)

    @pl.when(k == 0)
    def _():
        o_ref[...] = jnp.dot(u_ref[...], w_ref[...], preferred_element_type=F32)

    @pl.when(k > 0)
    def _():
        o_ref[...] += jnp.dot(u_ref[...], w_ref[...], preferred_element_type=F32)

    @pl.when(k == nk - 1)
    def _():
        _store_inv_rms(o_ref, stat_ref)

        def body(r, carry):
            rows = _rows(r, BF16_ROWS)
            inv = stat_ref[rows, 0:1]
            for cols in _col_chunks(o_ref.shape[1]):
                n = (o_ref[rows, cols] * inv) * gpost_ref[:, cols]
                o_ref[rows, cols] = x_ref[rows, cols] + gate_ref[:, cols] * n
            return carry

        lax.fori_loop(0, o_ref.shape[0] // BF16_ROWS, body, 0, unroll=2)


def _down(u, w_down, x1, mod, g_post, mod_rows):
    m, f = u.shape
    d = x1.shape[1]
    tm = _row_tile(m, mod_rows)
    tk = _pick(f, (1024, 512, 256, 128))
    nk = f // tk
    return pl.pallas_call(
        functools.partial(_down_kernel, nk=nk),
        grid=(m // tm, nk),
        in_specs=[pl.BlockSpec((tm, tk), lambda i, k: (i, k)),
                  pl.BlockSpec((tk, d), lambda i, k: (k, 0)),
                  pl.BlockSpec((tm, d), lambda i, k: (i, 0)),
                  pl.BlockSpec((1, d), lambda i, k: (0, 0)),
                  _mod_spec(d, 5, mod_rows, tm)],
        out_specs=pl.BlockSpec((tm, d), lambda i, k: (i, 0)),
        out_shape=jax.ShapeDtypeStruct((m, d), F32),
        scratch_shapes=[pltpu.VMEM((tm, LANES), F32)],
        compiler_params=_params(("arbitrary", "arbitrary")),
        name="mlp_down",
    )(u, w_down, x1, g_post.reshape(1, d), mod)


def _kv_out_kernel(ki_ref, vi_ref, k_ref, v_ref, *, n_att):
    for hh in range(n_att):
        for j in range(2):
            c0 = (hh * 2 + j) * QK_HD
            k_ref[:, hh, j, :] = ki_ref[:, c0:c0 + QK_HD]
        v_ref[:, hh, :] = vi_ref[:, hh * V_HD:(hh + 1) * V_HD]


def _kv_out(proj3, att_w):
    b, l, _ = proj3.shape
    n_att = att_w // V_HD
    tl = _pick(l, (256, 128))
    return pl.pallas_call(
        functools.partial(_kv_out_kernel, n_att=n_att),
        grid=(b, l // tl),
        in_specs=[pl.BlockSpec((None, tl, att_w), lambda bi, i: (bi, i, 1)),
                  pl.BlockSpec((None, tl, att_w), lambda bi, i: (bi, i, 2))],
        out_specs=[pl.BlockSpec((None, None, tl, n_att, 2, QK_HD), lambda bi, i: (bi, 0, i, 0, 0, 0)),
                   pl.BlockSpec((None, None, tl, n_att, V_HD), lambda bi, i: (bi, 0, i, 0, 0))],
        out_shape=[jax.ShapeDtypeStruct((b, 1, l, n_att, 2, QK_HD), F32),
                   jax.ShapeDtypeStruct((b, 1, l, n_att, V_HD), F32)],
        compiler_params=_params(("arbitrary", "arbitrary")),
        name="kv_out",
    )(proj3, proj3)


def _trunk(x, mod, mod_row0, rows_per_mod, w, ctx_k, ctx_v, init_f, init_b):
    b, l, d = x.shape
    m = b * l
    att_w = d // 2
    d_inner = d - att_w
    n_bc = 2 * N_GROUPS * D_STATE
    latent = ctx_k is not None

    mod_rows = (mod_row0, rows_per_mod)
    x2d = x.reshape(m, d)
    assert att_w == d_inner and (3 * att_w + 2 * d_inner) % n_bc == 0
    h, dt = _prenorm(x2d, mod, w['g_mix_pre'], w['w_in_dt'], mod_rows)
    w_up, w_down = w['w_up'], w['w_down']
    proj, w_up_cast = _in_proj(h, w['w_in'], 3 * att_w + 2 * d_inner + n_bc,
                               w_up if w_up.dtype == F32 else None)
    w_up = w_up if w_up_cast is None else w_up_cast
    proj3 = proj.reshape(b, l, -1)
    lams = (w['lq1'], w['lk1'], w['lq2'], w['lk2'])
    if latent:
        att = _attention(proj3, att_w, lams, w['g_subln'], 1, _pick(l, (512, 256, 128)),
                         ctx_k, ctx_v, _rope_tables(l))
    else:
        att = _attention(proj3, att_w, lams, w['g_subln'], att_w // V_HD, l)
    xs, bc = _conv(proj3, w, d_inner, n_bc, 4, (3 * att_w + 2 * d_inner) // n_bc)
    y, fin_f, fin_b = _ssd(xs, bc, dt.reshape(b, l, LANES), proj3, 3, w, init_f, init_b)
    mix = _out_proj(att.reshape(m, att_w), y.reshape(m, d_inner), w['w_out'])
    k_c, v_c = (None, None) if latent else _kv_out(proj3, att_w)
    return dict(mix=mix, x2d=x2d, mod_rows=mod_rows, shape=(b, l, d), k=k_c, v=v_c, fin_f=fin_f, fin_b=fin_b,
                w=dict(w, w_up=w_up, w_down=w_down))


def kernel(x_prompt, x_sample, c, cache_k, cache_v, state_ssm_fwd, state_ssm_bwd, c_ctx, w_ada, b_ada,
           g_mix_pre, g_mix_post, g_mlp_pre, g_mlp_post, w_in, lambda_q1, lambda_k1, lambda_q2, lambda_k2,
           g_subln, conv_w, conv_b, a_log, dt_bias, d_skip, g_ssm_norm, w_out, w_up, w_down):
    bp, lp, d = x_prompt.shape
    bd, ld, _ = x_sample.shape
    depth = w_in.shape[0]
    assert depth == 1 and bd + 1 <= MOD_ROWS
    att_w = d // 2
    d_inner = d - att_w
    heads = d_inner // SSM_HD
    n_att = att_w // V_HD
    n_main = 3 * att_w + 2 * d_inner + 2 * N_GROUPS * D_STATE
    assert 2 * heads <= LANES and w_in.shape[2] == n_main + 2 * heads

    lyr = 0
    pad_l = LANES - 2 * heads
    w = dict(
        g_mix_pre=g_mix_pre[lyr], g_mix_post=g_mix_post[lyr], g_mlp_pre=g_mlp_pre[lyr],
        g_mlp_post=g_mlp_post[lyr],
        w_in=w_in[lyr].astype(BF16),
        w_in_dt=jnp.pad(w_in[lyr, :, n_main:], ((0, 0), (0, pad_l))).astype(BF16),
        lq1=lambda_q1[lyr].reshape(1, QK_HD), lk1=lambda_k1[lyr].reshape(1, QK_HD),
        lq2=lambda_q2[lyr].reshape(1, QK_HD), lk2=lambda_k2[lyr].reshape(1, QK_HD),
        g_subln=g_subln[lyr],
        conv_w_x=conv_w[lyr, :, :d_inner], conv_w_bc=conv_w[lyr, :, d_inner:],
        conv_b_x=conv_b[lyr, :d_inner].reshape(1, -1), conv_b_bc=conv_b[lyr, d_inner:].reshape(1, -1),
        a_log=jnp.pad(a_log[lyr].reshape(1, -1), ((0, 0), (0, pad_l))),
        dt_bias=jnp.pad(dt_bias[lyr].reshape(1, -1), ((0, 0), (0, pad_l))),
        d_skip=jnp.repeat(d_skip[lyr], SSM_HD).reshape(1, -1),
        g_ssm_norm=g_ssm_norm[lyr].reshape(1, -1),
        w_out=w_out[lyr].astype(BF16), w_up=w_up[lyr], w_down=w_down[lyr],
    )

    cvec = jnp.concatenate([c_ctx[None, :], c, jnp.zeros((MOD_ROWS - 1 - bd, d), F32)], axis=0)
    mod = _ada(cvec, w_ada[lyr], b_ada[lyr]).reshape(MOD_ROWS, 6, 1, d)

    s = _trunk(
        x_sample, mod, 1, ld, w,
        cache_k[:, lyr].reshape(bd, -1, att_w), cache_v[:, lyr].reshape(bd, -1, att_w),
        state_ssm_fwd[:, lyr].reshape(bd, d_inner, D_STATE), state_ssm_bwd[:, lyr].reshape(bd, d_inner, D_STATE))
    p = _trunk(x_prompt, mod, 0, bp * lp, s['w'], None, None, None, None)
    w = p['w']
    w_up, w_down = w['w_up'], w['w_down']
    cast_down = w_down if w_down.dtype == F32 else None

    x1_p, h2_p = _sandwich(p['mix'], p['x2d'], mod, w['g_mix_post'], w['g_mlp_pre'], p['mod_rows'])
    u_p, w_down_cast, hosted = _up(h2_p, w_up, None,
                                   (s['mix'], s['x2d'], mod, w['g_mix_post'], w['g_mlp_pre'], s['mod_rows']))
    x1_s, h2_s = hosted or _sandwich(s['mix'], s['x2d'], mod, w['g_mix_post'], w['g_mlp_pre'], s['mod_rows'])
    u_s, w_down_cast = _up(h2_s, w_up, cast_down)
    w_down = w_down if w_down_cast is None else w_down_cast
    ys = _down(u_s, w_down, x1_s, mod, w['g_mlp_post'], s['mod_rows']).reshape(s['shape'])
    yp = _down(u_p, w_down, x1_p, mod, w['g_mlp_post'], p['mod_rows']).reshape(p['shape'])

    return (yp, ys, p['k'], p['v'],
            p['fin_f'].reshape(bp, 1, heads, SSM_HD, D_STATE), p['fin_b'].reshape(bp, 1, heads, SSM_HD, D_STATE))
```

```python
import functools
import math

import jax
import jax.numpy as jnp
from jax import lax
from jax.experimental import pallas as pl
from jax.experimental.pallas import tpu as pltpu

F32 = jnp.float32
BF16 = jnp.bfloat16

GRID_W = 64
V_HD = 256
QK_HD = V_HD // 2
AXIS_DIM = QK_HD // 2
ROPE_BASE = 10000.0
SSM_HD = 64
N_GROUPS = 4
D_STATE = 128
CONV_K = 5
CHUNK = 128
RMS_EPS = 1e-6
LAM_INIT = 0.8 - 0.6 * math.exp(-0.3 * 0)

V7X_VMEM_BYTES = 64 * 1024 * 1024
VMEM_LIMIT = V7X_VMEM_BYTES - 8 * 1024 * 1024
LANES = 128
SUBLANES = 8
CONV_HALO = SUBLANES
NORM_ROWS = SUBLANES
BF16_ROWS = 2 * SUBLANES
STAT_UNROLL = 16
APPLY_COLS = 1024
ROPE_ROWS = 512
CAST_GRIDS = ((16, 8), (8, 8), (4, 4), (2, 2))
MOD_ROWS = 8


def _pick(n, prefs):
    for p in prefs:
        if n % p == 0:
            return p
    return n


def _params(sem):
    return pltpu.CompilerParams(dimension_semantics=sem, vmem_limit_bytes=VMEM_LIMIT)


def _sigmoid(x):
    return 1.0 / (1.0 + jnp.exp(-x))


def _silu(x):
    return x * _sigmoid(x)


def _softplus(x):
    return jnp.maximum(x, 0.0) + jnp.log1p(jnp.exp(-jnp.abs(x)))


def _rms_rows(x, eps=RMS_EPS):
    return x * lax.rsqrt(jnp.mean(x * x, axis=-1, keepdims=True) + eps)


def _inv_rms(x, eps=RMS_EPS):
    return jnp.broadcast_to(lax.rsqrt(jnp.mean(x * x, axis=-1, keepdims=True) + eps), (x.shape[0], LANES))


def _rows(r, n):
    return pl.ds(pl.multiple_of(r * n, n), n)


def _store_inv_rms(src_ref, stat_ref):
    def body(r, carry):
        rows = _rows(r, NORM_ROWS)
        stat_ref[rows, :] = _inv_rms(src_ref[rows, :])
        return carry

    lax.fori_loop(0, src_ref.shape[0] // NORM_ROWS, body, 0, unroll=STAT_UNROLL)


def _col_chunks(d):
    w = _pick(d, (APPLY_COLS,))
    return [slice(c * w, (c + 1) * w) for c in range(d // w)]


def _ada_kernel(c_ref, w_ref, b_ref, o_ref):
    c = c_ref[...]
    s = _silu(c).astype(BF16)
    o_ref[...] = jnp.dot(s, w_ref[...].astype(BF16), preferred_element_type=F32) + b_ref[...]


def _ada(cvec, w_ada, b_ada):
    rows, d = cvec.shape
    n = w_ada.shape[1]
    tn = _pick(n, (512, 256, 128))
    return pl.pallas_call(
        _ada_kernel,
        grid=(n // tn,),
        in_specs=[pl.BlockSpec((rows, d), lambda j: (0, 0)),
                  pl.BlockSpec((d, tn), lambda j: (0, j)),
                  pl.BlockSpec((1, tn), lambda j: (0, j))],
        out_specs=pl.BlockSpec((rows, tn), lambda j: (0, j)),
        out_shape=jax.ShapeDtypeStruct((rows, n), F32),
        compiler_params=_params(("arbitrary",)),
        name="ada_mod",
    )(cvec, w_ada, b_ada.reshape(1, n))


def _row_tile(m, mod_rows, prefs=(512, 256, 128)):
    return _pick(math.gcd(m, mod_rows[1]), prefs)


def _mod_spec(d, which, mod_rows, tm):
    row0, per = mod_rows
    return pl.BlockSpec((None, None, 1, d), lambda i, j: (row0 + (i * tm) // per, which, 0, 0))


def _prenorm_kernel(x_ref, g_ref, sc_ref, sh_ref, wdt_ref, h_ref, dt_ref, stat_ref):
    _store_inv_rms(x_ref, stat_ref)

    def body(r, carry):
        rows = _rows(r, BF16_ROWS)
        inv = stat_ref[rows, 0:1]
        for cols in _col_chunks(x_ref.shape[1]):
            y = (x_ref[rows, cols] * inv) * g_ref[:, cols]
            h_ref[rows, cols] = (y * (1.0 + sc_ref[:, cols]) + sh_ref[:, cols]).astype(BF16)
        return carry

    lax.fori_loop(0, x_ref.shape[0] // BF16_ROWS, body, 0, unroll=2)
    dt_ref[...] = jnp.dot(h_ref[...], wdt_ref[...], preferred_element_type=F32)


def _prenorm(x, mod, g, w_dt, mod_rows):
    m, d = x.shape
    tm = _row_tile(m, mod_rows, (256, 128))
    return pl.pallas_call(
        _prenorm_kernel,
        grid=(m // tm, 1),
        in_specs=[pl.BlockSpec((tm, d), lambda i, j: (i, 0)),
                  pl.BlockSpec((1, d), lambda i, j: (0, 0)),
                  _mod_spec(d, 1, mod_rows, tm),
                  _mod_spec(d, 0, mod_rows, tm),
                  pl.BlockSpec((d, LANES), lambda i, j: (0, 0))],
        out_specs=[pl.BlockSpec((tm, d), lambda i, j: (i, 0)),
                   pl.BlockSpec((tm, LANES), lambda i, j: (i, 0))],
        out_shape=[jax.ShapeDtypeStruct((m, d), BF16), jax.ShapeDtypeStruct((m, LANES), F32)],
        scratch_shapes=[pltpu.VMEM((tm, LANES), F32)],
        compiler_params=_params(("arbitrary", "arbitrary")),
        name="mix_prenorm",
    )(x, g.reshape(1, d), mod, mod, w_dt)


def _side_cast(src, grid):
    r, c = src.shape
    for gr, gc in CAST_GRIDS:
        if r % (gr * BF16_ROWS) == 0 and c % (gc * LANES) == 0 and gr * gc <= grid[0] * grid[1]:
            break
    else:
        return None

    def index(i, j):
        t = jnp.minimum(i * grid[1] + j, gr * gc - 1)
        return (t // gc, t % gc)

    return pl.BlockSpec((r // gr, c // gc), index), jax.ShapeDtypeStruct((r, c), BF16)


def _in_proj_kernel(h_ref, w_ref, *refs):
    if len(refs) == 3:
        src_ref, o_ref, dst_ref = refs
        dst_ref[...] = src_ref[...].astype(BF16)
    else:
        o_ref, = refs
    o_ref[...] = jnp.dot(h_ref[...], w_ref[...], preferred_element_type=F32)


def _in_proj(h, w_in, n, cast_src=None):
    m, d = h.shape
    tm = _pick(m, (1024, 512, 256, 128))
    tn = _pick(n, (1024, 512, 256, 128))
    grid = (m // tm, n // tn)
    in_specs = [pl.BlockSpec((tm, d), lambda i, j: (i, 0)),
                pl.BlockSpec((d, tn), lambda i, j: (0, j))]
    out_specs = [pl.BlockSpec((tm, tn), lambda i, j: (i, j))]
    out_shape = [jax.ShapeDtypeStruct((m, n), F32)]
    args = [h, w_in]
    side = None if cast_src is None else _side_cast(cast_src, grid)
    if side is not None:
        in_specs.append(side[0])
        out_specs.append(side[0])
        out_shape.append(side[1])
        args.append(cast_src)
    outs = pl.pallas_call(
        _in_proj_kernel,
        grid=grid,
        in_specs=in_specs,
        out_specs=out_specs,
        out_shape=out_shape,
        compiler_params=_params(("arbitrary", "arbitrary")),
        name="in_proj",
    )(*args)
    if side is not None:
        return outs[0], outs[1]
    return outs[0], (None if cast_src is None else cast_src.astype(BF16))


def _rope_tile(x, cos, sin_signed, first_half):
    half = AXIS_DIM // 2
    swapped = jnp.where(first_half, pltpu.roll(x, LANES - half, axis=1), pltpu.roll(x, half, axis=1))
    return x * cos + swapped * sin_signed


def _rope_tables(l):
    rows = l // GRID_W
    row = jnp.repeat(jnp.arange(rows, dtype=F32), GRID_W)
    col = jnp.tile(jnp.arange(GRID_W, dtype=F32), rows)
    inv = 1.0 / (ROPE_BASE ** (jnp.arange(0, AXIS_DIM, 2, dtype=F32) / AXIS_DIM))

    def one(pos):
        ang = pos[:, None] * inv[None, :]
        c, s = jnp.cos(ang), jnp.sin(ang)
        return jnp.concatenate([c, c], axis=-1), jnp.concatenate([-s, s], axis=-1)

    cr, sr = one(row)
    cc, sc = one(col)
    return jnp.concatenate([cr, cc], axis=-1), jnp.concatenate([sr, sc], axis=-1)


def _attn_kernel(*refs, heads, ck, n_ctx):
    lq1_ref, lk1_ref, lq2_ref, lk2_ref, g_ref, q_ref, k_ref, v_ref = refs[:8]
    if n_ctx:
        ck_ref, cv_ref, cosq_ref, sinq_ref, cosk_ref, sink_ref, o_ref, kb_ref, vb_ref = refs[8:]
        lane = lax.broadcasted_iota(jnp.int32, (1, LANES), 1)
        first_half = (lane % AXIS_DIM) < (AXIS_DIM // 2)

        @pl.when(pl.program_id(2) == 0)
        def _():
            n_new = k_ref.shape[0]
            kb_ref[0:n_ctx, :] = ck_ref[...].astype(BF16)
            vb_ref[0:n_ctx, :] = cv_ref[...].astype(BF16)
            vb_ref[n_ctx:n_ctx + n_new, :] = v_ref[...].astype(BF16)
            step = _pick(n_new, (ROPE_ROWS,))
            for r0 in range(0, n_new, step):
                rows = slice(r0, r0 + step)
                for c0 in range(0, heads * V_HD, LANES):
                    cols = slice(c0, c0 + LANES)
                    roped = _rope_tile(k_ref[rows, cols], cosk_ref[rows, :], sink_ref[rows, :], first_half)
                    kb_ref[n_ctx + r0:n_ctx + r0 + step, cols] = roped.astype(BF16)

        k_src, v_src = kb_ref, vb_ref
    else:
        o_ref = refs[8]
        k_src, v_src = k_ref, v_ref

    lam = (jnp.exp(jnp.sum(lq1_ref[...] * lk1_ref[...], axis=-1, keepdims=True))
           - jnp.exp(jnp.sum(lq2_ref[...] * lk2_ref[...], axis=-1, keepdims=True)) + LAM_INIT)
    c2 = (QK_HD ** -0.5) * math.log2(math.e)
    nck = k_src.shape[0] // ck
    for h in range(heads):
        base = h * V_HD
        outs = []
        for half in range(2):
            cols = slice(base + half * QK_HD, base + (half + 1) * QK_HD)
            q = q_ref[:, cols]
            if n_ctx:
                q = _rope_tile(q, cosq_ref[...], sinq_ref[...], first_half)
            q = (q * c2).astype(BF16)
            m = l = acc = None
            for c in range(nck):
                rows = slice(c * ck, (c + 1) * ck)
                k = k_src[rows, cols].astype(BF16)
                s = lax.dot_general(q, k, (((1,), (1,)), ((), ())), preferred_element_type=F32)
                mc = jnp.max(s, axis=-1, keepdims=True)
                m_new = mc if c == 0 else jnp.maximum(m, mc)
                e = jnp.exp2(s - m_new)
                ls = jnp.sum(e, axis=-1, keepdims=True)
                pv = jnp.dot(e.astype(BF16), v_src[rows, base:base + V_HD].astype(BF16),
                             preferred_element_type=F32)
                if c == 0:
                    l, acc = ls, pv
                else:
                    alpha = jnp.exp2(m - m_new)
                    l = alpha * l + ls
                    acc = alpha * acc + pv
                m = m_new
            outs.append(acc * (1.0 / l))
        o = outs[0] - lam * outs[1]
        o_ref[:, base:base + V_HD] = ((_rms_rows(o) * g_ref[...]) * (1.0 - LAM_INIT)).astype(BF16)


def _attention(proj, w, lams, g_subln, heads_per_step, tq, ctx_k=None, ctx_v=None, tables=None):
    b, l, _ = proj.shape
    hw = heads_per_step * V_HD
    nh = w // hw
    n_ctx = 0 if ctx_k is None else ctx_k.shape[1]
    ck = _pick(n_ctx + l, (768, 1536, 1024, 512, 256))
    small = pl.BlockSpec((1, QK_HD), lambda bi, hi, qi: (0, 0))
    in_specs = [small, small, small, small,
                pl.BlockSpec((1, V_HD), lambda bi, hi, qi: (0, 0)),
                pl.BlockSpec((None, tq, hw), lambda bi, hi, qi: (bi, qi, hi)),
                pl.BlockSpec((None, l, hw), lambda bi, hi, qi: (bi, 0, nh + hi)),
                pl.BlockSpec((None, l, hw), lambda bi, hi, qi: (bi, 0, 2 * nh + hi))]
    args = [*lams, g_subln.reshape(1, V_HD), proj, proj, proj]
    scratch = []
    if n_ctx:
        ctx_spec = pl.BlockSpec((None, n_ctx, hw), lambda bi, hi, qi: (bi, 0, hi))
        q_tab = pl.BlockSpec((tq, LANES), lambda bi, hi, qi: (qi, 0))
        k_tab = pl.BlockSpec((l, LANES), lambda bi, hi, qi: (0, 0))
        in_specs += [ctx_spec, ctx_spec, q_tab, q_tab, k_tab, k_tab]
        args += [ctx_k, ctx_v, tables[0], tables[1], tables[0], tables[1]]
        scratch = [pltpu.VMEM((n_ctx + l, hw), BF16), pltpu.VMEM((n_ctx + l, hw), BF16)]
    return pl.pallas_call(
        functools.partial(_attn_kernel, heads=heads_per_step, ck=ck, n_ctx=n_ctx),
        grid=(b, nh, l // tq),
        in_specs=in_specs,
        out_specs=pl.BlockSpec((None, tq, hw), lambda bi, hi, qi: (bi, qi, hi)),
        out_shape=jax.ShapeDtypeStruct((b, l, w), BF16),
        scratch_shapes=scratch,
        compiler_params=_params(("arbitrary", "arbitrary", "arbitrary")),
        name="diff_attn",
    )(*args)


def _conv_silu(win_ref, prev_ref, main_ref, next_ref, w_ref, b_ref, has_prev, has_next):
    n = main_ref.shape[0]
    win_ref[0:CONV_HALO, :] = jnp.where(has_prev, prev_ref[...], 0.0)
    win_ref[CONV_HALO:CONV_HALO + n, :] = main_ref[...]
    win_ref[CONV_HALO + n:CONV_HALO + n + CONV_HALO, :] = jnp.where(has_next, next_ref[...], 0.0)
    first = CONV_HALO - CONV_K // 2
    out = b_ref[...]
    for j in range(CONV_K):
        out = out + w_ref[j:j + 1, :] * win_ref[first + j:first + j + n, :]
    return _silu(out)


def _decay_terms(dt_raw, dt_bias, a_log, heads):
    dt = _softplus(dt_raw + dt_bias)
    a = dt * (-jnp.exp(a_log))
    row = lax.broadcasted_iota(jnp.int32, (CHUNK, LANES), 0)
    cs = a
    step = 1
    while step < CHUNK:
        cs = cs + jnp.where(row >= step, pltpu.roll(cs, step, axis=0), 0.0)
        step *= 2
    total = cs[CHUNK - 1:CHUNK, :]
    lane = lax.broadcasted_iota(jnp.int32, (CHUNK, LANES), 1)
    cum = jnp.where(lane < heads, cs, total - cs + a)
    cum_t = cum.T
    dt_t = dt.T
    tot_t = jnp.broadcast_to(total, (CHUNK, LANES)).T
    log2e = math.log2(math.e)
    return (cum * log2e,
            cum_t * log2e,
            dt_t,
            jnp.exp(tot_t - cum_t) * dt_t,
            jnp.exp(cum_t),
            jnp.exp(tot_t))


def _conv_kernel(xs_p, xs_m, xs_n, bc_p, bc_m, bc_n, cwx_ref, cbx_ref, cwbc_ref, cbbc_ref,
                 xo_ref, bco_ref, winx_ref, winbc_ref, *, nt):
    i = pl.program_id(1)
    has_prev = i > 0
    has_next = i < nt - 1
    xo_ref[...] = _conv_silu(winx_ref, xs_p, xs_m, xs_n, cwx_ref, cbx_ref, has_prev, has_next)
    bco_ref[...] = _conv_silu(winbc_ref, bc_p, bc_m, bc_n, cwbc_ref, cbbc_ref, has_prev, has_next).astype(BF16)


def _conv(proj, p, dx, nbc, xc, bcc):
    b, l, _ = proj.shape
    tr = _pick(l, (256, 128))
    nt = l // tr
    hpt = tr // CONV_HALO
    nhalo = l // CONV_HALO

    def main(wd, col):
        return pl.BlockSpec((None, tr, wd), lambda bi, i: (bi, i, col))

    def halo_prev(wd, col):
        return pl.BlockSpec((None, CONV_HALO, wd), lambda bi, i: (bi, jnp.maximum(i * hpt - 1, 0), col))

    def halo_next(wd, col):
        return pl.BlockSpec((None, CONV_HALO, wd),
                            lambda bi, i: (bi, jnp.minimum((i + 1) * hpt, nhalo - 1), col))

    def full(shape):
        return pl.BlockSpec(shape, lambda bi, i: (0,) * len(shape))

    return pl.pallas_call(
        functools.partial(_conv_kernel, nt=nt),
        grid=(b, nt),
        in_specs=[halo_prev(dx, xc), main(dx, xc), halo_next(dx, xc),
                  halo_prev(nbc, bcc), main(nbc, bcc), halo_next(nbc, bcc),
                  full((CONV_K, dx)), full((1, dx)), full((CONV_K, nbc)), full((1, nbc))],
        out_specs=[pl.BlockSpec((None, tr, dx), lambda bi, i: (bi, i, 0)),
                   pl.BlockSpec((None, tr, nbc), lambda bi, i: (bi, i, 0))],
        out_shape=[jax.ShapeDtypeStruct((b, l, dx), F32), jax.ShapeDtypeStruct((b, l, nbc), BF16)],
        scratch_shapes=[pltpu.VMEM((tr + 2 * CONV_HALO, dx), F32),
                        pltpu.VMEM((tr + 2 * CONV_HALO, nbc), F32)],
        compiler_params=_params(("arbitrary", "arbitrary")),
        name="conv_silu",
    )(proj, proj, proj, proj, proj, proj, p['conv_w_x'], p['conv_b_x'], p['conv_w_bc'], p['conv_b_bc'])


def _ssd_kernel(*refs, nc, heads, has_init):
    xs_ref, bc_ref, dt_ref, z_ref, alog_ref, dtb_ref, dskip_ref, gn_ref = refs[:8]
    idx = 8
    if has_init:
        initf_ref, initb_ref = refs[idx], refs[idx + 1]
        idx += 2
    y_ref, sf_ref, sb_ref, carf_ref, carb_ref, prev_ref = refs[idx:idx + 6]

    phase = pl.program_id(1)
    i = pl.program_id(2)
    c = jnp.where(phase == 0, i, nc - 1 - i)
    gw = heads // N_GROUPS * SSM_HD
    hpg = heads // N_GROUPS

    @pl.when((phase == 0) & (i == 0))
    def _():
        if has_init:
            carf_ref[...] = initf_ref[...]
            carb_ref[...] = initb_ref[...]
        else:
            carf_ref[...] = jnp.zeros_like(carf_ref)
            carb_ref[...] = jnp.zeros_like(carb_ref)

    row = lax.broadcasted_iota(jnp.int32, (CHUNK, CHUNK), 0)
    colm = lax.broadcasted_iota(jnp.int32, (CHUNK, CHUNK), 1)
    cum2, cum2_t, dt_t, w_t, e_t, dec_t = _decay_terms(dt_ref[...], dtb_ref[...], alog_ref[...], heads)

    xs = xs_ref[...]
    bc = bc_ref[...]
    x_t = xs.T

    def state_update(car_ref, off):
        for g in range(N_GROUPS):
            b_g = bc[:, g * D_STATE:(g + 1) * D_STATE].astype(BF16)
            rows = []
            for hh in range(hpg):
                h = g * hpg + hh
                rows.append(x_t[h * SSM_HD:(h + 1) * SSM_HD, :] * w_t[off + h:off + h + 1, :])
            xw = jnp.concatenate(rows, axis=0).astype(BF16)
            st = jnp.dot(xw, b_g, preferred_element_type=F32)
            for hh in range(hpg):
                h = g * hpg + hh
                r = slice(h * SSM_HD, (h + 1) * SSM_HD)
                car_ref[r, :] = car_ref[r, :] * dec_t[off + h:off + h + 1, :] + st[hh * SSM_HD:(hh + 1) * SSM_HD, :]

    @pl.when(phase == 0)
    def _():
        prev_ref[c] = carf_ref[...].astype(BF16)
        state_update(carf_ref, 0)

        @pl.when(i == nc - 1)
        def _():
            sf_ref[...] = carf_ref[...]

    @pl.when(phase == 1)
    def _():
        causal = colm >= row
        anti = colm <= row
        y_parts = []
        for g in range(N_GROUPS):
            b_g = bc[:, g * D_STATE:(g + 1) * D_STATE].astype(BF16)
            c_g = bc[:, (N_GROUPS + g) * D_STATE:(N_GROUPS + g + 1) * D_STATE].astype(BF16)
            cb_t = lax.dot_general(b_g, c_g, (((1,), (1,)), ((), ())), preferred_element_type=F32)
            rg = slice(g * gw, (g + 1) * gw)
            yo_f = lax.dot_general(prev_ref[c, rg, :], c_g, (((1,), (1,)), ((), ())),
                                   preferred_element_type=F32)
            yo_b = lax.dot_general(carb_ref[rg, :].astype(BF16), c_g, (((1,), (1,)), ((), ())),
                                   preferred_element_type=F32)
            for hh in range(hpg):
                h = g * hpg + hh
                hb = heads + h
                x_h = x_t[h * SSM_HD:(h + 1) * SSM_HD, :]
                seg_f = cum2_t[h:h + 1, :] - cum2[:, h:h + 1]
                seg_b = cum2_t[hb:hb + 1, :] - cum2[:, hb:hb + 1]
                w_f = cb_t * jnp.exp2(jnp.where(causal, seg_f, -1e4))
                w_b = cb_t * jnp.exp2(jnp.where(anti, seg_b, -1e4))
                lhs = jnp.concatenate([x_h * dt_t[h:h + 1, :], x_h * dt_t[hb:hb + 1, :]], axis=1)
                rhs = jnp.concatenate([w_f, w_b], axis=0)
                yd = jnp.dot(lhs.astype(BF16), rhs.astype(BF16), preferred_element_type=F32)
                r = slice(hh * SSM_HD, (hh + 1) * SSM_HD)
                y_parts.append(yd + yo_f[r, :] * e_t[h:h + 1, :] + yo_b[r, :] * e_t[hb:hb + 1, :])
        y = jnp.concatenate(y_parts, axis=0).T
        y = y + dskip_ref[...] * xs
        y = y * _silu(z_ref[...])
        for g in range(N_GROUPS):
            cols = slice(g * gw, (g + 1) * gw)
            y_ref[:, cols] = (_rms_rows(y[:, cols]) * gn_ref[:, cols]).astype(BF16)
        state_update(carb_ref, heads)

        @pl.when(i == nc - 1)
        def _():
            sb_ref[...] = carb_ref[...]


def _ssd(xs, bc, dt, proj, zc, p, init_f, init_b):
    b, l, dx = xs.shape
    nbc = bc.shape[2]
    heads = dx // SSM_HD
    nc = l // CHUNK
    has_init = init_f is not None

    def chunk_of(ph, i):
        return jnp.where(ph == 0, i, nc - 1 - i)

    def main(wd, col):
        return pl.BlockSpec((None, CHUNK, wd), lambda bi, ph, i: (bi, chunk_of(ph, i), col))

    def full(shape):
        return pl.BlockSpec(shape, lambda bi, ph, i: (0,) * len(shape))

    state_spec = pl.BlockSpec((None, dx, D_STATE), lambda bi, ph, i: (bi, 0, 0))
    in_specs = [main(dx, 0), main(nbc, 0), main(LANES, 0), main(dx, zc),
                full((1, LANES)), full((1, LANES)), full((1, dx)), full((1, dx))]
    args = [xs, bc, dt, proj, p['a_log'], p['dt_bias'], p['d_skip'], p['g_ssm_norm']]
    if has_init:
        in_specs += [state_spec, state_spec]
        args += [init_f, init_b]
    y_spec = pl.BlockSpec((None, CHUNK, dx), lambda bi, ph, i: (bi, jnp.where(ph == 0, nc - 1, nc - 1 - i), 0))
    return pl.pallas_call(
        functools.partial(_ssd_kernel, nc=nc, heads=heads, has_init=has_init),
        grid=(b, 2, nc),
        in_specs=in_specs,
        out_specs=[y_spec, state_spec, state_spec],
        out_shape=[jax.ShapeDtypeStruct((b, l, dx), BF16),
                   jax.ShapeDtypeStruct((b, dx, D_STATE), F32),
                   jax.ShapeDtypeStruct((b, dx, D_STATE), F32)],
        scratch_shapes=[pltpu.VMEM((dx, D_STATE), F32), pltpu.VMEM((dx, D_STATE), F32),
                        pltpu.VMEM((nc, dx, D_STATE), BF16)],
        compiler_params=_params(("arbitrary", "arbitrary", "arbitrary")),
        name="ssd_scan",
    )(*args)


def _out_proj_kernel(a_ref, y_ref, wa_ref, wy_ref, o_ref):
    o_ref[...] = (jnp.dot(a_ref[...], wa_ref[...], preferred_element_type=F32)
                  + jnp.dot(y_ref[...], wy_ref[...], preferred_element_type=F32))


def _out_proj(att, y, w_out):
    m, wa = att.shape
    d = w_out.shape[1]
    assert y.shape[1] == wa and w_out.shape[0] == 2 * wa
    tm = _pick(m, (1024, 512, 256, 128))
    tn = _pick(d, (1024, 512, 256, 128))
    return pl.pallas_call(
        _out_proj_kernel,
        grid=(m // tm, d // tn),
        in_specs=[pl.BlockSpec((tm, wa), lambda i, j: (i, 0)),
                  pl.BlockSpec((tm, wa), lambda i, j: (i, 0)),
                  pl.BlockSpec((wa, tn), lambda i, j: (0, j)),
                  pl.BlockSpec((wa, tn), lambda i, j: (1, j))],
        out_specs=pl.BlockSpec((tm, tn), lambda i, j: (i, j)),
        out_shape=jax.ShapeDtypeStruct((m, d), F32),
        compiler_params=_params(("arbitrary", "arbitrary")),
        name="out_proj",
    )(att, y, w_out, w_out)


def _sandwich_kernel(mix_ref, x_ref, gpost_ref, gate_ref, gpre_ref, sc_ref, sh_ref,
                     x1_ref, h2_ref, stat_ref, stat2_ref):
    chunks = _col_chunks(x1_ref.shape[1])
    _store_inv_rms(mix_ref, stat_ref)

    def residual(r, carry):
        rows = _rows(r, BF16_ROWS)
        inv = stat_ref[rows, 0:1]
        for cols in chunks:
            n = (mix_ref[rows, cols] * inv) * gpost_ref[:, cols]
            x1_ref[rows, cols] = x_ref[rows, cols] + gate_ref[:, cols] * n
        return carry

    lax.fori_loop(0, x1_ref.shape[0] // BF16_ROWS, residual, 0, unroll=2)
    _store_inv_rms(x1_ref, stat2_ref)

    def prenorm(r, carry):
        rows = _rows(r, BF16_ROWS)
        inv = stat2_ref[rows, 0:1]
        for cols in chunks:
            y = (x1_ref[rows, cols] * inv) * gpre_ref[:, cols]
            h2_ref[rows, cols] = (y * (1.0 + sc_ref[:, cols]) + sh_ref[:, cols]).astype(BF16)
        return carry

    lax.fori_loop(0, x1_ref.shape[0] // BF16_ROWS, prenorm, 0, unroll=2)


def _sandwich(mix, x, mod, g_post, g_pre, mod_rows):
    m, d = x.shape
    tm = _row_tile(m, mod_rows, (256, 128))
    vec = pl.BlockSpec((1, d), lambda i, j: (0, 0))
    tile = pl.BlockSpec((tm, d), lambda i, j: (i, 0))
    return pl.pallas_call(
        _sandwich_kernel,
        grid=(m // tm, 1),
        in_specs=[tile, tile, vec, _mod_spec(d, 2, mod_rows, tm), vec,
                  _mod_spec(d, 4, mod_rows, tm), _mod_spec(d, 3, mod_rows, tm)],
        out_specs=[tile, tile],
        out_shape=[jax.ShapeDtypeStruct((m, d), F32), jax.ShapeDtypeStruct((m, d), BF16)],
        scratch_shapes=[pltpu.VMEM((tm, LANES), F32), pltpu.VMEM((tm, LANES), F32)],
        compiler_params=_params(("arbitrary", "arbitrary")),
        name="mix_sandwich",
    )(mix, x, g_post.reshape(1, d), mod, g_pre.reshape(1, d), mod, mod)


def _up_kernel(h_ref, w_ref, *refs):
    if len(refs) == 3:
        src_ref, o_ref, dst_ref = refs
        dst_ref[...] = src_ref[...].astype(BF16)
    else:
        o_ref, = refs
    u = jnp.maximum(jnp.dot(h_ref[...], w_ref[...], preferred_element_type=F32), 0.0)
    o_ref[...] = (u * u).astype(BF16)


def _up(h2, w_up, cast_src=None):
    m, d = h2.shape
    f = w_up.shape[1]
    tm = _pick(m, (1024, 512, 256, 128))
    tn = _pick(f, (1024, 512, 256, 128))
    grid = (m // tm, f // tn)
    in_specs = [pl.BlockSpec((tm, d), lambda i, j: (i, 0)),
                pl.BlockSpec((d, tn), lambda i, j: (0, j))]
    out_specs = [pl.BlockSpec((tm, tn), lambda i, j: (i, j))]
    out_shape = [jax.ShapeDtypeStruct((m, f), BF16)]
    args = [h2, w_up]
    side = None if cast_src is None else _side_cast(cast_src, grid)
    if side is not None:
        in_specs.append(side[0])
        out_specs.append(side[0])
        out_shape.append(side[1])
        args.append(cast_src)
    outs = pl.pallas_call(
        _up_kernel,
        grid=grid,
        in_specs=in_specs,
        out_specs=out_specs,
        out_shape=out_shape,
        compiler_params=_params(("arbitrary", "arbitrary")),
        name="mlp_up",
    )(*args)
    if side is not None:
        return outs[0], outs[1]
    return outs[0], (None if cast_src is None else cast_src.astype(BF16))


def _down_kernel(u_ref, w_ref, x_ref, gpost_ref, gate_ref, o_ref, stat_ref, *, nk):
    k = pl.program_id(1)

    @pl.when(k == 0)
    def _():
        o_ref[...] = jnp.dot(u_ref[...], w_ref[...], preferred_element_type=F32)

    @pl.when(k > 0)
    def _():
        o_ref[...] += jnp.dot(u_ref[...], w_ref[...], preferred_element_type=F32)

    @pl.when(k == nk - 1)
    def _():
        _store_inv_rms(o_ref, stat_ref)

        def body(r, carry):
            rows = _rows(r, BF16_ROWS)
            inv = stat_ref[rows, 0:1]
            for cols in _col_chunks(o_ref.shape[1]):
                n = (o_ref[rows, cols] * inv) * gpost_ref[:, cols]
                o_ref[rows, cols] = x_ref[rows, cols] + gate_ref[:, cols] * n
            return carry

        lax.fori_loop(0, o_ref.shape[0] // BF16_ROWS, body, 0, unroll=2)


def _down(u, w_down, x1, mod, g_post, mod_rows):
    m, f = u.shape
    d = x1.shape[1]
    tm = _row_tile(m, mod_rows, (1024, 512, 256, 128))
    tk = _pick(f, (512, 256, 128))
    nk = f // tk
    return pl.pallas_call(
        functools.partial(_down_kernel, nk=nk),
        grid=(m // tm, nk),
        in_specs=[pl.BlockSpec((tm, tk), lambda i, k: (i, k)),
                  pl.BlockSpec((tk, d), lambda i, k: (k, 0)),
                  pl.BlockSpec((tm, d), lambda i, k: (i, 0), pipeline_mode=pl.Buffered(1)),
                  pl.BlockSpec((1, d), lambda i, k: (0, 0)),
                  _mod_spec(d, 5, mod_rows, tm)],
        out_specs=pl.BlockSpec((tm, d), lambda i, k: (i, 0), pipeline_mode=pl.Buffered(1)),
        out_shape=jax.ShapeDtypeStruct((m, d), F32),
        scratch_shapes=[pltpu.VMEM((tm, LANES), F32)],
        compiler_params=_params(("arbitrary", "arbitrary")),
        name="mlp_down",
    )(u, w_down, x1, g_post.reshape(1, d), mod)


def _kv_out_kernel(ki_ref, vi_ref, k_ref, v_ref, *, n_att):
    for hh in range(n_att):
        for j in range(2):
            c0 = (hh * 2 + j) * QK_HD
            k_ref[:, hh, j, :] = ki_ref[:, c0:c0 + QK_HD]
        v_ref[:, hh, :] = vi_ref[:, hh * V_HD:(hh + 1) * V_HD]


def _kv_out(proj3, att_w):
    b, l, _ = proj3.shape
    n_att = att_w // V_HD
    tl = _pick(l, (256, 128))
    return pl.pallas_call(
        functools.partial(_kv_out_kernel, n_att=n_att),
        grid=(b, l // tl),
        in_specs=[pl.BlockSpec((None, tl, att_w), lambda bi, i: (bi, i, 1)),
                  pl.BlockSpec((None, tl, att_w), lambda bi, i: (bi, i, 2))],
        out_specs=[pl.BlockSpec((None, None, tl, n_att, 2, QK_HD), lambda bi, i: (bi, 0, i, 0, 0, 0)),
                   pl.BlockSpec((None, None, tl, n_att, V_HD), lambda bi, i: (bi, 0, i, 0, 0))],
        out_shape=[jax.ShapeDtypeStruct((b, 1, l, n_att, 2, QK_HD), F32),
                   jax.ShapeDtypeStruct((b, 1, l, n_att, V_HD), F32)],
        compiler_params=_params(("arbitrary", "arbitrary")),
        name="kv_out",
    )(proj3, proj3)


def _trunk(x, mod, mod_row0, rows_per_mod, w, ctx_k, ctx_v, init_f, init_b):
    b, l, d = x.shape
    m = b * l
    att_w = d // 2
    d_inner = d - att_w
    n_bc = 2 * N_GROUPS * D_STATE
    latent = ctx_k is not None

    mod_rows = (mod_row0, rows_per_mod)
    x2d = x.reshape(m, d)
    assert att_w == d_inner and (3 * att_w + 2 * d_inner) % n_bc == 0
    h, dt = _prenorm(x2d, mod, w['g_mix_pre'], w['w_in_dt'], mod_rows)
    w_up, w_down = w['w_up'], w['w_down']
    proj, w_up_cast = _in_proj(h, w['w_in'], 3 * att_w + 2 * d_inner + n_bc,
                               w_up if w_up.dtype == F32 else None)
    w_up = w_up if w_up_cast is None else w_up_cast
    proj3 = proj.reshape(b, l, -1)
    lams = (w['lq1'], w['lk1'], w['lq2'], w['lk2'])
    if latent:
        att = _attention(proj3, att_w, lams, w['g_subln'], 1, _pick(l, (512, 256, 128)),
                         ctx_k, ctx_v, _rope_tables(l))
    else:
        att = _attention(proj3, att_w, lams, w['g_subln'], att_w // V_HD, l)
    xs, bc = _conv(proj3, w, d_inner, n_bc, 4, (3 * att_w + 2 * d_inner) // n_bc)
    y, fin_f, fin_b = _ssd(xs, bc, dt.reshape(b, l, LANES), proj3, 3, w, init_f, init_b)
    mix = _out_proj(att.reshape(m, att_w), y.reshape(m, d_inner), w['w_out'])
    x1, h2 = _sandwich(mix, x2d, mod, w['g_mix_post'], w['g_mlp_pre'], mod_rows)
    u, w_down_cast = _up(h2, w_up, w_down if w_down.dtype == F32 else None)
    w_down = w_down if w_down_cast is None else w_down_cast
    x2 = _down(u, w_down, x1, mod, w['g_mlp_post'], mod_rows)
    k_c, v_c = (None, None) if latent else _kv_out(proj3, att_w)
    return x2.reshape(b, l, d), k_c, v_c, fin_f, fin_b, dict(w, w_up=w_up, w_down=w_down)


def kernel(x_prompt, x_sample, c, cache_k, cache_v, state_ssm_fwd, state_ssm_bwd, c_ctx, w_ada, b_ada,
           g_mix_pre, g_mix_post, g_mlp_pre, g_mlp_post, w_in, lambda_q1, lambda_k1, lambda_q2, lambda_k2,
           g_subln, conv_w, conv_b, a_log, dt_bias, d_skip, g_ssm_norm, w_out, w_up, w_down):
    bp, lp, d = x_prompt.shape
    bd, ld, _ = x_sample.shape
    depth = w_in.shape[0]
    assert depth == 1 and bd + 1 <= MOD_ROWS
    att_w = d // 2
    d_inner = d - att_w
    heads = d_inner // SSM_HD
    n_att = att_w // V_HD
    n_main = 3 * att_w + 2 * d_inner + 2 * N_GROUPS * D_STATE
    assert 2 * heads <= LANES and w_in.shape[2] == n_main + 2 * heads

    lyr = 0
    pad_l = LANES - 2 * heads
    w = dict(
        g_mix_pre=g_mix_pre[lyr], g_mix_post=g_mix_post[lyr], g_mlp_pre=g_mlp_pre[lyr],
        g_mlp_post=g_mlp_post[lyr],
        w_in=w_in[lyr].astype(BF16),
        w_in_dt=jnp.pad(w_in[lyr, :, n_main:], ((0, 0), (0, pad_l))).astype(BF16),
        lq1=lambda_q1[lyr].reshape(1, QK_HD), lk1=lambda_k1[lyr].reshape(1, QK_HD),
        lq2=lambda_q2[lyr].reshape(1, QK_HD), lk2=lambda_k2[lyr].reshape(1, QK_HD),
        g_subln=g_subln[lyr],
        conv_w_x=conv_w[lyr, :, :d_inner], conv_w_bc=conv_w[lyr, :, d_inner:],
        conv_b_x=conv_b[lyr, :d_inner].reshape(1, -1), conv_b_bc=conv_b[lyr, d_inner:].reshape(1, -1),
        a_log=jnp.pad(a_log[lyr].reshape(1, -1), ((0, 0), (0, pad_l))),
        dt_bias=jnp.pad(dt_bias[lyr].reshape(1, -1), ((0, 0), (0, pad_l))),
        d_skip=jnp.repeat(d_skip[lyr], SSM_HD).reshape(1, -1),
        g_ssm_norm=g_ssm_norm[lyr].reshape(1, -1),
        w_out=w_out[lyr].astype(BF16), w_up=w_up[lyr], w_down=w_down[lyr],
    )

    cvec = jnp.concatenate([c_ctx[None, :], c, jnp.zeros((MOD_ROWS - 1 - bd, d), F32)], axis=0)
    mod = _ada(cvec, w_ada[lyr], b_ada[lyr]).reshape(MOD_ROWS, 6, 1, d)

    ys, _, _, _, _, w = _trunk(
        x_sample, mod, 1, ld, w,
        cache_k[:, lyr].reshape(bd, -1, att_w), cache_v[:, lyr].reshape(bd, -1, att_w),
        state_ssm_fwd[:, lyr].reshape(bd, d_inner, D_STATE), state_ssm_bwd[:, lyr].reshape(bd, d_inner, D_STATE))
    yp, k_c, v_c, sf, sb, _ = _trunk(x_prompt, mod, 0, bp * lp, w, None, None, None, None)

    return (yp, ys, k_c, v_c,
            sf.reshape(bp, 1, heads, SSM_HD, D_STATE), sb.reshape(bp, 1, heads, SSM_HD, D_STATE))
```
